```python
import jax, jax.numpy as jnp
from jax import lax
import numpy as np

D_MODEL = 1024
BATCH = 4
SEQ = 8192
DEPTH = 1

MIX_WIDTH = D_MODEL
POOL_WIDTH = MIX_WIDTH // 2
SGU_WIDTH = MIX_WIDTH - POOL_WIDTH
POOL_WINDOWS = (2, 4, 8, 16)
POOL_GROUPS = len(POOL_WINDOWS)
POOL_GROUP_DIM = POOL_WIDTH // POOL_GROUPS
SGU_CHUNK = 128
SGU_HEADS = 4
SGU_HEAD_DIM = SGU_WIDTH // SGU_HEADS
IN_PROJ_WIDTH = POOL_WIDTH + 2 * SGU_WIDTH
PEER_HEADS = 8
PEER_N_KEYS = 128
PEER_N_EXPERTS = PEER_N_KEYS * PEER_N_KEYS
PEER_D_KEY = 256
PEER_HALF = PEER_D_KEY // 2
PEER_TOPK = 16
PEER_TOKEN_BLOCK = 128
N_MOD = 6
EPS = 1e-6

kernel_name = "hybrid_pool_sgu_peer_adaln"


def rms_norm(x, g):
    xf = x.astype(jnp.float32)
    y = xf * lax.rsqrt(jnp.mean(xf * xf, axis=-1, keepdims=True) + EPS)
    return (y * g.astype(jnp.float32)).astype(x.dtype)


def layer_norm(x, g, b):
    xf = x.astype(jnp.float32)
    mu = jnp.mean(xf, axis=-1, keepdims=True)
    xc = xf - mu
    y = xc * lax.rsqrt(jnp.mean(xc * xc, axis=-1, keepdims=True) + EPS)
    return (y * g.astype(jnp.float32) + b.astype(jnp.float32)).astype(x.dtype)


def modulate(h, shift, scale):
    return h * (1 + scale) + shift


def pool_mixer(a, w_pool, pool_scale):
    B, S, _ = a.shape
    af = a.astype(jnp.float32).reshape(B, S, POOL_GROUPS, POOL_GROUP_DIM)
    cs = jnp.cumsum(af, axis=1)
    pos = jnp.arange(1, S + 1, dtype=jnp.float32)
    outs = []
    for g, w in enumerate(POOL_WINDOWS):
        c_g = cs[:, :, g]
        prev = jnp.pad(c_g, ((0, 0), (w, 0), (0, 0)))[:, :S]
        mean = (c_g - prev) / jnp.minimum(pos, float(w))[None, :, None]
        outs.append(mean - af[:, :, g])
    pooled = jnp.stack(outs, axis=2)
    y = jnp.einsum('bsgc,gcd->bsgd', pooled, w_pool.astype(jnp.float32))
    y = y.reshape(B, S, POOL_WIDTH) * pool_scale.astype(jnp.float32)
    return y.astype(a.dtype)


def spatial_gating(u, v, ln_g, ln_b, w_spatial, b_spatial):
    B, S, _ = u.shape
    n = S // SGU_CHUNK
    vn = layer_norm(v, ln_g, ln_b).astype(jnp.float32)
    vn = vn.reshape(B, n, SGU_CHUNK, SGU_HEADS, SGU_HEAD_DIM)
    mask = jnp.tril(jnp.ones((SGU_CHUNK, SGU_CHUNK), dtype=bool))
    w = jnp.where(mask[None], w_spatial.astype(jnp.float32), 0.0)
    mixed = jnp.einsum('hts,bnshc->bnthc', w, vn)
    mixed = mixed + b_spatial.astype(jnp.float32).T[None, None, :, :, None]
    uf = u.astype(jnp.float32).reshape(B, n, SGU_CHUNK, SGU_HEADS, SGU_HEAD_DIM)
    return (uf * mixed).reshape(B, S, SGU_WIDTH).astype(u.dtype)


def peer_ffn(h, w_query, sub_keys, expert_down, expert_up):
    B, S, D = h.shape
    tokens = h.reshape(-1, PEER_TOKEN_BLOCK, D)

    def block(xb):
        T = xb.shape[0]
        q = (xb @ w_query).astype(jnp.float32).reshape(T, PEER_HEADS, 2, PEER_HALF)
        scores = jnp.einsum('thpd,hpkd->thpk', q, sub_keys.astype(jnp.float32))
        s_top, i_top = lax.top_k(scores, PEER_TOPK)
        cand = s_top[:, :, 0, :, None] + s_top[:, :, 1, None, :]
        cand_idx = i_top[:, :, 0, :, None] * PEER_N_KEYS + i_top[:, :, 1, None, :]
        cand = cand.reshape(T, PEER_HEADS, PEER_TOPK * PEER_TOPK)
        cand_idx = cand_idx.reshape(T, PEER_HEADS, PEER_TOPK * PEER_TOPK)
        best, sel = lax.top_k(cand, PEER_TOPK)
        idx = jnp.take_along_axis(cand_idx, sel, axis=-1)
        gate = jax.nn.softmax(best, axis=-1)
        u_e = expert_down[idx]
        v_e = expert_up[idx]
        act = jax.nn.gelu(jnp.einsum('thkd,td->thk', u_e, xb).astype(jnp.float32), approximate=False)
        return jnp.einsum('thk,thkd->td', (gate * act).astype(xb.dtype), v_e)

    return lax.map(block, tokens).reshape(B, S, D)


def setup_inputs(seed: int = 0) -> dict:
    key = jax.random.key(seed)
    ks = jax.random.split(key, 20)
    f = jnp.float32
    D = D_MODEL
    L = DEPTH
    nrm = lambda k, shape, s: jax.random.normal(k, shape, f) * s
    return {
        "x": nrm(ks[0], (BATCH, SEQ, D), 1.0),
        "c": nrm(ks[1], (BATCH, D), 1.0),
        "w_ada": nrm(ks[2], (L, D, N_MOD * D), 0.5 * D ** -0.5),
        "b_ada": nrm(ks[3], (L, N_MOD * D), 0.02),
        "g_norm1": 1.0 + nrm(ks[4], (L, D), 0.05),
        "w_in": nrm(ks[5], (L, D, IN_PROJ_WIDTH), D ** -0.5),
        "w_pool": nrm(ks[6], (L, POOL_GROUPS, POOL_GROUP_DIM, POOL_GROUP_DIM), POOL_GROUP_DIM ** -0.5),
        "pool_scale": 1.0 + nrm(ks[7], (L, POOL_WIDTH), 0.1),
        "sgu_ln_g": 1.0 + nrm(ks[8], (L, SGU_WIDTH), 0.05),
        "sgu_ln_b": nrm(ks[9], (L, SGU_WIDTH), 0.02),
        "w_spatial": nrm(ks[10], (L, SGU_HEADS, SGU_CHUNK, SGU_CHUNK), SGU_CHUNK ** -0.5),
        "b_spatial": 1.0 + nrm(ks[11], (L, SGU_HEADS, SGU_CHUNK), 0.05),
        "w_out": nrm(ks[12], (L, MIX_WIDTH, D), MIX_WIDTH ** -0.5),
        "g_norm2": 1.0 + nrm(ks[13], (L, D), 0.05),
        "w_query": nrm(ks[14], (L, D, PEER_HEADS * PEER_D_KEY), D ** -0.5),
        "sub_keys": nrm(ks[15], (L, PEER_HEADS, 2, PEER_N_KEYS, PEER_HALF), PEER_HALF ** -0.5),
        "expert_down": nrm(ks[16], (L, PEER_N_EXPERTS, D), D ** -0.5),
        "expert_up": nrm(ks[17], (L, PEER_N_EXPERTS, D), PEER_HEADS ** -0.5),
        "g_final": 1.0 + nrm(ks[18], (D,), 0.05),
    }


def reference(x, c, w_ada, b_ada, g_norm1, w_in, w_pool, pool_scale, sgu_ln_g, sgu_ln_b, w_spatial, b_spatial, w_out, g_norm2, w_query, sub_keys, expert_down, expert_up, g_final):
    c_act = jax.nn.silu(c)
    for l in range(DEPTH):
        mod = jnp.einsum('bd,de->be', c_act, w_ada[l]) + b_ada[l]
        shift1, scale1, gate1, shift2, scale2, gate2 = jnp.split(mod[:, None, :], N_MOD, axis=-1)
        h = modulate(rms_norm(x, g_norm1[l]), shift1, scale1)
        proj = h @ w_in[l]
        a = proj[..., :POOL_WIDTH]
        u = proj[..., POOL_WIDTH:POOL_WIDTH + SGU_WIDTH]
        v = proj[..., POOL_WIDTH + SGU_WIDTH:]
        mixed = jnp.concatenate([
            pool_mixer(a, w_pool[l], pool_scale[l]),
            spatial_gating(u, v, sgu_ln_g[l], sgu_ln_b[l], w_spatial[l], b_spatial[l]),
        ], axis=-1)
        x = x + gate1 * (mixed @ w_out[l])
        h = modulate(rms_norm(x, g_norm2[l]), shift2, scale2)
        x = x + gate2 * peer_ffn(h, w_query[l], sub_keys[l], expert_down[l], expert_up[l])
    return rms_norm(x, g_final)
```

```python
import functools
import math

import jax
import jax.numpy as jnp
import numpy as np
from jax import lax
from jax.experimental import pallas as pl
from jax.experimental.pallas import tpu as pltpu
from jax.experimental.pallas import tpu_sc as plsc

D_MODEL = 1024
POOL_WIDTH = 512
SGU_WIDTH = 512
POOL_WINDOWS = (2, 4, 8, 16)
GROUP_DIM = 128
CHUNK = 128
SGU_HEADS = 4
IN_PROJ_WIDTH = POOL_WIDTH + 2 * SGU_WIDTH
PEER_HEADS = 8
N_KEYS = 128
HALF_KEY = 128
TOPK = 16
N_MOD = 6
EPS = 1e-6

MIX_BLOCK = 512
TOPK_BLOCK = 256
EW_BLOCK = 1024
VMEM_LIMIT_BYTES = 48 * 1024 * 1024

SC_LANES = 16
SC_TOKENS = 32
N_SEL = PEER_HEADS * TOPK

_CAND_A = np.concatenate([np.zeros(16, np.int32), np.repeat(np.arange(1, 8, dtype=np.int32), 8),
                          np.arange(8, 16, dtype=np.int32)])
_CAND_B = np.concatenate([np.arange(16, dtype=np.int32), np.tile(np.arange(8, dtype=np.int32), 7),
                          np.zeros(8, np.int32)])
N_CAND = _CAND_A.shape[0]


def _pool_band():
    t = np.arange(CHUNK)[:, None]
    j = np.arange(2 * CHUNK)[None, :]
    bands = []
    for w in POOL_WINDOWS:
        m = (j > CHUNK + t - w) & (j <= CHUNK + t)
        bands.append(np.where(m, 1.0 / w, 0.0))
    return np.stack(bands).astype(np.float32)


def _bf16_dot(a, b):
    return jnp.dot(a.astype(jnp.bfloat16), b.astype(jnp.bfloat16), preferred_element_type=jnp.float32)


def _ada_kernel(c_ref, w_ref, b_ref, o_ref):
    c = c_ref[...]
    c_act = c * jax.nn.sigmoid(c)
    o_ref[...] = _bf16_dot(c_act, w_ref[...]) + b_ref[...]


def _ada_mod(c_pad, w_ada, b_ada):
    rows = c_pad.shape[0]
    return pl.pallas_call(
        _ada_kernel,
        grid=(N_MOD,),
        in_specs=[
            pl.BlockSpec((rows, D_MODEL), lambda i: (0, 0)),
            pl.BlockSpec((D_MODEL, D_MODEL), lambda i: (0, i)),
            pl.BlockSpec((1, D_MODEL), lambda i: (0, i)),
        ],
        out_specs=pl.BlockSpec((rows, D_MODEL), lambda i: (0, i)),
        out_shape=jax.ShapeDtypeStruct((rows, N_MOD * D_MODEL), jnp.float32),
        name="ada_mod",
    )(c_pad, w_ada, b_ada)


def _rms(x, g):
    return x * lax.rsqrt(jnp.mean(x * x, axis=-1, keepdims=True) + EPS) * g


def _mix_kernel(x_ref, mod_ref, g1_ref, win_ref, band_ref, wpool_ref, pscale_ref, lng_ref, lnb_ref,
                wsp_ref, bsp_ref, wout_ref, g2_ref, x1_ref, h2_ref, aext_ref, mixed_ref):
    j = pl.program_id(1)
    x = x_ref[0]
    shift1, scale1, gate1 = mod_ref[0, 0:1, :], mod_ref[0, 1:2, :], mod_ref[0, 2:3, :]
    shift2, scale2 = mod_ref[0, 3:4, :], mod_ref[0, 4:5, :]

    h = _rms(x, g1_ref[...]) * (1.0 + scale1) + shift1
    proj = _bf16_dot(h, win_ref[...])
    a = proj[:, :POOL_WIDTH]
    u = proj[:, POOL_WIDTH:POOL_WIDTH + SGU_WIDTH]
    v = proj[:, POOL_WIDTH + SGU_WIDTH:]

    @pl.when(j == 0)
    def _():
        aext_ref[0:CHUNK, :] = jnp.zeros((CHUNK, POOL_WIDTH), jnp.float32)

    aext_ref[CHUNK:, :] = a

    mu = jnp.mean(v, axis=-1, keepdims=True)
    vc = v - mu
    vn = vc * lax.rsqrt(jnp.mean(vc * vc, axis=-1, keepdims=True) + EPS) * lng_ref[...] + lnb_ref[...]

    row = lax.broadcasted_iota(jnp.int32, (CHUNK, CHUNK), 0)
    col = lax.broadcasted_iota(jnp.int32, (CHUNK, CHUNK), 1)
    tril = col <= row
    t_col = lax.broadcasted_iota(jnp.int32, (CHUNK, 1), 0)

    for c in range(MIX_BLOCK // CHUNK):
        r0 = c * CHUNK
        pos = (j * MIX_BLOCK + r0 + 1 + t_col).astype(jnp.float32)
        for g, w in enumerate(POOL_WINDOWS):
            c0 = g * GROUP_DIM
            seg = aext_ref[r0:r0 + 2 * CHUNK, c0:c0 + GROUP_DIM]
            hi = seg.astype(jnp.bfloat16)
            lo = (seg - hi.astype(jnp.float32)).astype(jnp.bfloat16)
            band = band_ref[g]
            win = (jnp.dot(band, hi, preferred_element_type=jnp.float32)
                   + jnp.dot(band, lo, preferred_element_type=jnp.float32))
            mean = win * (float(w) / jnp.minimum(pos, float(w)))
            pooled = mean - seg[CHUNK:, :]
            y = _bf16_dot(pooled, wpool_ref[g]) * pscale_ref[:, c0:c0 + GROUP_DIM]
            mixed_ref[r0:r0 + CHUNK, c0:c0 + GROUP_DIM] = y.astype(jnp.bfloat16)
        for hh in range(SGU_HEADS):
            c0 = hh * GROUP_DIM
            wm = jnp.where(tril, wsp_ref[hh], 0.0)
            m = _bf16_dot(wm, vn[r0:r0 + CHUNK, c0:c0 + GROUP_DIM]) + bsp_ref[:, hh:hh + 1]
            s = u[r0:r0 + CHUNK, c0:c0 + GROUP_DIM] * m
            mixed_ref[r0:r0 + CHUNK, POOL_WIDTH + c0:POOL_WIDTH + c0 + GROUP_DIM] = s.astype(jnp.bfloat16)

    aext_ref[0:CHUNK, :] = aext_ref[MIX_BLOCK:MIX_BLOCK + CHUNK, :]

    x1 = x + gate1 * jnp.dot(mixed_ref[...], wout_ref[...], preferred_element_type=jnp.float32)
    x1_ref[0] = x1
    h2_ref[0] = _rms(x1, g2_ref[...]) * (1.0 + scale2) + shift2


def _mixer(x, mod3, g1, w_in, band, w_pool, pool_scale, ln_g, ln_b, w_sp, b_sp_t, w_out, g2):
    B, S, D = x.shape
    const2 = lambda b, j: (0, 0)
    const3 = lambda b, j: (0, 0, 0)
    tok = pl.BlockSpec((1, MIX_BLOCK, D), lambda b, j: (b, j, 0))
    return pl.pallas_call(
        _mix_kernel,
        grid=(B, S // MIX_BLOCK),
        in_specs=[
            tok,
            pl.BlockSpec((1, N_MOD, D), lambda b, j: (b, 0, 0)),
            pl.BlockSpec((1, D), const2),
            pl.BlockSpec((D, IN_PROJ_WIDTH), const2),
            pl.BlockSpec((len(POOL_WINDOWS), CHUNK, 2 * CHUNK), const3),
            pl.BlockSpec((len(POOL_WINDOWS), GROUP_DIM, GROUP_DIM), const3),
            pl.BlockSpec((1, POOL_WIDTH), const2),
            pl.BlockSpec((1, SGU_WIDTH), const2),
            pl.BlockSpec((1, SGU_WIDTH), const2),
            pl.BlockSpec((SGU_HEADS, CHUNK, CHUNK), const3),
            pl.BlockSpec((CHUNK, SGU_HEADS), const2),
            pl.BlockSpec((D, D), const2),
            pl.BlockSpec((1, D), const2),
        ],
        out_specs=[tok, tok],
        out_shape=[jax.ShapeDtypeStruct((B, S, D), jnp.float32)] * 2,
        scratch_shapes=[
            pltpu.VMEM((MIX_BLOCK + CHUNK, POOL_WIDTH), jnp.float32),
            pltpu.VMEM((MIX_BLOCK, D), jnp.bfloat16),
        ],
        compiler_params=pltpu.CompilerParams(
            dimension_semantics=("arbitrary", "arbitrary"), vmem_limit_bytes=VMEM_LIMIT_BYTES),
        name="mixer",
    )(x, mod3, g1, w_in, band, w_pool, pool_scale, ln_g, ln_b, w_sp, b_sp_t, w_out, g2)


def _extract_top(s, ids, payload, n_out):
    big = jnp.int32(2 ** 30)
    vals, pays = [], []
    for _ in range(n_out):
        m = jnp.max(s, axis=0, keepdims=True)
        pick = jnp.min(jnp.where(s == m, ids, big), axis=0, keepdims=True)
        sel = ids == pick
        vals.append(m)
        pays.append(jnp.max(jnp.where(sel, payload, -1), axis=0, keepdims=True))
        s = jnp.where(sel, -jnp.inf, s)
    return jnp.concatenate(vals, axis=0), jnp.concatenate(pays, axis=0)


def _topk_kernel(h2_ref, wq_ref, keys_ref, idx_ref, gate_ref):
    T = TOPK_BLOCK
    q = _bf16_dot(h2_ref[...], wq_ref[...])
    key_ids = lax.broadcasted_iota(jnp.int32, (N_KEYS, T), 0)
    idx_rows, gate_rows = [], []
    for hd in range(PEER_HEADS):
        tops = []
        for p in range(2):
            hp = hd * 2 + p
            qs = q[:, hp * HALF_KEY:(hp + 1) * HALF_KEY].astype(jnp.bfloat16)
            st = lax.dot_general(keys_ref[hp], qs, (((1,), (1,)), ((), ())),
                                 preferred_element_type=jnp.float32)
            tops.append(_extract_top(st, key_ids, key_ids, TOPK))
        (s0, i0), (s1, i1) = tops
        cand = jnp.concatenate(
            [s0[0:1] + s1] + [s0[a:a + 1] + s1[0:8] for a in range(1, 8)] + [s0[8:16] + s1[0:1]], axis=0)
        cidx = jnp.concatenate(
            [i0[0:1] * N_KEYS + i1] + [i0[a:a + 1] * N_KEYS + i1[0:8] for a in range(1, 8)]
            + [i0[8:16] * N_KEYS + i1[0:1]], axis=0)
        row = lax.broadcasted_iota(jnp.int32, (N_CAND, T), 0)
        flat = jnp.where(row < 16, row,
                         jnp.where(row < 72, jnp.right_shift(row - 8, 3) * 16 + jnp.bitwise_and(row, 7),
                                   (row - 64) * 16))
        best, sel_idx = _extract_top(cand, flat, cidx, TOPK)
        e = jnp.exp(best - jnp.max(best, axis=0, keepdims=True))
        gate_rows.append(e / jnp.sum(e, axis=0, keepdims=True))
        idx_rows.append(sel_idx)
    idx_ref[...] = jnp.concatenate(idx_rows, axis=0).T
    gate_ref[...] = jnp.concatenate(gate_rows, axis=0).T


def _query_topk(h2, w_query, keys):
    N, D = h2.shape
    tok = pl.BlockSpec((TOPK_BLOCK, N_SEL), lambda i: (i, 0))
    return pl.pallas_call(
        _topk_kernel,
        grid=(N // TOPK_BLOCK,),
        in_specs=[
            pl.BlockSpec((TOPK_BLOCK, D), lambda i: (i, 0)),
            pl.BlockSpec((D, 2 * PEER_HEADS * HALF_KEY), lambda i: (0, 0)),
            pl.BlockSpec((2 * PEER_HEADS, N_KEYS, HALF_KEY), lambda i: (0, 0, 0)),
        ],
        out_specs=[tok, tok],
        out_shape=[jax.ShapeDtypeStruct((N, N_SEL), jnp.int32), jax.ShapeDtypeStruct((N, N_SEL), jnp.float32)],
        compiler_params=pltpu.CompilerParams(
            dimension_semantics=("arbitrary",), vmem_limit_bytes=VMEM_LIMIT_BYTES),
        name="query_topk",
    )(h2, w_query, keys)


def _sc_mesh_and_workers():
    info = plsc.get_sparse_core_info()
    assert info.num_lanes == SC_LANES
    mesh = plsc.VectorSubcoreMesh(core_axis_name="core", subcore_axis_name="subcore")
    return mesh, info.num_cores, info.num_cores * info.num_subcores


def _peer_down(h2, idx, table):
    N, D = h2.shape
    mesh, n_cores, n_workers = _sc_mesh_and_workers()
    per_worker = N // n_workers
    n_slices = D // SC_LANES

    def body(h_hbm, idx_hbm, tab_hbm, out_hbm, idx_v, h_v, rows_v, out_v, tr_v, sem):
        wid = lax.axis_index("subcore") * n_cores + lax.axis_index("core")
        lane = lax.iota(jnp.int32, SC_LANES)

        @pl.loop(0, per_worker // SC_TOKENS)
        def _(blk):
            base = wid * per_worker + blk * SC_TOKENS
            pltpu.sync_copy(idx_hbm.at[pl.ds(base, SC_TOKENS)], idx_v)
            pltpu.sync_copy(h_hbm.at[pl.ds(base, SC_TOKENS)], h_v)

            @pl.loop(0, SC_TOKENS)
            def _(t):
                for hd in range(PEER_HEADS):
                    pltpu.async_copy(tab_hbm.at[idx_v.at[t, pl.ds(hd * TOPK, TOPK)]], rows_v, sem).wait()

                    def dot_step(s, accs):
                        hv = h_v[t, pl.ds(s * SC_LANES, SC_LANES)]
                        return tuple(acc + rows_v[k, pl.ds(s * SC_LANES, SC_LANES)] * hv
                                     for k, acc in enumerate(accs))

                    accs = lax.fori_loop(0, n_slices, dot_step,
                                         tuple(jnp.zeros((SC_LANES,), jnp.float32) for _ in range(TOPK)))
                    for k in range(TOPK):
                        tr_v[k, :] = accs[k]
                    tot = jnp.zeros((SC_LANES,), jnp.float32)
                    for l in range(SC_LANES):
                        tot = tot + plsc.load_gather(tr_v, [lane, jnp.full((SC_LANES,), l, jnp.int32)])
                    out_v[t, pl.ds(hd * TOPK, TOPK)] = tot

            pltpu.sync_copy(out_v, out_hbm.at[pl.ds(base, SC_TOKENS)])

    return pl.kernel(
        body,
        out_type=jax.ShapeDtypeStruct((N, N_SEL), jnp.float32),
        mesh=mesh,
        scratch_types=[
            pltpu.VMEM((SC_TOKENS, N_SEL), jnp.int32),
            pltpu.VMEM((SC_TOKENS, D), jnp.float32),
            pltpu.VMEM((TOPK, D), jnp.float32),
            pltpu.VMEM((SC_TOKENS, N_SEL), jnp.float32),
            pltpu.VMEM((TOPK, SC_LANES), jnp.float32),
            pltpu.SemaphoreType.DMA,
        ],
        compiler_params=pltpu.CompilerParams(needs_layout_passes=False),
        name="peer_down",
    )(h2, idx, table)


def _peer_up(wgt, idx, table):
    N = wgt.shape[0]
    D = table.shape[1]
    mesh, n_cores, n_workers = _sc_mesh_and_workers()
    per_worker = N // n_workers
    n_slices = D // SC_LANES

    def body(w_hbm, idx_hbm, tab_hbm, out_hbm, idx_v, w_v, rows_v, out_v, sem):
        wid = lax.axis_index("subcore") * n_cores + lax.axis_index("core")

        @pl.loop(0, per_worker // SC_TOKENS)
        def _(blk):
            base = wid * per_worker + blk * SC_TOKENS
            pltpu.sync_copy(idx_hbm.at[pl.ds(base, SC_TOKENS)], idx_v)
            pltpu.sync_copy(w_hbm.at[pl.ds(base, SC_TOKENS)], w_v)

            @pl.loop(0, SC_TOKENS)
            def _(t):
                t_vec = jnp.full((SC_LANES,), t, jnp.int32)
                for hd in range(PEER_HEADS):
                    pltpu.async_copy(tab_hbm.at[idx_v.at[t, pl.ds(hd * TOPK, TOPK)]], rows_v, sem).wait()
                    ws = [plsc.load_gather(w_v, [t_vec, jnp.full((SC_LANES,), hd * TOPK + k, jnp.int32)])
                          for k in range(TOPK)]

                    @pl.loop(0, n_slices)
                    def _(s):
                        sl = pl.ds(s * SC_LANES, SC_LANES)
                        acc = jnp.zeros((SC_LANES,), jnp.float32) if hd == 0 else out_v[t, sl]
                        for k in range(TOPK):
                            acc = acc + ws[k] * rows_v[k, sl]
                        out_v[t, sl] = acc

            pltpu.sync_copy(out_v, out_hbm.at[pl.ds(base, SC_TOKENS)])

    return pl.kernel(
        body,
        out_type=jax.ShapeDtypeStruct((N, D), jnp.float32),
        mesh=mesh,
        scratch_types=[
            pltpu.VMEM((SC_TOKENS, N_SEL), jnp.int32),
            pltpu.VMEM((SC_TOKENS, N_SEL), jnp.float32),
            pltpu.VMEM((TOPK, D), jnp.float32),
            pltpu.VMEM((SC_TOKENS, D), jnp.float32),
            pltpu.SemaphoreType.DMA,
        ],
        compiler_params=pltpu.CompilerParams(needs_layout_passes=False),
        name="peer_up",
    )(wgt, idx, table)


def _gate_act_kernel(pre_ref, gate_ref, o_ref):
    p = pre_ref[...]
    act = 0.5 * p * (1.0 + lax.erf(p * (1.0 / math.sqrt(2.0))))
    o_ref[...] = gate_ref[...] * act


def _gate_act(pre, gate):
    N = pre.shape[0]
    spec = pl.BlockSpec((EW_BLOCK, N_SEL), lambda i: (i, 0))
    return pl.pallas_call(
        _gate_act_kernel, grid=(N // EW_BLOCK,), in_specs=[spec, spec], out_specs=spec,
        out_shape=jax.ShapeDtypeStruct((N, N_SEL), jnp.float32), name="gate_act",
    )(pre, gate)


def _final_kernel(x1_ref, peer_ref, mod_ref, g_ref, o_ref, *, normalize):
    gate2 = mod_ref[0, 5:6, :]
    y = x1_ref[0] + gate2 * peer_ref[0]
    o_ref[0] = _rms(y, g_ref[...]) if normalize else y


def _residual(x1, peer, mod3, g_final, normalize):
    B, S, D = x1.shape
    tok = pl.BlockSpec((1, EW_BLOCK, D), lambda b, j: (b, j, 0))
    return pl.pallas_call(
        functools.partial(_final_kernel, normalize=normalize),
        grid=(B, S // EW_BLOCK),
        in_specs=[tok, tok, pl.BlockSpec((1, N_MOD, D), lambda b, j: (b, 0, 0)),
                  pl.BlockSpec((1, D), lambda b, j: (0, 0))],
        out_specs=tok,
        out_shape=jax.ShapeDtypeStruct((B, S, D), jnp.float32),
        compiler_params=pltpu.CompilerParams(vmem_limit_bytes=VMEM_LIMIT_BYTES),
        name="final_norm",
    )(x1, peer, mod3, g_final)


def kernel(x, c, w_ada, b_ada, g_norm1, w_in, w_pool, pool_scale, sgu_ln_g, sgu_ln_b, w_spatial, b_spatial,
           w_out, g_norm2, w_query, sub_keys, expert_down, expert_up, g_final):
    B, S, D = x.shape
    depth = w_ada.shape[0]
    bf16 = jnp.bfloat16
    band = jnp.asarray(_pool_band(), bf16)
    c_pad = jnp.pad(c, ((0, 8 - B), (0, 0)))
    for l in range(depth):
        mod = _ada_mod(c_pad, w_ada[l].astype(bf16), b_ada[l][None, :])[:B]
        mod3 = mod.reshape(B, N_MOD, D)
        x1, h2 = _mixer(
            x, mod3, g_norm1[l][None, :], w_in[l].astype(bf16), band, w_pool[l].astype(bf16),
            pool_scale[l][None, :], sgu_ln_g[l][None, :], sgu_ln_b[l][None, :], w_spatial[l],
            b_spatial[l].T, w_out[l].astype(bf16), g_norm2[l][None, :])
        h2f = h2.reshape(B * S, D)
        keys = sub_keys[l].reshape(2 * PEER_HEADS, N_KEYS, HALF_KEY).astype(bf16)
        idx, gate = _query_topk(h2f, w_query[l].astype(bf16), keys)
        pre = _peer_down(h2f, idx, expert_down[l])
        wgt = _gate_act(pre, gate)
        peer = _peer_up(wgt, idx, expert_up[l]).reshape(B, S, D)
        x = _residual(x1, peer, mod3, g_final[None, :], normalize=(l + 1 == depth))
    return x
```

```python
import functools
import math

import jax
import jax.numpy as jnp
import numpy as np
from jax import lax
from jax.experimental import pallas as pl
from jax.experimental.pallas import tpu as pltpu
from jax.experimental.pallas import tpu_sc as plsc

D_MODEL = 1024
POOL_WIDTH = 512
SGU_WIDTH = 512
POOL_WINDOWS = (2, 4, 8, 16)
GROUP_DIM = 128
CHUNK = 128
SGU_HEADS = 4
IN_PROJ_WIDTH = POOL_WIDTH + 2 * SGU_WIDTH
PEER_HEADS = 8
N_KEYS = 128
HALF_KEY = 128
TOPK = 16
N_MOD = 6
EPS = 1e-6

MIX_BLOCK = 512
TOPK_BLOCK = 256
EW_BLOCK = 1024
VMEM_LIMIT_BYTES = 48 * 1024 * 1024

SC_LANES = 16
SC_TOKENS = 32
SC_RING = 4
N_SEL = PEER_HEADS * TOPK

N_CAND = 16 + 7 * 8 + 8


def _pool_band():
    t = np.arange(CHUNK)[:, None]
    j = np.arange(2 * CHUNK)[None, :]
    bands = []
    for w in POOL_WINDOWS:
        m = (j > CHUNK + t - w) & (j <= CHUNK + t)
        bands.append(np.where(m, 1.0 / w, 0.0))
    return np.stack(bands).astype(np.float32)


def _bf16_dot(a, b):
    return jnp.dot(a.astype(jnp.bfloat16), b.astype(jnp.bfloat16), preferred_element_type=jnp.float32)


def _ada_kernel(c_ref, w_ref, b_ref, o_ref):
    c = c_ref[...]
    c_act = c * jax.nn.sigmoid(c)
    o_ref[...] = _bf16_dot(c_act, w_ref[...]) + b_ref[...]


def _ada_mod(c_pad, w_ada, b_ada):
    rows = c_pad.shape[0]
    return pl.pallas_call(
        _ada_kernel,
        grid=(N_MOD,),
        in_specs=[
            pl.BlockSpec((rows, D_MODEL), lambda i: (0, 0)),
            pl.BlockSpec((D_MODEL, D_MODEL), lambda i: (0, i)),
            pl.BlockSpec((1, D_MODEL), lambda i: (0, i)),
        ],
        out_specs=pl.BlockSpec((rows, D_MODEL), lambda i: (0, i)),
        out_shape=jax.ShapeDtypeStruct((rows, N_MOD * D_MODEL), jnp.float32),
        name="ada_mod",
    )(c_pad, w_ada, b_ada)


def _rms(x, g):
    return x * lax.rsqrt(jnp.mean(x * x, axis=-1, keepdims=True) + EPS) * g


def _mix_kernel(x_ref, mod_ref, g1_ref, win_ref, band_ref, wpool_ref, pscale_ref, lng_ref, lnb_ref,
                wsp_ref, bsp_ref, wout_ref, g2_ref, x1_ref, h2_ref, aext_ref, mixed_ref):
    j = pl.program_id(1)
    x = x_ref[0]
    shift1, scale1, gate1 = mod_ref[0, 0:1, :], mod_ref[0, 1:2, :], mod_ref[0, 2:3, :]
    shift2, scale2 = mod_ref[0, 3:4, :], mod_ref[0, 4:5, :]

    h = _rms(x, g1_ref[...]) * (1.0 + scale1) + shift1
    proj = _bf16_dot(h, win_ref[...])
    a = proj[:, :POOL_WIDTH]
    u = proj[:, POOL_WIDTH:POOL_WIDTH + SGU_WIDTH]
    v = proj[:, POOL_WIDTH + SGU_WIDTH:]

    @pl.when(j == 0)
    def _():
        aext_ref[0:CHUNK, :] = jnp.zeros((CHUNK, POOL_WIDTH), jnp.float32)

    aext_ref[CHUNK:, :] = a

    mu = jnp.mean(v, axis=-1, keepdims=True)
    vc = v - mu
    vn = vc * lax.rsqrt(jnp.mean(vc * vc, axis=-1, keepdims=True) + EPS) * lng_ref[...] + lnb_ref[...]

    row = lax.broadcasted_iota(jnp.int32, (CHUNK, CHUNK), 0)
    col = lax.broadcasted_iota(jnp.int32, (CHUNK, CHUNK), 1)
    tril = col <= row
    t_col = lax.broadcasted_iota(jnp.int32, (CHUNK, 1), 0)

    for c in range(MIX_BLOCK // CHUNK):
        r0 = c * CHUNK
        pos = (j * MIX_BLOCK + r0 + 1 + t_col).astype(jnp.float32)
        for g, w in enumerate(POOL_WINDOWS):
            c0 = g * GROUP_DIM
            seg = aext_ref[r0:r0 + 2 * CHUNK, c0:c0 + GROUP_DIM]
            hi = seg.astype(jnp.bfloat16)
            lo = (seg - hi.astype(jnp.float32)).astype(jnp.bfloat16)
            band = band_ref[g]
            win = (jnp.dot(band, hi, preferred_element_type=jnp.float32)
                   + jnp.dot(band, lo, preferred_element_type=jnp.float32))
            mean = win * (float(w) / jnp.minimum(pos, float(w)))
            pooled = mean - seg[CHUNK:, :]
            y = _bf16_dot(pooled, wpool_ref[g]) * pscale_ref[:, c0:c0 + GROUP_DIM]
            mixed_ref[r0:r0 + CHUNK, c0:c0 + GROUP_DIM] = y.astype(jnp.bfloat16)
        for hh in range(SGU_HEADS):
            c0 = hh * GROUP_DIM
            wm = jnp.where(tril, wsp_ref[hh], 0.0)
            m = _bf16_dot(wm, vn[r0:r0 + CHUNK, c0:c0 + GROUP_DIM]) + bsp_ref[:, hh:hh + 1]
            s = u[r0:r0 + CHUNK, c0:c0 + GROUP_DIM] * m
            mixed_ref[r0:r0 + CHUNK, POOL_WIDTH + c0:POOL_WIDTH + c0 + GROUP_DIM] = s.astype(jnp.bfloat16)

    aext_ref[0:CHUNK, :] = aext_ref[MIX_BLOCK:MIX_BLOCK + CHUNK, :]

    x1 = x + gate1 * jnp.dot(mixed_ref[...], wout_ref[...], preferred_element_type=jnp.float32)
    x1_ref[0] = x1
    h2_ref[0] = _rms(x1, g2_ref[...]) * (1.0 + scale2) + shift2


def _mixer(x, mod3, g1, w_in, band, w_pool, pool_scale, ln_g, ln_b, w_sp, b_sp_t, w_out, g2):
    B, S, D = x.shape
    const2 = lambda b, j: (0, 0)
    const3 = lambda b, j: (0, 0, 0)
    tok = pl.BlockSpec((1, MIX_BLOCK, D), lambda b, j: (b, j, 0))
    return pl.pallas_call(
        _mix_kernel,
        grid=(B, S // MIX_BLOCK),
        in_specs=[
            tok,
            pl.BlockSpec((1, N_MOD, D), lambda b, j: (b, 0, 0)),
            pl.BlockSpec((1, D), const2),
            pl.BlockSpec((D, IN_PROJ_WIDTH), const2),
            pl.BlockSpec((len(POOL_WINDOWS), CHUNK, 2 * CHUNK), const3),
            pl.BlockSpec((len(POOL_WINDOWS), GROUP_DIM, GROUP_DIM), const3),
            pl.BlockSpec((1, POOL_WIDTH), const2),
            pl.BlockSpec((1, SGU_WIDTH), const2),
            pl.BlockSpec((1, SGU_WIDTH), const2),
            pl.BlockSpec((SGU_HEADS, CHUNK, CHUNK), const3),
            pl.BlockSpec((CHUNK, SGU_HEADS), const2),
            pl.BlockSpec((D, D), const2),
            pl.BlockSpec((1, D), const2),
        ],
        out_specs=[tok, tok],
        out_shape=[jax.ShapeDtypeStruct((B, S, D), jnp.float32)] * 2,
        scratch_shapes=[
            pltpu.VMEM((MIX_BLOCK + CHUNK, POOL_WIDTH), jnp.float32),
            pltpu.VMEM((MIX_BLOCK, D), jnp.bfloat16),
        ],
        compiler_params=pltpu.CompilerParams(
            dimension_semantics=("arbitrary", "arbitrary"), vmem_limit_bytes=VMEM_LIMIT_BYTES),
        name="mixer",
    )(x, mod3, g1, w_in, band, w_pool, pool_scale, ln_g, ln_b, w_sp, b_sp_t, w_out, g2)


def _extract_top(s, ids, payload, n_out):
    big = jnp.int32(2 ** 30)
    vals, pays = [], []
    for _ in range(n_out):
        m = jnp.max(s, axis=0, keepdims=True)
        pick = jnp.min(jnp.where(s == m, ids, big), axis=0, keepdims=True)
        sel = ids == pick
        vals.append(m)
        pays.append(jnp.max(jnp.where(sel, payload, -1), axis=0, keepdims=True))
        s = jnp.where(sel, -jnp.inf, s)
    return jnp.concatenate(vals, axis=0), jnp.concatenate(pays, axis=0)


def _topk_kernel(h2_ref, wq_ref, keys_ref, idx_ref, gate_ref):
    T = TOPK_BLOCK
    q = _bf16_dot(h2_ref[...], wq_ref[...])
    key_ids = lax.broadcasted_iota(jnp.int32, (N_KEYS, T), 0)
    idx_rows, gate_rows = [], []
    for hd in range(PEER_HEADS):
        tops = []
        for p in range(2):
            hp = hd * 2 + p
            qs = q[:, hp * HALF_KEY:(hp + 1) * HALF_KEY].astype(jnp.bfloat16)
            st = lax.dot_general(keys_ref[hp], qs, (((1,), (1,)), ((), ())),
                                 preferred_element_type=jnp.float32)
            tops.append(_extract_top(st, key_ids, key_ids, TOPK))
        (s0, i0), (s1, i1) = tops
        cand = jnp.concatenate(
            [s0[0:1] + s1] + [s0[a:a + 1] + s1[0:8] for a in range(1, 8)] + [s0[8:16] + s1[0:1]], axis=0)
        cidx = jnp.concatenate(
            [i0[0:1] * N_KEYS + i1] + [i0[a:a + 1] * N_KEYS + i1[0:8] for a in range(1, 8)]
            + [i0[8:16] * N_KEYS + i1[0:1]], axis=0)
        row = lax.broadcasted_iota(jnp.int32, (N_CAND, T), 0)
        flat = jnp.where(row < 16, row,
                         jnp.where(row < 72, jnp.right_shift(row - 8, 3) * 16 + jnp.bitwise_and(row, 7),
                                   (row - 64) * 16))
        best, sel_idx = _extract_top(cand, flat, cidx, TOPK)
        e = jnp.exp(best - jnp.max(best, axis=0, keepdims=True))
        gate_rows.append(e / jnp.sum(e, axis=0, keepdims=True))
        idx_rows.append(sel_idx)
    idx_ref[...] = jnp.concatenate(idx_rows, axis=0).T
    gate_ref[...] = jnp.concatenate(gate_rows, axis=0).T


def _query_topk(h2, w_query, keys):
    N, D = h2.shape
    tok = pl.BlockSpec((TOPK_BLOCK, N_SEL), lambda i: (i, 0))
    return pl.pallas_call(
        _topk_kernel,
        grid=(N // TOPK_BLOCK,),
        in_specs=[
            pl.BlockSpec((TOPK_BLOCK, D), lambda i: (i, 0)),
            pl.BlockSpec((D, 2 * PEER_HEADS * HALF_KEY), lambda i: (0, 0)),
            pl.BlockSpec((2 * PEER_HEADS, N_KEYS, HALF_KEY), lambda i: (0, 0, 0)),
        ],
        out_specs=[tok, tok],
        out_shape=[jax.ShapeDtypeStruct((N, N_SEL), jnp.int32), jax.ShapeDtypeStruct((N, N_SEL), jnp.float32)],
        compiler_params=pltpu.CompilerParams(
            dimension_semantics=("arbitrary",), vmem_limit_bytes=VMEM_LIMIT_BYTES),
        name="query_topk",
    )(h2, w_query, keys)


def _sc_mesh_and_workers():
    info = plsc.get_sparse_core_info()
    assert info.num_lanes == SC_LANES
    mesh = plsc.VectorSubcoreMesh(core_axis_name="core", subcore_axis_name="subcore")
    return mesh, info.num_cores, info.num_cores * info.num_subcores


def _gather_ring(tab_hbm, idx_v, rows_v, sems, consume):
    look = SC_RING - 1

    def copy(t, hd):
        slot = hd % SC_RING
        return pltpu.make_async_copy(
            tab_hbm.at[idx_v.at[t, pl.ds(hd * TOPK, TOPK)]], rows_v.at[slot], sems.at[slot])

    for hd in range(look):
        copy(0, hd).start()

    @pl.loop(0, SC_TOKENS)
    def _(t):
        for hd in range(PEER_HEADS):
            nxt = hd + look
            if nxt < PEER_HEADS:
                copy(t, nxt).start()
            else:
                @pl.when(t + 1 < SC_TOKENS)
                def _():
                    copy(t + 1, nxt - PEER_HEADS).start()
            copy(t, hd).wait()
            consume(t, hd, rows_v.at[hd % SC_RING])


def _peer_down(h2, idx, table):
    N, D = h2.shape
    mesh, n_cores, n_workers = _sc_mesh_and_workers()
    per_worker = N // n_workers
    n_slices = D // SC_LANES

    def body(h_hbm, idx_hbm, tab_hbm, out_hbm, idx_v, h_v, rows_v, out_v, tr_v, sems):
        wid = lax.axis_index("subcore") * n_cores + lax.axis_index("core")
        lane = lax.iota(jnp.int32, SC_LANES)

        def consume(t, hd, rows):
            def dot_step(s, accs):
                hv = h_v[t, pl.ds(s * SC_LANES, SC_LANES)]
                return tuple(acc + rows[k, pl.ds(s * SC_LANES, SC_LANES)] * hv for k, acc in enumerate(accs))

            accs = lax.fori_loop(0, n_slices, dot_step,
                                 tuple(jnp.zeros((SC_LANES,), jnp.float32) for _ in range(TOPK)))
            for k in range(TOPK):
                tr_v[k, :] = accs[k]
            tot = jnp.zeros((SC_LANES,), jnp.float32)
            for l in range(SC_LANES):
                tot = tot + plsc.load_gather(tr_v, [lane, jnp.full((SC_LANES,), l, jnp.int32)])
            out_v[t, pl.ds(hd * TOPK, TOPK)] = tot

        @pl.loop(0, per_worker // SC_TOKENS)
        def _(blk):
            base = wid * per_worker + blk * SC_TOKENS
            pltpu.sync_copy(idx_hbm.at[pl.ds(base, SC_TOKENS)], idx_v)
            pltpu.sync_copy(h_hbm.at[pl.ds(base, SC_TOKENS)], h_v)
            _gather_ring(tab_hbm, idx_v, rows_v, sems, consume)
            pltpu.sync_copy(out_v, out_hbm.at[pl.ds(base, SC_TOKENS)])

    return pl.kernel(
        body,
        out_type=jax.ShapeDtypeStruct((N, N_SEL), jnp.float32),
        mesh=mesh,
        scratch_types=[
            pltpu.VMEM((SC_TOKENS, N_SEL), jnp.int32),
            pltpu.VMEM((SC_TOKENS, D), jnp.float32),
            pltpu.VMEM((SC_RING, TOPK, D), jnp.float32),
            pltpu.VMEM((SC_TOKENS, N_SEL), jnp.float32),
            pltpu.VMEM((TOPK, SC_LANES), jnp.float32),
            pltpu.SemaphoreType.DMA((SC_RING,)),
        ],
        compiler_params=pltpu.CompilerParams(needs_layout_passes=False),
        name="peer_down",
    )(h2, idx, table)


def _peer_up(wgt, idx, table):
    N = wgt.shape[0]
    D = table.shape[1]
    mesh, n_cores, n_workers = _sc_mesh_and_workers()
    per_worker = N // n_workers
    n_slices = D // SC_LANES

    def body(w_hbm, idx_hbm, tab_hbm, out_hbm, idx_v, w_v, rows_v, out_v, sems):
        wid = lax.axis_index("subcore") * n_cores + lax.axis_index("core")

        def consume(t, hd, rows):
            t_vec = jnp.full((SC_LANES,), t, jnp.int32)
            ws = [plsc.load_gather(w_v, [t_vec, jnp.full((SC_LANES,), hd * TOPK + k, jnp.int32)])
                  for k in range(TOPK)]

            @pl.loop(0, n_slices)
            def _(s):
                sl = pl.ds(s * SC_LANES, SC_LANES)
                acc = jnp.zeros((SC_LANES,), jnp.float32) if hd == 0 else out_v[t, sl]
                for k in range(TOPK):
                    acc = acc + ws[k] * rows[k, sl]
                out_v[t, sl] = acc

        @pl.loop(0, per_worker // SC_TOKENS)
        def _(blk):
            base = wid * per_worker + blk * SC_TOKENS
            pltpu.sync_copy(idx_hbm.at[pl.ds(base, SC_TOKENS)], idx_v)
            pltpu.sync_copy(w_hbm.at[pl.ds(base, SC_TOKENS)], w_v)
            _gather_ring(tab_hbm, idx_v, rows_v, sems, consume)
            pltpu.sync_copy(out_v, out_hbm.at[pl.ds(base, SC_TOKENS)])

    return pl.kernel(
        body,
        out_type=jax.ShapeDtypeStruct((N, D), jnp.float32),
        mesh=mesh,
        scratch_types=[
            pltpu.VMEM((SC_TOKENS, N_SEL), jnp.int32),
            pltpu.VMEM((SC_TOKENS, N_SEL), jnp.float32),
            pltpu.VMEM((SC_RING, TOPK, D), jnp.float32),
            pltpu.VMEM((SC_TOKENS, D), jnp.float32),
            pltpu.SemaphoreType.DMA((SC_RING,)),
        ],
        compiler_params=pltpu.CompilerParams(needs_layout_passes=False),
        name="peer_up",
    )(wgt, idx, table)


def _gate_act_kernel(pre_ref, gate_ref, o_ref):
    p = pre_ref[...]
    act = 0.5 * p * (1.0 + lax.erf(p * (1.0 / math.sqrt(2.0))))
    o_ref[...] = gate_ref[...] * act


def _gate_act(pre, gate):
    N = pre.shape[0]
    spec = pl.BlockSpec((EW_BLOCK, N_SEL), lambda i: (i, 0))
    return pl.pallas_call(
        _gate_act_kernel, grid=(N // EW_BLOCK,), in_specs=[spec, spec], out_specs=spec,
        out_shape=jax.ShapeDtypeStruct((N, N_SEL), jnp.float32), name="gate_act",
    )(pre, gate)


def _final_kernel(x1_ref, peer_ref, mod_ref, g_ref, o_ref, *, normalize):
    gate2 = mod_ref[0, 5:6, :]
    y = x1_ref[0] + gate2 * peer_ref[0]
    o_ref[0] = _rms(y, g_ref[...]) if normalize else y


def _residual(x1, peer, mod3, g_final, normalize):
    B, S, D = x1.shape
    tok = pl.BlockSpec((1, EW_BLOCK, D), lambda b, j: (b, j, 0))
    return pl.pallas_call(
        functools.partial(_final_kernel, normalize=normalize),
        grid=(B, S // EW_BLOCK),
        in_specs=[tok, tok, pl.BlockSpec((1, N_MOD, D), lambda b, j: (b, 0, 0)),
                  pl.BlockSpec((1, D), lambda b, j: (0, 0))],
        out_specs=tok,
        out_shape=jax.ShapeDtypeStruct((B, S, D), jnp.float32),
        compiler_params=pltpu.CompilerParams(vmem_limit_bytes=VMEM_LIMIT_BYTES),
        name="final_norm",
    )(x1, peer, mod3, g_final)


def kernel(x, c, w_ada, b_ada, g_norm1, w_in, w_pool, pool_scale, sgu_ln_g, sgu_ln_b, w_spatial, b_spatial,
           w_out, g_norm2, w_query, sub_keys, expert_down, expert_up, g_final):
    B, S, D = x.shape
    depth = w_ada.shape[0]
    bf16 = jnp.bfloat16
    band = jnp.asarray(_pool_band(), bf16)
    c_pad = jnp.pad(c, ((0, 8 - B), (0, 0)))
    for l in range(depth):
        mod = _ada_mod(c_pad, w_ada[l].astype(bf16), b_ada[l][None, :])[:B]
        mod3 = mod.reshape(B, N_MOD, D)
        x1, h2 = _mixer(
            x, mod3, g_norm1[l][None, :], w_in[l].astype(bf16), band, w_pool[l].astype(bf16),
            pool_scale[l][None, :], sgu_ln_g[l][None, :], sgu_ln_b[l][None, :], w_spatial[l],
            b_spatial[l].T, w_out[l].astype(bf16), g_norm2[l][None, :])
        h2f = h2.reshape(B * S, D)
        keys = sub_keys[l].reshape(2 * PEER_HEADS, N_KEYS, HALF_KEY).astype(bf16)
        idx, gate = _query_topk(h2f, w_query[l].astype(bf16), keys)
        pre = _peer_down(h2f, idx, expert_down[l])
        wgt = _gate_act(pre, gate)
        peer = _peer_up(wgt, idx, expert_up[l]).reshape(B, S, D)
        x = _residual(x1, peer, mod3, g_final[None, :], normalize=(l + 1 == depth))
    return x
```

```python
import functools
import math

import jax
import jax.numpy as jnp
import numpy as np
from jax import lax
from jax.experimental import pallas as pl
from jax.experimental.pallas import tpu as pltpu
from jax.experimental.pallas import tpu_sc as plsc

D_MODEL = 1024
POOL_WIDTH = 512
SGU_WIDTH = 512
POOL_WINDOWS = (2, 4, 8, 16)
GROUP_DIM = 128
CHUNK = 128
SGU_HEADS = 4
IN_PROJ_WIDTH = POOL_WIDTH + 2 * SGU_WIDTH
PEER_HEADS = 8
N_KEYS = 128
HALF_KEY = 128
TOPK = 16
N_MOD = 6
EPS = 1e-6

MIX_BLOCK = 512
TOPK_BLOCK = 256
EW_BLOCK = 1024
VMEM_LIMIT_BYTES = 48 * 1024 * 1024

SC_LANES = 16
SC_TOKENS = 32
SC_RING = 4
N_SEL = PEER_HEADS * TOPK

N_CAND = 16 + 7 * 8 + 8


def _pool_band():
    t = np.arange(CHUNK)[:, None]
    j = np.arange(2 * CHUNK)[None, :]
    bands = []
    for w in POOL_WINDOWS:
        m = (j > CHUNK + t - w) & (j <= CHUNK + t)
        bands.append(np.where(m, 1.0 / w, 0.0))
    return np.stack(bands).astype(np.float32)


def _bf16_dot(a, b):
    return jnp.dot(a.astype(jnp.bfloat16), b.astype(jnp.bfloat16), preferred_element_type=jnp.float32)


def _ada_kernel(c_ref, w_ref, b_ref, o_ref):
    c = c_ref[...]
    c_act = c * jax.nn.sigmoid(c)
    o_ref[...] = _bf16_dot(c_act, w_ref[...]) + b_ref[...]


def _ada_mod(c_pad, w_ada, b_ada):
    rows = c_pad.shape[0]
    return pl.pallas_call(
        _ada_kernel,
        grid=(N_MOD,),
        in_specs=[
            pl.BlockSpec((rows, D_MODEL), lambda i: (0, 0)),
            pl.BlockSpec((D_MODEL, D_MODEL), lambda i: (0, i)),
            pl.BlockSpec((1, D_MODEL), lambda i: (0, i)),
        ],
        out_specs=pl.BlockSpec((rows, D_MODEL), lambda i: (0, i)),
        out_shape=jax.ShapeDtypeStruct((rows, N_MOD * D_MODEL), jnp.float32),
        name="ada_mod",
    )(c_pad, w_ada, b_ada)


def _rms(x, g):
    return x * lax.rsqrt(jnp.mean(x * x, axis=-1, keepdims=True) + EPS) * g


def _mix_kernel(x_ref, mod_ref, g1_ref, win_ref, band_ref, wpool_ref, pscale_ref, lng_ref, lnb_ref,
                wsp_ref, bsp_ref, wout_ref, g2_ref, x1_ref, h2_ref, aext_ref, mixed_ref):
    j = pl.program_id(1)
    x = x_ref[0]
    shift1, scale1, gate1 = mod_ref[0, 0:1, :], mod_ref[0, 1:2, :], mod_ref[0, 2:3, :]
    shift2, scale2 = mod_ref[0, 3:4, :], mod_ref[0, 4:5, :]

    h = _rms(x, g1_ref[...]) * (1.0 + scale1) + shift1
    proj = _bf16_dot(h, win_ref[...])
    a = proj[:, :POOL_WIDTH]
    u = proj[:, POOL_WIDTH:POOL_WIDTH + SGU_WIDTH]
    v = proj[:, POOL_WIDTH + SGU_WIDTH:]

    @pl.when(j == 0)
    def _():
        aext_ref[0:CHUNK, :] = jnp.zeros((CHUNK, POOL_WIDTH), jnp.float32)

    aext_ref[CHUNK:, :] = a

    mu = jnp.mean(v, axis=-1, keepdims=True)
    vc = v - mu
    vn = vc * lax.rsqrt(jnp.mean(vc * vc, axis=-1, keepdims=True) + EPS) * lng_ref[...] + lnb_ref[...]

    row = lax.broadcasted_iota(jnp.int32, (CHUNK, CHUNK), 0)
    col = lax.broadcasted_iota(jnp.int32, (CHUNK, CHUNK), 1)
    tril = col <= row
    t_col = lax.broadcasted_iota(jnp.int32, (CHUNK, 1), 0)

    for c in range(MIX_BLOCK // CHUNK):
        r0 = c * CHUNK
        pos = (j * MIX_BLOCK + r0 + 1 + t_col).astype(jnp.float32)
        for g, w in enumerate(POOL_WINDOWS):
            c0 = g * GROUP_DIM
            seg = aext_ref[r0:r0 + 2 * CHUNK, c0:c0 + GROUP_DIM]
            hi = seg.astype(jnp.bfloat16)
            lo = (seg - hi.astype(jnp.float32)).astype(jnp.bfloat16)
            band = band_ref[g]
            win = (jnp.dot(band, hi, preferred_element_type=jnp.float32)
                   + jnp.dot(band, lo, preferred_element_type=jnp.float32))
            mean = win * (float(w) / jnp.minimum(pos, float(w)))
            pooled = mean - seg[CHUNK:, :]
            y = _bf16_dot(pooled, wpool_ref[g]) * pscale_ref[:, c0:c0 + GROUP_DIM]
            mixed_ref[r0:r0 + CHUNK, c0:c0 + GROUP_DIM] = y.astype(jnp.bfloat16)
        for hh in range(SGU_HEADS):
            c0 = hh * GROUP_DIM
            wm = jnp.where(tril, wsp_ref[hh], 0.0)
            m = _bf16_dot(wm, vn[r0:r0 + CHUNK, c0:c0 + GROUP_DIM]) + bsp_ref[:, hh:hh + 1]
            s = u[r0:r0 + CHUNK, c0:c0 + GROUP_DIM] * m
            mixed_ref[r0:r0 + CHUNK, POOL_WIDTH + c0:POOL_WIDTH + c0 + GROUP_DIM] = s.astype(jnp.bfloat16)

    aext_ref[0:CHUNK, :] = aext_ref[MIX_BLOCK:MIX_BLOCK + CHUNK, :]

    x1 = x + gate1 * jnp.dot(mixed_ref[...], wout_ref[...], preferred_element_type=jnp.float32)
    x1_ref[0] = x1
    h2_ref[0] = _rms(x1, g2_ref[...]) * (1.0 + scale2) + shift2


def _mixer(x, mod3, g1, w_in, band, w_pool, pool_scale, ln_g, ln_b, w_sp, b_sp_t, w_out, g2):
    B, S, D = x.shape
    const2 = lambda b, j: (0, 0)
    const3 = lambda b, j: (0, 0, 0)
    tok = pl.BlockSpec((1, MIX_BLOCK, D), lambda b, j: (b, j, 0))
    return pl.pallas_call(
        _mix_kernel,
        grid=(B, S // MIX_BLOCK),
        in_specs=[
            tok,
            pl.BlockSpec((1, N_MOD, D), lambda b, j: (b, 0, 0)),
            pl.BlockSpec((1, D), const2),
            pl.BlockSpec((D, IN_PROJ_WIDTH), const2),
            pl.BlockSpec((len(POOL_WINDOWS), CHUNK, 2 * CHUNK), const3),
            pl.BlockSpec((len(POOL_WINDOWS), GROUP_DIM, GROUP_DIM), const3),
            pl.BlockSpec((1, POOL_WIDTH), const2),
            pl.BlockSpec((1, SGU_WIDTH), const2),
            pl.BlockSpec((1, SGU_WIDTH), const2),
            pl.BlockSpec((SGU_HEADS, CHUNK, CHUNK), const3),
            pl.BlockSpec((CHUNK, SGU_HEADS), const2),
            pl.BlockSpec((D, D), const2),
            pl.BlockSpec((1, D), const2),
        ],
        out_specs=[tok, tok],
        out_shape=[jax.ShapeDtypeStruct((B, S, D), jnp.float32)] * 2,
        scratch_shapes=[
            pltpu.VMEM((MIX_BLOCK + CHUNK, POOL_WIDTH), jnp.float32),
            pltpu.VMEM((MIX_BLOCK, D), jnp.bfloat16),
        ],
        compiler_params=pltpu.CompilerParams(
            dimension_semantics=("arbitrary", "arbitrary"), vmem_limit_bytes=VMEM_LIMIT_BYTES),
        name="mixer",
    )(x, mod3, g1, w_in, band, w_pool, pool_scale, ln_g, ln_b, w_sp, b_sp_t, w_out, g2)


def _extract_top(s, ids, payload, n_out):
    big = jnp.int32(2 ** 30)
    vals, pays = [], []
    for _ in range(n_out):
        m = jnp.max(s, axis=0, keepdims=True)
        pick = jnp.min(jnp.where(s == m, ids, big), axis=0, keepdims=True)
        sel = ids == pick
        vals.append(m)
        pays.append(jnp.max(jnp.where(sel, payload, -1), axis=0, keepdims=True))
        s = jnp.where(sel, -jnp.inf, s)
    return jnp.concatenate(vals, axis=0), jnp.concatenate(pays, axis=0)


def _topk_kernel(h2_ref, wq_ref, keys_ref, idx_ref, gate_ref):
    T = TOPK_BLOCK
    q = _bf16_dot(h2_ref[...], wq_ref[...])
    key_ids = lax.broadcasted_iota(jnp.int32, (N_KEYS, T), 0)
    idx_rows, gate_rows = [], []
    for hd in range(PEER_HEADS):
        tops = []
        for p in range(2):
            hp = hd * 2 + p
            qs = q[:, hp * HALF_KEY:(hp + 1) * HALF_KEY].astype(jnp.bfloat16)
            st = lax.dot_general(keys_ref[hp], qs, (((1,), (1,)), ((), ())),
                                 preferred_element_type=jnp.float32)
            tops.append(_extract_top(st, key_ids, key_ids, TOPK))
        (s0, i0), (s1, i1) = tops
        cand = jnp.concatenate(
            [s0[0:1] + s1] + [s0[a:a + 1] + s1[0:8] for a in range(1, 8)] + [s0[8:16] + s1[0:1]], axis=0)
        cidx = jnp.concatenate(
            [i0[0:1] * N_KEYS + i1] + [i0[a:a + 1] * N_KEYS + i1[0:8] for a in range(1, 8)]
            + [i0[8:16] * N_KEYS + i1[0:1]], axis=0)
        row = lax.broadcasted_iota(jnp.int32, (N_CAND, T), 0)
        flat = jnp.where(row < 16, row,
                         jnp.where(row < 72, jnp.right_shift(row - 8, 3) * 16 + jnp.bitwise_and(row, 7),
                                   (row - 64) * 16))
        best, sel_idx = _extract_top(cand, flat, cidx, TOPK)
        e = jnp.exp(best - jnp.max(best, axis=0, keepdims=True))
        gate_rows.append(e / jnp.sum(e, axis=0, keepdims=True))
        idx_rows.append(sel_idx)
    idx_ref[...] = jnp.concatenate(idx_rows, axis=0).T
    gate_ref[...] = jnp.concatenate(gate_rows, axis=0).T


def _query_topk(h2, w_query, keys):
    N, D = h2.shape
    tok = pl.BlockSpec((TOPK_BLOCK, N_SEL), lambda i: (i, 0))
    return pl.pallas_call(
        _topk_kernel,
        grid=(N // TOPK_BLOCK,),
        in_specs=[
            pl.BlockSpec((TOPK_BLOCK, D), lambda i: (i, 0)),
            pl.BlockSpec((D, 2 * PEER_HEADS * HALF_KEY), lambda i: (0, 0)),
            pl.BlockSpec((2 * PEER_HEADS, N_KEYS, HALF_KEY), lambda i: (0, 0, 0)),
        ],
        out_specs=[tok, tok],
        out_shape=[jax.ShapeDtypeStruct((N, N_SEL), jnp.int32), jax.ShapeDtypeStruct((N, N_SEL), jnp.float32)],
        compiler_params=pltpu.CompilerParams(
            dimension_semantics=("arbitrary",), vmem_limit_bytes=VMEM_LIMIT_BYTES),
        name="query_topk",
    )(h2, w_query, keys)


def _sc_mesh_and_workers():
    info = plsc.get_sparse_core_info()
    assert info.num_lanes == SC_LANES
    mesh = plsc.VectorSubcoreMesh(core_axis_name="core", subcore_axis_name="subcore")
    return mesh, info.num_cores, info.num_cores * info.num_subcores


def _gather_ring(tab_hbm, idx_v, rows_v, sems, consume):
    look = SC_RING - 1

    def copy(t, hd):
        slot = hd % SC_RING
        return pltpu.make_async_copy(
            tab_hbm.at[idx_v.at[t, pl.ds(hd * TOPK, TOPK)]], rows_v.at[slot], sems.at[slot])

    for hd in range(look):
        copy(0, hd).start()

    @pl.loop(0, SC_TOKENS)
    def _(t):
        for hd in range(PEER_HEADS):
            nxt = hd + look
            if nxt < PEER_HEADS:
                copy(t, nxt).start()
            else:
                @pl.when(t + 1 < SC_TOKENS)
                def _():
                    copy(t + 1, nxt - PEER_HEADS).start()
            copy(t, hd).wait()
            consume(t, hd, rows_v.at[hd % SC_RING])


def _peer_down(h2, idx, table):
    N, D = h2.shape
    mesh, n_cores, n_workers = _sc_mesh_and_workers()
    per_worker = N // n_workers
    n_slices = D // SC_LANES

    def body(h_hbm, idx_hbm, tab_hbm, out_hbm, idx_v, h_v, rows_v, out_v, tr_v, sems):
        wid = lax.axis_index("subcore") * n_cores + lax.axis_index("core")
        lane = lax.iota(jnp.int32, SC_LANES)

        def consume(t, hd, rows):
            def dot_step(s, accs):
                hv = h_v[t, pl.ds(s * SC_LANES, SC_LANES)]
                return tuple(acc + rows[k, pl.ds(s * SC_LANES, SC_LANES)] * hv for k, acc in enumerate(accs))

            accs = lax.fori_loop(0, n_slices, dot_step,
                                 tuple(jnp.zeros((SC_LANES,), jnp.float32) for _ in range(TOPK)))
            for k in range(TOPK):
                tr_v[k, :] = accs[k]
            tot = jnp.zeros((SC_LANES,), jnp.float32)
            for l in range(SC_LANES):
                tot = tot + plsc.load_gather(tr_v, [lane, jnp.full((SC_LANES,), l, jnp.int32)])
            out_v[t, pl.ds(hd * TOPK, TOPK)] = tot

        @pl.loop(0, per_worker // SC_TOKENS)
        def _(blk):
            base = wid * per_worker + blk * SC_TOKENS
            pltpu.sync_copy(idx_hbm.at[pl.ds(base, SC_TOKENS)], idx_v)
            pltpu.sync_copy(h_hbm.at[pl.ds(base, SC_TOKENS)], h_v)
            _gather_ring(tab_hbm, idx_v, rows_v, sems, consume)
            pltpu.sync_copy(out_v, out_hbm.at[pl.ds(base, SC_TOKENS)])

    return pl.kernel(
        body,
        out_type=jax.ShapeDtypeStruct((N, N_SEL), jnp.float32),
        mesh=mesh,
        scratch_types=[
            pltpu.VMEM((SC_TOKENS, N_SEL), jnp.int32),
            pltpu.VMEM((SC_TOKENS, D), jnp.float32),
            pltpu.VMEM((SC_RING, TOPK, D), jnp.float32),
            pltpu.VMEM((SC_TOKENS, N_SEL), jnp.float32),
            pltpu.VMEM((TOPK, SC_LANES), jnp.float32),
            pltpu.SemaphoreType.DMA((SC_RING,)),
        ],
        compiler_params=pltpu.CompilerParams(needs_layout_passes=False),
        name="peer_down",
    )(h2, idx, table)


def _peer_up(wgt, idx, table):
    N = wgt.shape[0]
    D = table.shape[1]
    mesh, n_cores, n_workers = _sc_mesh_and_workers()
    per_worker = N // n_workers
    n_slices = D // SC_LANES

    def body(w_hbm, idx_hbm, tab_hbm, out_hbm, idx_v, w_v, rows_v, out_v, sems):
        wid = lax.axis_index("subcore") * n_cores + lax.axis_index("core")

        def consume(t, hd, rows):
            t_vec = jnp.full((SC_LANES,), t, jnp.int32)
            ws = [plsc.load_gather(w_v, [t_vec, jnp.full((SC_LANES,), hd * TOPK + k, jnp.int32)])
                  for k in range(TOPK)]

            @plsc.parallel_loop(0, n_slices, unroll=2)
            def _(s):
                sl = pl.ds(s * SC_LANES, SC_LANES)
                terms = [ws[k] * rows[k, sl] for k in range(TOPK)]
                if hd > 0:
                    terms.append(out_v[t, sl])
                while len(terms) > 1:
                    terms = [terms[i] + terms[i + 1] for i in range(0, len(terms) - 1, 2)] + (
                        [terms[-1]] if len(terms) % 2 else [])
                out_v[t, sl] = terms[0]

        @pl.loop(0, per_worker // SC_TOKENS)
        def _(blk):
            base = wid * per_worker + blk * SC_TOKENS
            pltpu.sync_copy(idx_hbm.at[pl.ds(base, SC_TOKENS)], idx_v)
            pltpu.sync_copy(w_hbm.at[pl.ds(base, SC_TOKENS)], w_v)
            _gather_ring(tab_hbm, idx_v, rows_v, sems, consume)
            pltpu.sync_copy(out_v, out_hbm.at[pl.ds(base, SC_TOKENS)])

    return pl.kernel(
        body,
        out_type=jax.ShapeDtypeStruct((N, D), jnp.float32),
        mesh=mesh,
        scratch_types=[
            pltpu.VMEM((SC_TOKENS, N_SEL), jnp.int32),
            pltpu.VMEM((SC_TOKENS, N_SEL), jnp.float32),
            pltpu.VMEM((SC_RING, TOPK, D), jnp.float32),
            pltpu.VMEM((SC_TOKENS, D), jnp.float32),
            pltpu.SemaphoreType.DMA((SC_RING,)),
        ],
        compiler_params=pltpu.CompilerParams(needs_layout_passes=False),
        name="peer_up",
    )(wgt, idx, table)


def _gate_act_kernel(pre_ref, gate_ref, o_ref):
    p = pre_ref[...]
    act = 0.5 * p * (1.0 + lax.erf(p * (1.0 / math.sqrt(2.0))))
    o_ref[...] = gate_ref[...] * act


def _gate_act(pre, gate):
    N = pre.shape[0]
    spec = pl.BlockSpec((EW_BLOCK, N_SEL), lambda i: (i, 0))
    return pl.pallas_call(
        _gate_act_kernel, grid=(N // EW_BLOCK,), in_specs=[spec, spec], out_specs=spec,
        out_shape=jax.ShapeDtypeStruct((N, N_SEL), jnp.float32), name="gate_act",
    )(pre, gate)


def _final_kernel(x1_ref, peer_ref, mod_ref, g_ref, o_ref, *, normalize):
    gate2 = mod_ref[0, 5:6, :]
    y = x1_ref[0] + gate2 * peer_ref[0]
    o_ref[0] = _rms(y, g_ref[...]) if normalize else y


def _residual(x1, peer, mod3, g_final, normalize):
    B, S, D = x1.shape
    tok = pl.BlockSpec((1, EW_BLOCK, D), lambda b, j: (b, j, 0))
    return pl.pallas_call(
        functools.partial(_final_kernel, normalize=normalize),
        grid=(B, S // EW_BLOCK),
        in_specs=[tok, tok, pl.BlockSpec((1, N_MOD, D), lambda b, j: (b, 0, 0)),
                  pl.BlockSpec((1, D), lambda b, j: (0, 0))],
        out_specs=tok,
        out_shape=jax.ShapeDtypeStruct((B, S, D), jnp.float32),
        compiler_params=pltpu.CompilerParams(vmem_limit_bytes=VMEM_LIMIT_BYTES),
        name="final_norm",
    )(x1, peer, mod3, g_final)


def kernel(x, c, w_ada, b_ada, g_norm1, w_in, w_pool, pool_scale, sgu_ln_g, sgu_ln_b, w_spatial, b_spatial,
           w_out, g_norm2, w_query, sub_keys, expert_down, expert_up, g_final):
    B, S, D = x.shape
    depth = w_ada.shape[0]
    bf16 = jnp.bfloat16
    band = jnp.asarray(_pool_band(), bf16)
    c_pad = jnp.pad(c, ((0, 8 - B), (0, 0)))
    for l in range(depth):
        mod = _ada_mod(c_pad, w_ada[l].astype(bf16), b_ada[l][None, :])[:B]
        mod3 = mod.reshape(B, N_MOD, D)
        x1, h2 = _mixer(
            x, mod3, g_norm1[l][None, :], w_in[l].astype(bf16), band, w_pool[l].astype(bf16),
            pool_scale[l][None, :], sgu_ln_g[l][None, :], sgu_ln_b[l][None, :], w_spatial[l],
            b_spatial[l].T, w_out[l].astype(bf16), g_norm2[l][None, :])
        h2f = h2.reshape(B * S, D)
        keys = sub_keys[l].reshape(2 * PEER_HEADS, N_KEYS, HALF_KEY).astype(bf16)
        idx, gate = _query_topk(h2f, w_query[l].astype(bf16), keys)
        pre = _peer_down(h2f, idx, expert_down[l])
        wgt = _gate_act(pre, gate)
        peer = _peer_up(wgt, idx, expert_up[l]).reshape(B, S, D)
        x = _residual(x1, peer, mod3, g_final[None, :], normalize=(l + 1 == depth))
    return x
```

```python
import functools
import math

import jax
import jax.numpy as jnp
import numpy as np
from jax import lax
from jax.experimental import pallas as pl
from jax.experimental.pallas import tpu as pltpu
from jax.experimental.pallas import tpu_sc as plsc

D_MODEL = 1024
POOL_WIDTH = 512
SGU_WIDTH = 512
POOL_WINDOWS = (2, 4, 8, 16)
GROUP_DIM = 128
CHUNK = 128
SGU_HEADS = 4
IN_PROJ_WIDTH = POOL_WIDTH + 2 * SGU_WIDTH
PEER_HEADS = 8
N_KEYS = 128
HALF_KEY = 128
TOPK = 16
N_MOD = 6
EPS = 1e-6

MIX_BLOCK = 512
TOPK_BLOCK = 256
EW_BLOCK = 1024
VMEM_LIMIT_BYTES = 48 * 1024 * 1024

SC_LANES = 16
SC_TOKENS = 32
SC_RING = 4
N_SEL = PEER_HEADS * TOPK

N_CAND = 16 + 7 * 8 + 8


def _pool_band():
    t = np.arange(CHUNK)[:, None]
    j = np.arange(2 * CHUNK)[None, :]
    bands = []
    for w in POOL_WINDOWS:
        m = (j > CHUNK + t - w) & (j <= CHUNK + t)
        bands.append(np.where(m, 1.0 / w, 0.0))
    return np.stack(bands).astype(np.float32)


def _bf16_dot(a, b):
    return jnp.dot(a.astype(jnp.bfloat16), b.astype(jnp.bfloat16), preferred_element_type=jnp.float32)


def _ada_kernel(c_ref, w_ref, b_ref, o_ref):
    c = c_ref[...]
    c_act = c * jax.nn.sigmoid(c)
    o_ref[...] = _bf16_dot(c_act, w_ref[...]) + b_ref[...]


def _ada_mod(c_pad, w_ada, b_ada):
    rows = c_pad.shape[0]
    return pl.pallas_call(
        _ada_kernel,
        grid=(N_MOD,),
        in_specs=[
            pl.BlockSpec((rows, D_MODEL), lambda i: (0, 0)),
            pl.BlockSpec((D_MODEL, D_MODEL), lambda i: (0, i)),
            pl.BlockSpec((1, D_MODEL), lambda i: (0, i)),
        ],
        out_specs=pl.BlockSpec((rows, D_MODEL), lambda i: (0, i)),
        out_shape=jax.ShapeDtypeStruct((rows, N_MOD * D_MODEL), jnp.float32),
        name="ada_mod",
    )(c_pad, w_ada, b_ada)


def _rms(x, g):
    return x * lax.rsqrt(jnp.mean(x * x, axis=-1, keepdims=True) + EPS) * g


def _mix_kernel(x_ref, mod_ref, g1_ref, win_ref, band_ref, wpool_ref, pscale_ref, lng_ref, lnb_ref,
                wsp_ref, bsp_ref, wout_ref, g2_ref, x1_ref, h2_ref, aext_ref, mixed_ref):
    j = pl.program_id(1)
    x = x_ref[0]
    shift1, scale1, gate1 = mod_ref[0, 0:1, :], mod_ref[0, 1:2, :], mod_ref[0, 2:3, :]
    shift2, scale2 = mod_ref[0, 3:4, :], mod_ref[0, 4:5, :]

    h = _rms(x, g1_ref[...]) * (1.0 + scale1) + shift1
    proj = _bf16_dot(h, win_ref[...])
    a = proj[:, :POOL_WIDTH]
    u = proj[:, POOL_WIDTH:POOL_WIDTH + SGU_WIDTH]
    v = proj[:, POOL_WIDTH + SGU_WIDTH:]

    @pl.when(j == 0)
    def _():
        aext_ref[0:CHUNK, :] = jnp.zeros((CHUNK, POOL_WIDTH), jnp.float32)

    aext_ref[CHUNK:, :] = a

    mu = jnp.mean(v, axis=-1, keepdims=True)
    vc = v - mu
    vn = vc * lax.rsqrt(jnp.mean(vc * vc, axis=-1, keepdims=True) + EPS) * lng_ref[...] + lnb_ref[...]

    row = lax.broadcasted_iota(jnp.int32, (CHUNK, CHUNK), 0)
    col = lax.broadcasted_iota(jnp.int32, (CHUNK, CHUNK), 1)
    tril = col <= row
    t_col = lax.broadcasted_iota(jnp.int32, (CHUNK, 1), 0)

    for c in range(MIX_BLOCK // CHUNK):
        r0 = c * CHUNK
        pos = (j * MIX_BLOCK + r0 + 1 + t_col).astype(jnp.float32)
        for g, w in enumerate(POOL_WINDOWS):
            c0 = g * GROUP_DIM
            seg = aext_ref[r0:r0 + 2 * CHUNK, c0:c0 + GROUP_DIM]
            hi = seg.astype(jnp.bfloat16)
            lo = (seg - hi.astype(jnp.float32)).astype(jnp.bfloat16)
            band = band_ref[g]
            win = (jnp.dot(band, hi, preferred_element_type=jnp.float32)
                   + jnp.dot(band, lo, preferred_element_type=jnp.float32))
            mean = win * (float(w) / jnp.minimum(pos, float(w)))
            pooled = mean - seg[CHUNK:, :]
            y = _bf16_dot(pooled, wpool_ref[g]) * pscale_ref[:, c0:c0 + GROUP_DIM]
            mixed_ref[r0:r0 + CHUNK, c0:c0 + GROUP_DIM] = y.astype(jnp.bfloat16)
        for hh in range(SGU_HEADS):
            c0 = hh * GROUP_DIM
            wm = jnp.where(tril, wsp_ref[hh], 0.0)
            m = _bf16_dot(wm, vn[r0:r0 + CHUNK, c0:c0 + GROUP_DIM]) + bsp_ref[:, hh:hh + 1]
            s = u[r0:r0 + CHUNK, c0:c0 + GROUP_DIM] * m
            mixed_ref[r0:r0 + CHUNK, POOL_WIDTH + c0:POOL_WIDTH + c0 + GROUP_DIM] = s.astype(jnp.bfloat16)

    aext_ref[0:CHUNK, :] = aext_ref[MIX_BLOCK:MIX_BLOCK + CHUNK, :]

    x1 = x + gate1 * jnp.dot(mixed_ref[...], wout_ref[...], preferred_element_type=jnp.float32)
    x1_ref[0] = x1
    h2_ref[0] = _rms(x1, g2_ref[...]) * (1.0 + scale2) + shift2


def _mixer(x, b0, n_seq, mod3, g1, w_in, band, w_pool, pool_scale, ln_g, ln_b, w_sp, b_sp_t, w_out, g2):
    _, S, D = x.shape
    const2 = lambda b, j: (0, 0)
    const3 = lambda b, j: (0, 0, 0)
    tok = pl.BlockSpec((1, MIX_BLOCK, D), lambda b, j: (b, j, 0))
    return pl.pallas_call(
        _mix_kernel,
        grid=(n_seq, S // MIX_BLOCK),
        in_specs=[
            pl.BlockSpec((1, MIX_BLOCK, D), lambda b, j: (b0 + b, j, 0)),
            pl.BlockSpec((1, N_MOD, D), lambda b, j: (b0 + b, 0, 0)),
            pl.BlockSpec((1, D), const2),
            pl.BlockSpec((D, IN_PROJ_WIDTH), const2),
            pl.BlockSpec((len(POOL_WINDOWS), CHUNK, 2 * CHUNK), const3),
            pl.BlockSpec((len(POOL_WINDOWS), GROUP_DIM, GROUP_DIM), const3),
            pl.BlockSpec((1, POOL_WIDTH), const2),
            pl.BlockSpec((1, SGU_WIDTH), const2),
            pl.BlockSpec((1, SGU_WIDTH), const2),
            pl.BlockSpec((SGU_HEADS, CHUNK, CHUNK), const3),
            pl.BlockSpec((CHUNK, SGU_HEADS), const2),
            pl.BlockSpec((D, D), const2),
            pl.BlockSpec((1, D), const2),
        ],
        out_specs=[tok, tok],
        out_shape=[jax.ShapeDtypeStruct((n_seq, S, D), jnp.float32)] * 2,
        scratch_shapes=[
            pltpu.VMEM((MIX_BLOCK + CHUNK, POOL_WIDTH), jnp.float32),
            pltpu.VMEM((MIX_BLOCK, D), jnp.bfloat16),
        ],
        compiler_params=pltpu.CompilerParams(
            dimension_semantics=("arbitrary", "arbitrary"), vmem_limit_bytes=VMEM_LIMIT_BYTES),
        name="mixer",
    )(x, mod3, g1, w_in, band, w_pool, pool_scale, ln_g, ln_b, w_sp, b_sp_t, w_out, g2)


def _extract_top(s, ids, payload, n_out):
    big = jnp.int32(2 ** 30)
    vals, pays = [], []
    for _ in range(n_out):
        m = jnp.max(s, axis=0, keepdims=True)
        pick = jnp.min(jnp.where(s == m, ids, big), axis=0, keepdims=True)
        sel = ids == pick
        vals.append(m)
        pays.append(jnp.max(jnp.where(sel, payload, -1), axis=0, keepdims=True))
        s = jnp.where(sel, -jnp.inf, s)
    return jnp.concatenate(vals, axis=0), jnp.concatenate(pays, axis=0)


def _topk_kernel(h2_ref, wq_ref, keys_ref, idx_ref, gate_ref):
    T = TOPK_BLOCK
    q = _bf16_dot(h2_ref[...], wq_ref[...])
    key_ids = lax.broadcasted_iota(jnp.int32, (N_KEYS, T), 0)
    idx_rows, gate_rows = [], []
    for hd in range(PEER_HEADS):
        tops = []
        for p in range(2):
            hp = hd * 2 + p
            qs = q[:, hp * HALF_KEY:(hp + 1) * HALF_KEY].astype(jnp.bfloat16)
            st = lax.dot_general(keys_ref[hp], qs, (((1,), (1,)), ((), ())),
                                 preferred_element_type=jnp.float32)
            tops.append(_extract_top(st, key_ids, key_ids, TOPK))
        (s0, i0), (s1, i1) = tops
        cand = jnp.concatenate(
            [s0[0:1] + s1] + [s0[a:a + 1] + s1[0:8] for a in range(1, 8)] + [s0[8:16] + s1[0:1]], axis=0)
        cidx = jnp.concatenate(
            [i0[0:1] * N_KEYS + i1] + [i0[a:a + 1] * N_KEYS + i1[0:8] for a in range(1, 8)]
            + [i0[8:16] * N_KEYS + i1[0:1]], axis=0)
        row = lax.broadcasted_iota(jnp.int32, (N_CAND, T), 0)
        flat = jnp.where(row < 16, row,
                         jnp.where(row < 72, jnp.right_shift(row - 8, 3) * 16 + jnp.bitwise_and(row, 7),
                                   (row - 64) * 16))
        best, sel_idx = _extract_top(cand, flat, cidx, TOPK)
        e = jnp.exp(best - jnp.max(best, axis=0, keepdims=True))
        gate_rows.append(e / jnp.sum(e, axis=0, keepdims=True))
        idx_rows.append(sel_idx)
    idx_ref[...] = jnp.concatenate(idx_rows, axis=0).T
    gate_ref[...] = jnp.concatenate(gate_rows, axis=0).T


def _query_topk(h2, w_query, keys):
    N, D = h2.shape
    tok = pl.BlockSpec((TOPK_BLOCK, N_SEL), lambda i: (i, 0))
    return pl.pallas_call(
        _topk_kernel,
        grid=(N // TOPK_BLOCK,),
        in_specs=[
            pl.BlockSpec((TOPK_BLOCK, D), lambda i: (i, 0)),
            pl.BlockSpec((D, 2 * PEER_HEADS * HALF_KEY), lambda i: (0, 0)),
            pl.BlockSpec((2 * PEER_HEADS, N_KEYS, HALF_KEY), lambda i: (0, 0, 0)),
        ],
        out_specs=[tok, tok],
        out_shape=[jax.ShapeDtypeStruct((N, N_SEL), jnp.int32), jax.ShapeDtypeStruct((N, N_SEL), jnp.float32)],
        compiler_params=pltpu.CompilerParams(
            dimension_semantics=("arbitrary",), vmem_limit_bytes=VMEM_LIMIT_BYTES),
        name="query_topk",
    )(h2, w_query, keys)


def _sc_mesh_and_workers():
    info = plsc.get_sparse_core_info()
    assert info.num_lanes == SC_LANES
    mesh = plsc.VectorSubcoreMesh(core_axis_name="core", subcore_axis_name="subcore")
    return mesh, info.num_cores, info.num_cores * info.num_subcores


def _gather_ring(tab_hbm, idx_v, rows_v, sems, consume):
    look = SC_RING - 1

    def copy(t, hd):
        slot = hd % SC_RING
        return pltpu.make_async_copy(
            tab_hbm.at[idx_v.at[t, pl.ds(hd * TOPK, TOPK)]], rows_v.at[slot], sems.at[slot])

    for hd in range(look):
        copy(0, hd).start()

    @pl.loop(0, SC_TOKENS)
    def _(t):
        for hd in range(PEER_HEADS):
            nxt = hd + look
            if nxt < PEER_HEADS:
                copy(t, nxt).start()
            else:
                @pl.when(t + 1 < SC_TOKENS)
                def _():
                    copy(t + 1, nxt - PEER_HEADS).start()
            copy(t, hd).wait()
            consume(t, hd, rows_v.at[hd % SC_RING])


def _peer_down(h2, idx, table):
    N, D = h2.shape
    mesh, n_cores, n_workers = _sc_mesh_and_workers()
    per_worker = N // n_workers
    n_slices = D // SC_LANES

    def body(h_hbm, idx_hbm, tab_hbm, out_hbm, idx_v, h_v, rows_v, out_v, tr_v, sems):
        wid = lax.axis_index("subcore") * n_cores + lax.axis_index("core")
        lane = lax.iota(jnp.int32, SC_LANES)

        def consume(t, hd, rows):
            def dot_step(s, accs):
                hv = h_v[t, pl.ds(s * SC_LANES, SC_LANES)]
                return tuple(acc + rows[k, pl.ds(s * SC_LANES, SC_LANES)] * hv for k, acc in enumerate(accs))

            accs = lax.fori_loop(0, n_slices, dot_step,
                                 tuple(jnp.zeros((SC_LANES,), jnp.float32) for _ in range(TOPK)))
            for k in range(TOPK):
                tr_v[k, :] = accs[k]
            tot = jnp.zeros((SC_LANES,), jnp.float32)
            for l in range(SC_LANES):
                tot = tot + plsc.load_gather(tr_v, [lane, jnp.full((SC_LANES,), l, jnp.int32)])
            out_v[t, pl.ds(hd * TOPK, TOPK)] = tot

        @pl.loop(0, per_worker // SC_TOKENS)
        def _(blk):
            base = wid * per_worker + blk * SC_TOKENS
            pltpu.sync_copy(idx_hbm.at[pl.ds(base, SC_TOKENS)], idx_v)
            pltpu.sync_copy(h_hbm.at[pl.ds(base, SC_TOKENS)], h_v)
            _gather_ring(tab_hbm, idx_v, rows_v, sems, consume)
            pltpu.sync_copy(out_v, out_hbm.at[pl.ds(base, SC_TOKENS)])

    return pl.kernel(
        body,
        out_type=jax.ShapeDtypeStruct((N, N_SEL), jnp.float32),
        mesh=mesh,
        scratch_types=[
            pltpu.VMEM((SC_TOKENS, N_SEL), jnp.int32),
            pltpu.VMEM((SC_TOKENS, D), jnp.float32),
            pltpu.VMEM((SC_RING, TOPK, D), jnp.float32),
            pltpu.VMEM((SC_TOKENS, N_SEL), jnp.float32),
            pltpu.VMEM((TOPK, SC_LANES), jnp.float32),
            pltpu.SemaphoreType.DMA((SC_RING,)),
        ],
        compiler_params=pltpu.CompilerParams(needs_layout_passes=False),
        name="peer_down",
    )(h2, idx, table)


def _peer_up(wgt, idx, table):
    N = wgt.shape[0]
    D = table.shape[1]
    mesh, n_cores, n_workers = _sc_mesh_and_workers()
    per_worker = N // n_workers
    n_slices = D // SC_LANES

    def body(w_hbm, idx_hbm, tab_hbm, out_hbm, idx_v, w_v, rows_v, out_v, sems):
        wid = lax.axis_index("subcore") * n_cores + lax.axis_index("core")

        def consume(t, hd, rows):
            t_vec = jnp.full((SC_LANES,), t, jnp.int32)
            ws = [plsc.load_gather(w_v, [t_vec, jnp.full((SC_LANES,), hd * TOPK + k, jnp.int32)])
                  for k in range(TOPK)]

            @plsc.parallel_loop(0, n_slices, unroll=2)
            def _(s):
                sl = pl.ds(s * SC_LANES, SC_LANES)
                terms = [ws[k] * rows[k, sl] for k in range(TOPK)]
                if hd > 0:
                    terms.append(out_v[t, sl])
                while len(terms) > 1:
                    terms = [terms[i] + terms[i + 1] for i in range(0, len(terms) - 1, 2)] + (
                        [terms[-1]] if len(terms) % 2 else [])
                out_v[t, sl] = terms[0]

        @pl.loop(0, per_worker // SC_TOKENS)
        def _(blk):
            base = wid * per_worker + blk * SC_TOKENS
            pltpu.sync_copy(idx_hbm.at[pl.ds(base, SC_TOKENS)], idx_v)
            pltpu.sync_copy(w_hbm.at[pl.ds(base, SC_TOKENS)], w_v)
            _gather_ring(tab_hbm, idx_v, rows_v, sems, consume)
            pltpu.sync_copy(out_v, out_hbm.at[pl.ds(base, SC_TOKENS)])

    return pl.kernel(
        body,
        out_type=jax.ShapeDtypeStruct((N, D), jnp.float32),
        mesh=mesh,
        scratch_types=[
            pltpu.VMEM((SC_TOKENS, N_SEL), jnp.int32),
            pltpu.VMEM((SC_TOKENS, N_SEL), jnp.float32),
            pltpu.VMEM((SC_RING, TOPK, D), jnp.float32),
            pltpu.VMEM((SC_TOKENS, D), jnp.float32),
            pltpu.SemaphoreType.DMA((SC_RING,)),
        ],
        compiler_params=pltpu.CompilerParams(needs_layout_passes=False),
        name="peer_up",
    )(wgt, idx, table)


def _gate_act_kernel(pre_ref, gate_ref, o_ref):
    p = pre_ref[...]
    act = 0.5 * p * (1.0 + lax.erf(p * (1.0 / math.sqrt(2.0))))
    o_ref[...] = gate_ref[...] * act


def _gate_act(pre, gate):
    N = pre.shape[0]
    spec = pl.BlockSpec((EW_BLOCK, N_SEL), lambda i: (i, 0))
    return pl.pallas_call(
        _gate_act_kernel, grid=(N // EW_BLOCK,), in_specs=[spec, spec], out_specs=spec,
        out_shape=jax.ShapeDtypeStruct((N, N_SEL), jnp.float32), name="gate_act",
    )(pre, gate)


def _final_kernel(x1_ref, peer_ref, mod_ref, g_ref, o_ref, *, normalize):
    gate2 = mod_ref[0, 5:6, :]
    y = x1_ref[0] + gate2 * peer_ref[0]
    o_ref[0] = _rms(y, g_ref[...]) if normalize else y


def _residual(x1, peer, b0, mod3, g_final, normalize):
    B, S, D = x1.shape
    tok = pl.BlockSpec((1, EW_BLOCK, D), lambda b, j: (b, j, 0))
    return pl.pallas_call(
        functools.partial(_final_kernel, normalize=normalize),
        grid=(B, S // EW_BLOCK),
        in_specs=[tok, tok, pl.BlockSpec((1, N_MOD, D), lambda b, j: (b0 + b, 0, 0)),
                  pl.BlockSpec((1, D), lambda b, j: (0, 0))],
        out_specs=tok,
        out_shape=jax.ShapeDtypeStruct((B, S, D), jnp.float32),
        compiler_params=pltpu.CompilerParams(vmem_limit_bytes=VMEM_LIMIT_BYTES),
        name="final_norm",
    )(x1, peer, mod3, g_final)


def kernel(x, c, w_ada, b_ada, g_norm1, w_in, w_pool, pool_scale, sgu_ln_g, sgu_ln_b, w_spatial, b_spatial,
           w_out, g_norm2, w_query, sub_keys, expert_down, expert_up, g_final):
    B, S, D = x.shape
    depth = w_ada.shape[0]
    bf16 = jnp.bfloat16
    band = jnp.asarray(_pool_band(), bf16)
    c_pad = jnp.pad(c, ((0, 8 - B), (0, 0)))
    for l in range(depth):
        mod = _ada_mod(c_pad, w_ada[l].astype(bf16), b_ada[l][None, :])[:B]
        mod3 = mod.reshape(B, N_MOD, D)
        mix_w = (g_norm1[l][None, :], w_in[l].astype(bf16), band, w_pool[l].astype(bf16),
                 pool_scale[l][None, :], sgu_ln_g[l][None, :], sgu_ln_b[l][None, :], w_spatial[l],
                 b_spatial[l].T, w_out[l].astype(bf16), g_norm2[l][None, :])
        keys = sub_keys[l].reshape(2 * PEER_HEADS, N_KEYS, HALF_KEY).astype(bf16)
        wq = w_query[l].astype(bf16)
        outs = []
        for b in range(B):
            x1, h2 = _mixer(x, b, 1, mod3, *mix_w)
            h2f = h2.reshape(S, D)
            idx, gate = _query_topk(h2f, wq, keys)
            pre = _peer_down(h2f, idx, expert_down[l])
            wgt = _gate_act(pre, gate)
            peer = _peer_up(wgt, idx, expert_up[l]).reshape(1, S, D)
            outs.append(_residual(x1, peer, b, mod3, g_final[None, :], normalize=(l + 1 == depth)))
        x = jnp.concatenate(outs, axis=0)
    return x
```

```python
import functools
import math

import jax
import jax.numpy as jnp
import numpy as np
from jax import lax
from jax.experimental import pallas as pl
from jax.experimental.pallas import tpu as pltpu
from jax.experimental.pallas import tpu_sc as plsc

D_MODEL = 1024
POOL_WIDTH = 512
SGU_WIDTH = 512
POOL_WINDOWS = (2, 4, 8, 16)
GROUP_DIM = 128
CHUNK = 128
SGU_HEADS = 4
IN_PROJ_WIDTH = POOL_WIDTH + 2 * SGU_WIDTH
PEER_HEADS = 8
N_KEYS = 128
HALF_KEY = 128
TOPK = 16
N_MOD = 6
EPS = 1e-6

MIX_BLOCK = 512
TOPK_BLOCK = 256
EW_BLOCK = 1024
VMEM_LIMIT_BYTES = 48 * 1024 * 1024

SC_LANES = 16
SC_TOKENS = 32
SC_RING = 4
N_SEL = PEER_HEADS * TOPK

N_CAND = 16 + 7 * 8 + 8


def _pool_band():
    t = np.arange(CHUNK)[:, None]
    j = np.arange(2 * CHUNK)[None, :]
    bands = []
    for w in POOL_WINDOWS:
        m = (j > CHUNK + t - w) & (j <= CHUNK + t)
        bands.append(np.where(m, 1.0 / w, 0.0))
    return np.stack(bands).astype(np.float32)


def _bf16_dot(a, b):
    return jnp.dot(a.astype(jnp.bfloat16), b.astype(jnp.bfloat16), preferred_element_type=jnp.float32)


def _ada_kernel(c_ref, w_ref, b_ref, o_ref):
    c = c_ref[...]
    c_act = c * jax.nn.sigmoid(c)
    o_ref[...] = _bf16_dot(c_act, w_ref[...]) + b_ref[...]


def _ada_mod(c_pad, w_ada, b_ada):
    rows = c_pad.shape[0]
    return pl.pallas_call(
        _ada_kernel,
        grid=(N_MOD,),
        in_specs=[
            pl.BlockSpec((rows, D_MODEL), lambda i: (0, 0)),
            pl.BlockSpec((D_MODEL, D_MODEL), lambda i: (0, i)),
            pl.BlockSpec((1, D_MODEL), lambda i: (0, i)),
        ],
        out_specs=pl.BlockSpec((rows, D_MODEL), lambda i: (0, i)),
        out_shape=jax.ShapeDtypeStruct((rows, N_MOD * D_MODEL), jnp.float32),
        name="ada_mod",
    )(c_pad, w_ada, b_ada)


def _rms(x, g):
    return x * lax.rsqrt(jnp.mean(x * x, axis=-1, keepdims=True) + EPS) * g


def _mix_kernel(x_ref, mod_ref, g1_ref, win_ref, band_ref, wpool_ref, pscale_ref, lng_ref, lnb_ref,
                wsp_ref, bsp_ref, wout_ref, g2_ref, x1_ref, h2_ref, aext_ref, mixed_ref):
    j = pl.program_id(1)
    x = x_ref[0]
    shift1, scale1, gate1 = mod_ref[0, 0:1, :], mod_ref[0, 1:2, :], mod_ref[0, 2:3, :]
    shift2, scale2 = mod_ref[0, 3:4, :], mod_ref[0, 4:5, :]

    h = _rms(x, g1_ref[...]) * (1.0 + scale1) + shift1
    proj = _bf16_dot(h, win_ref[...])
    a = proj[:, :POOL_WIDTH]
    u = proj[:, POOL_WIDTH:POOL_WIDTH + SGU_WIDTH]
    v = proj[:, POOL_WIDTH + SGU_WIDTH:]

    @pl.when(j == 0)
    def _():
        aext_ref[0:CHUNK, :] = jnp.zeros((CHUNK, POOL_WIDTH), jnp.float32)

    aext_ref[CHUNK:, :] = a

    mu = jnp.mean(v, axis=-1, keepdims=True)
    vc = v - mu
    vn = vc * lax.rsqrt(jnp.mean(vc * vc, axis=-1, keepdims=True) + EPS) * lng_ref[...] + lnb_ref[...]

    row = lax.broadcasted_iota(jnp.int32, (CHUNK, CHUNK), 0)
    col = lax.broadcasted_iota(jnp.int32, (CHUNK, CHUNK), 1)
    tril = col <= row
    t_col = lax.broadcasted_iota(jnp.int32, (CHUNK, 1), 0)

    for c in range(MIX_BLOCK // CHUNK):
        r0 = c * CHUNK
        pos = (j * MIX_BLOCK + r0 + 1 + t_col).astype(jnp.float32)
        for g, w in enumerate(POOL_WINDOWS):
            c0 = g * GROUP_DIM
            seg = aext_ref[r0:r0 + 2 * CHUNK, c0:c0 + GROUP_DIM]
            hi = seg.astype(jnp.bfloat16)
            lo = (seg - hi.astype(jnp.float32)).astype(jnp.bfloat16)
            band = band_ref[g]
            win = (jnp.dot(band, hi, preferred_element_type=jnp.float32)
                   + jnp.dot(band, lo, preferred_element_type=jnp.float32))
            mean = win * (float(w) / jnp.minimum(pos, float(w)))
            pooled = mean - seg[CHUNK:, :]
            y = _bf16_dot(pooled, wpool_ref[g]) * pscale_ref[:, c0:c0 + GROUP_DIM]
            mixed_ref[r0:r0 + CHUNK, c0:c0 + GROUP_DIM] = y.astype(jnp.bfloat16)
        for hh in range(SGU_HEADS):
            c0 = hh * GROUP_DIM
            wm = jnp.where(tril, wsp_ref[hh], 0.0)
            m = _bf16_dot(wm, vn[r0:r0 + CHUNK, c0:c0 + GROUP_DIM]) + bsp_ref[:, hh:hh + 1]
            s = u[r0:r0 + CHUNK, c0:c0 + GROUP_DIM] * m
            mixed_ref[r0:r0 + CHUNK, POOL_WIDTH + c0:POOL_WIDTH + c0 + GROUP_DIM] = s.astype(jnp.bfloat16)

    aext_ref[0:CHUNK, :] = aext_ref[MIX_BLOCK:MIX_BLOCK + CHUNK, :]

    x1 = x + gate1 * jnp.dot(mixed_ref[...], wout_ref[...], preferred_element_type=jnp.float32)
    x1_ref[0] = x1
    h2_ref[0] = _rms(x1, g2_ref[...]) * (1.0 + scale2) + shift2


def _mixer(x, b0, n_seq, mod3, g1, w_in, band, w_pool, pool_scale, ln_g, ln_b, w_sp, b_sp_t, w_out, g2):
    _, S, D = x.shape
    const2 = lambda b, j: (0, 0)
    const3 = lambda b, j: (0, 0, 0)
    tok = pl.BlockSpec((1, MIX_BLOCK, D), lambda b, j: (b, j, 0))
    return pl.pallas_call(
        _mix_kernel,
        grid=(n_seq, S // MIX_BLOCK),
        in_specs=[
            pl.BlockSpec((1, MIX_BLOCK, D), lambda b, j: (b0 + b, j, 0)),
            pl.BlockSpec((1, N_MOD, D), lambda b, j: (b0 + b, 0, 0)),
            pl.BlockSpec((1, D), const2),
            pl.BlockSpec((D, IN_PROJ_WIDTH), const2),
            pl.BlockSpec((len(POOL_WINDOWS), CHUNK, 2 * CHUNK), const3),
            pl.BlockSpec((len(POOL_WINDOWS), GROUP_DIM, GROUP_DIM), const3),
            pl.BlockSpec((1, POOL_WIDTH), const2),
            pl.BlockSpec((1, SGU_WIDTH), const2),
            pl.BlockSpec((1, SGU_WIDTH), const2),
            pl.BlockSpec((SGU_HEADS, CHUNK, CHUNK), const3),
            pl.BlockSpec((CHUNK, SGU_HEADS), const2),
            pl.BlockSpec((D, D), const2),
            pl.BlockSpec((1, D), const2),
        ],
        out_specs=[tok, tok],
        out_shape=[jax.ShapeDtypeStruct((n_seq, S, D), jnp.float32)] * 2,
        scratch_shapes=[
            pltpu.VMEM((MIX_BLOCK + CHUNK, POOL_WIDTH), jnp.float32),
            pltpu.VMEM((MIX_BLOCK, D), jnp.bfloat16),
        ],
        compiler_params=pltpu.CompilerParams(
            dimension_semantics=("arbitrary", "arbitrary"), vmem_limit_bytes=VMEM_LIMIT_BYTES),
        name="mixer",
    )(x, mod3, g1, w_in, band, w_pool, pool_scale, ln_g, ln_b, w_sp, b_sp_t, w_out, g2)


def _extract_top(s, ids, payload, n_out):
    vals, pays = [], []
    for _ in range(n_out):
        m = jnp.max(s, axis=0, keepdims=True)
        pick = jnp.min(jnp.where(s == m, ids, jnp.float32(1e9)), axis=0, keepdims=True)
        sel = ids == pick
        vals.append(m)
        pays.append(pick if payload is None
                    else jnp.max(jnp.where(sel, payload, -1.0), axis=0, keepdims=True))
        s = jnp.where(sel, -jnp.inf, s)
    return jnp.concatenate(vals, axis=0), jnp.concatenate(pays, axis=0)


def _topk_kernel(h2_ref, wq_ref, keys_ref, idx_ref, gate_ref):
    T = TOPK_BLOCK
    q = _bf16_dot(h2_ref[...], wq_ref[...])
    key_ids = lax.broadcasted_iota(jnp.int32, (N_KEYS, T), 0).astype(jnp.float32)
    row = lax.broadcasted_iota(jnp.int32, (N_CAND, T), 0)
    flat = jnp.where(row < 16, row,
                     jnp.where(row < 72, jnp.right_shift(row - 8, 3) * 16 + jnp.bitwise_and(row, 7),
                               (row - 64) * 16)).astype(jnp.float32)
    idx_rows, gate_rows = [], []
    for hd in range(PEER_HEADS):
        tops = []
        for p in range(2):
            hp = hd * 2 + p
            qs = q[:, hp * HALF_KEY:(hp + 1) * HALF_KEY].astype(jnp.bfloat16)
            st = lax.dot_general(keys_ref[hp], qs, (((1,), (1,)), ((), ())),
                                 preferred_element_type=jnp.float32)
            tops.append(_extract_top(st, key_ids, None, TOPK))
        (s0, i0), (s1, i1) = tops
        i0 = i0 * float(N_KEYS)
        cand = jnp.concatenate(
            [s0[0:1] + s1] + [s0[a:a + 1] + s1[0:8] for a in range(1, 8)] + [s0[8:16] + s1[0:1]], axis=0)
        cidx = jnp.concatenate(
            [i0[0:1] + i1] + [i0[a:a + 1] + i1[0:8] for a in range(1, 8)] + [i0[8:16] + i1[0:1]],
            axis=0)
        best, sel_idx = _extract_top(cand, flat, cidx, TOPK)
        e = jnp.exp(best - jnp.max(best, axis=0, keepdims=True))
        gate_rows.append(e / jnp.sum(e, axis=0, keepdims=True))
        idx_rows.append(sel_idx)
    idx_ref[...] = jnp.concatenate(idx_rows, axis=0).T.astype(jnp.int32)
    gate_ref[...] = jnp.concatenate(gate_rows, axis=0).T


def _query_topk(h2, w_query, keys):
    N, D = h2.shape
    tok = pl.BlockSpec((TOPK_BLOCK, N_SEL), lambda i: (i, 0))
    return pl.pallas_call(
        _topk_kernel,
        grid=(N // TOPK_BLOCK,),
        in_specs=[
            pl.BlockSpec((TOPK_BLOCK, D), lambda i: (i, 0)),
            pl.BlockSpec((D, 2 * PEER_HEADS * HALF_KEY), lambda i: (0, 0)),
            pl.BlockSpec((2 * PEER_HEADS, N_KEYS, HALF_KEY), lambda i: (0, 0, 0)),
        ],
        out_specs=[tok, tok],
        out_shape=[jax.ShapeDtypeStruct((N, N_SEL), jnp.int32), jax.ShapeDtypeStruct((N, N_SEL), jnp.float32)],
        compiler_params=pltpu.CompilerParams(
            dimension_semantics=("arbitrary",), vmem_limit_bytes=VMEM_LIMIT_BYTES),
        name="query_topk",
    )(h2, w_query, keys)


def _sc_mesh_and_workers():
    info = plsc.get_sparse_core_info()
    assert info.num_lanes == SC_LANES
    mesh = plsc.VectorSubcoreMesh(core_axis_name="core", subcore_axis_name="subcore")
    return mesh, info.num_cores, info.num_cores * info.num_subcores


def _gather_ring(tab_hbm, idx_v, rows_v, sems, consume):
    look = SC_RING - 1

    def copy(t, hd):
        slot = hd % SC_RING
        return pltpu.make_async_copy(
            tab_hbm.at[idx_v.at[t, pl.ds(hd * TOPK, TOPK)]], rows_v.at[slot], sems.at[slot])

    for hd in range(look):
        copy(0, hd).start()

    @pl.loop(0, SC_TOKENS)
    def _(t):
        for hd in range(PEER_HEADS):
            nxt = hd + look
            if nxt < PEER_HEADS:
                copy(t, nxt).start()
            else:
                @pl.when(t + 1 < SC_TOKENS)
                def _():
                    copy(t + 1, nxt - PEER_HEADS).start()
            copy(t, hd).wait()
            consume(t, hd, rows_v.at[hd % SC_RING])


def _peer_down(h2, idx, table):
    N, D = h2.shape
    mesh, n_cores, n_workers = _sc_mesh_and_workers()
    per_worker = N // n_workers
    n_slices = D // SC_LANES

    def body(h_hbm, idx_hbm, tab_hbm, out_hbm, idx_v, h_v, rows_v, out_v, tr_v, sems):
        wid = lax.axis_index("subcore") * n_cores + lax.axis_index("core")
        lane = lax.iota(jnp.int32, SC_LANES)

        def consume(t, hd, rows):
            def dot_step(s, accs):
                hv = h_v[t, pl.ds(s * SC_LANES, SC_LANES)]
                return tuple(acc + rows[k, pl.ds(s * SC_LANES, SC_LANES)] * hv for k, acc in enumerate(accs))

            accs = lax.fori_loop(0, n_slices, dot_step,
                                 tuple(jnp.zeros((SC_LANES,), jnp.float32) for _ in range(TOPK)))
            for k in range(TOPK):
                tr_v[k, :] = accs[k]
            tot = jnp.zeros((SC_LANES,), jnp.float32)
            for l in range(SC_LANES):
                tot = tot + plsc.load_gather(tr_v, [lane, jnp.full((SC_LANES,), l, jnp.int32)])
            out_v[t, pl.ds(hd * TOPK, TOPK)] = tot

        @pl.loop(0, per_worker // SC_TOKENS)
        def _(blk):
            base = wid * per_worker + blk * SC_TOKENS
            pltpu.sync_copy(idx_hbm.at[pl.ds(base, SC_TOKENS)], idx_v)
            pltpu.sync_copy(h_hbm.at[pl.ds(base, SC_TOKENS)], h_v)
            _gather_ring(tab_hbm, idx_v, rows_v, sems, consume)
            pltpu.sync_copy(out_v, out_hbm.at[pl.ds(base, SC_TOKENS)])

    return pl.kernel(
        body,
        out_type=jax.ShapeDtypeStruct((N, N_SEL), jnp.float32),
        mesh=mesh,
        scratch_types=[
            pltpu.VMEM((SC_TOKENS, N_SEL), jnp.int32),
            pltpu.VMEM((SC_TOKENS, D), jnp.float32),
            pltpu.VMEM((SC_RING, TOPK, D), jnp.float32),
            pltpu.VMEM((SC_TOKENS, N_SEL), jnp.float32),
            pltpu.VMEM((TOPK, SC_LANES), jnp.float32),
            pltpu.SemaphoreType.DMA((SC_RING,)),
        ],
        compiler_params=pltpu.CompilerParams(needs_layout_passes=False),
        name="peer_down",
    )(h2, idx, table)


def _peer_up(wgt, idx, table):
    N = wgt.shape[0]
    D = table.shape[1]
    mesh, n_cores, n_workers = _sc_mesh_and_workers()
    per_worker = N // n_workers
    n_slices = D // SC_LANES

    def body(w_hbm, idx_hbm, tab_hbm, out_hbm, idx_v, w_v, rows_v, out_v, sems):
        wid = lax.axis_index("subcore") * n_cores + lax.axis_index("core")

        def consume(t, hd, rows):
            t_vec = jnp.full((SC_LANES,), t, jnp.int32)
            ws = [plsc.load_gather(w_v, [t_vec, jnp.full((SC_LANES,), hd * TOPK + k, jnp.int32)])
                  for k in range(TOPK)]

            @plsc.parallel_loop(0, n_slices, unroll=2)
            def _(s):
                sl = pl.ds(s * SC_LANES, SC_LANES)
                terms = [ws[k] * rows[k, sl] for k in range(TOPK)]
                if hd > 0:
                    terms.append(out_v[t, sl])
                while len(terms) > 1:
                    terms = [terms[i] + terms[i + 1] for i in range(0, len(terms) - 1, 2)] + (
                        [terms[-1]] if len(terms) % 2 else [])
                out_v[t, sl] = terms[0]

        @pl.loop(0, per_worker // SC_TOKENS)
        def _(blk):
            base = wid * per_worker + blk * SC_TOKENS
            pltpu.sync_copy(idx_hbm.at[pl.ds(base, SC_TOKENS)], idx_v)
            pltpu.sync_copy(w_hbm.at[pl.ds(base, SC_TOKENS)], w_v)
            _gather_ring(tab_hbm, idx_v, rows_v, sems, consume)
            pltpu.sync_copy(out_v, out_hbm.at[pl.ds(base, SC_TOKENS)])

    return pl.kernel(
        body,
        out_type=jax.ShapeDtypeStruct((N, D), jnp.float32),
        mesh=mesh,
        scratch_types=[
            pltpu.VMEM((SC_TOKENS, N_SEL), jnp.int32),
            pltpu.VMEM((SC_TOKENS, N_SEL), jnp.float32),
            pltpu.VMEM((SC_RING, TOPK, D), jnp.float32),
            pltpu.VMEM((SC_TOKENS, D), jnp.float32),
            pltpu.SemaphoreType.DMA((SC_RING,)),
        ],
        compiler_params=pltpu.CompilerParams(needs_layout_passes=False),
        name="peer_up",
    )(wgt, idx, table)


def _gate_act_kernel(pre_ref, gate_ref, o_ref):
    p = pre_ref[...]
    act = 0.5 * p * (1.0 + lax.erf(p * (1.0 / math.sqrt(2.0))))
    o_ref[...] = gate_ref[...] * act


def _gate_act(pre, gate):
    N = pre.shape[0]
    spec = pl.BlockSpec((EW_BLOCK, N_SEL), lambda i: (i, 0))
    return pl.pallas_call(
        _gate_act_kernel, grid=(N // EW_BLOCK,), in_specs=[spec, spec], out_specs=spec,
        out_shape=jax.ShapeDtypeStruct((N, N_SEL), jnp.float32), name="gate_act",
    )(pre, gate)


def _final_kernel(x1_ref, peer_ref, mod_ref, g_ref, o_ref, *, normalize):
    gate2 = mod_ref[0, 5:6, :]
    y = x1_ref[0] + gate2 * peer_ref[0]
    o_ref[0] = _rms(y, g_ref[...]) if normalize else y


def _residual(x1, peer, b0, mod3, g_final, normalize):
    B, S, D = x1.shape
    tok = pl.BlockSpec((1, EW_BLOCK, D), lambda b, j: (b, j, 0))
    return pl.pallas_call(
        functools.partial(_final_kernel, normalize=normalize),
        grid=(B, S // EW_BLOCK),
        in_specs=[tok, tok, pl.BlockSpec((1, N_MOD, D), lambda b, j: (b0 + b, 0, 0)),
                  pl.BlockSpec((1, D), lambda b, j: (0, 0))],
        out_specs=tok,
        out_shape=jax.ShapeDtypeStruct((B, S, D), jnp.float32),
        compiler_params=pltpu.CompilerParams(vmem_limit_bytes=VMEM_LIMIT_BYTES),
        name="final_norm",
    )(x1, peer, mod3, g_final)


def kernel(x, c, w_ada, b_ada, g_norm1, w_in, w_pool, pool_scale, sgu_ln_g, sgu_ln_b, w_spatial, b_spatial,
           w_out, g_norm2, w_query, sub_keys, expert_down, expert_up, g_final):
    B, S, D = x.shape
    depth = w_ada.shape[0]
    bf16 = jnp.bfloat16
    band = jnp.asarray(_pool_band(), bf16)
    c_pad = jnp.pad(c, ((0, 8 - B), (0, 0)))
    for l in range(depth):
        mod = _ada_mod(c_pad, w_ada[l].astype(bf16), b_ada[l][None, :])[:B]
        mod3 = mod.reshape(B, N_MOD, D)
        mix_w = (g_norm1[l][None, :], w_in[l].astype(bf16), band, w_pool[l].astype(bf16),
                 pool_scale[l][None, :], sgu_ln_g[l][None, :], sgu_ln_b[l][None, :], w_spatial[l],
                 b_spatial[l].T, w_out[l].astype(bf16), g_norm2[l][None, :])
        keys = sub_keys[l].reshape(2 * PEER_HEADS, N_KEYS, HALF_KEY).astype(bf16)
        wq = w_query[l].astype(bf16)
        outs = []
        for b in range(B):
            x1, h2 = _mixer(x, b, 1, mod3, *mix_w)
            h2f = h2.reshape(S, D)
            idx, gate = _query_topk(h2f, wq, keys)
            pre = _peer_down(h2f, idx, expert_down[l])
            wgt = _gate_act(pre, gate)
            peer = _peer_up(wgt, idx, expert_up[l]).reshape(1, S, D)
            outs.append(_residual(x1, peer, b, mod3, g_final[None, :], normalize=(l + 1 == depth)))
        x = jnp.concatenate(outs, axis=0)
    return x
```

```python
import functools
import math

import jax
import jax.numpy as jnp
import numpy as np
from jax import lax
from jax.experimental import pallas as pl
from jax.experimental.pallas import tpu as pltpu
from jax.experimental.pallas import tpu_sc as plsc

D_MODEL = 1024
POOL_WIDTH = 512
SGU_WIDTH = 512
POOL_WINDOWS = (2, 4, 8, 16)
GROUP_DIM = 128
CHUNK = 128
SGU_HEADS = 4
IN_PROJ_WIDTH = POOL_WIDTH + 2 * SGU_WIDTH
PEER_HEADS = 8
N_KEYS = 128
HALF_KEY = 128
TOPK = 16
N_MOD = 6
EPS = 1e-6

MIX_BLOCK = 512
TOPK_BLOCK = 256
EW_BLOCK = 1024
VMEM_LIMIT_BYTES = 48 * 1024 * 1024

SC_LANES = 16
SC_TOKENS = 32
SC_RING = 4
N_SEL = PEER_HEADS * TOPK
TC_TOKENS = 8
SC_SHARE_NUM, SC_SHARE_DEN = 3, 4

N_CAND = 16 + 7 * 8 + 8


def _pool_band():
    t = np.arange(CHUNK)[:, None]
    j = np.arange(2 * CHUNK)[None, :]
    bands = []
    for w in POOL_WINDOWS:
        m = (j > CHUNK + t - w) & (j <= CHUNK + t)
        bands.append(np.where(m, 1.0 / w, 0.0))
    return np.stack(bands).astype(np.float32)


def _bf16_dot(a, b):
    return jnp.dot(a.astype(jnp.bfloat16), b.astype(jnp.bfloat16), preferred_element_type=jnp.float32)


def _ada_kernel(c_ref, w_ref, b_ref, o_ref):
    c = c_ref[...]
    c_act = c * jax.nn.sigmoid(c)
    o_ref[...] = _bf16_dot(c_act, w_ref[...]) + b_ref[...]


def _ada_mod(c_pad, w_ada, b_ada):
    rows = c_pad.shape[0]
    return pl.pallas_call(
        _ada_kernel,
        grid=(N_MOD,),
        in_specs=[
            pl.BlockSpec((rows, D_MODEL), lambda i: (0, 0)),
            pl.BlockSpec((D_MODEL, D_MODEL), lambda i: (0, i)),
            pl.BlockSpec((1, D_MODEL), lambda i: (0, i)),
        ],
        out_specs=pl.BlockSpec((rows, D_MODEL), lambda i: (0, i)),
        out_shape=jax.ShapeDtypeStruct((rows, N_MOD * D_MODEL), jnp.float32),
        name="ada_mod",
    )(c_pad, w_ada, b_ada)


def _rms(x, g):
    return x * lax.rsqrt(jnp.mean(x * x, axis=-1, keepdims=True) + EPS) * g


def _mix_kernel(x_ref, mod_ref, g1_ref, win_ref, band_ref, wpool_ref, pscale_ref, lng_ref, lnb_ref,
                wsp_ref, bsp_ref, wout_ref, g2_ref, x1_ref, h2_ref, aext_ref, mixed_ref):
    j = pl.program_id(1)
    x = x_ref[0]
    shift1, scale1, gate1 = mod_ref[0, 0:1, :], mod_ref[0, 1:2, :], mod_ref[0, 2:3, :]
    shift2, scale2 = mod_ref[0, 3:4, :], mod_ref[0, 4:5, :]

    h = _rms(x, g1_ref[...]) * (1.0 + scale1) + shift1
    proj = _bf16_dot(h, win_ref[...])
    a = proj[:, :POOL_WIDTH]
    u = proj[:, POOL_WIDTH:POOL_WIDTH + SGU_WIDTH]
    v = proj[:, POOL_WIDTH + SGU_WIDTH:]

    @pl.when(j == 0)
    def _():
        aext_ref[0:CHUNK, :] = jnp.zeros((CHUNK, POOL_WIDTH), jnp.float32)

    aext_ref[CHUNK:, :] = a

    mu = jnp.mean(v, axis=-1, keepdims=True)
    vc = v - mu
    vn = vc * lax.rsqrt(jnp.mean(vc * vc, axis=-1, keepdims=True) + EPS) * lng_ref[...] + lnb_ref[...]

    row = lax.broadcasted_iota(jnp.int32, (CHUNK, CHUNK), 0)
    col = lax.broadcasted_iota(jnp.int32, (CHUNK, CHUNK), 1)
    tril = col <= row
    t_col = lax.broadcasted_iota(jnp.int32, (CHUNK, 1), 0)

    for c in range(MIX_BLOCK // CHUNK):
        r0 = c * CHUNK
        pos = (j * MIX_BLOCK + r0 + 1 + t_col).astype(jnp.float32)
        for g, w in enumerate(POOL_WINDOWS):
            c0 = g * GROUP_DIM
            seg = aext_ref[r0:r0 + 2 * CHUNK, c0:c0 + GROUP_DIM]
            hi = seg.astype(jnp.bfloat16)
            lo = (seg - hi.astype(jnp.float32)).astype(jnp.bfloat16)
            band = band_ref[g]
            win = (jnp.dot(band, hi, preferred_element_type=jnp.float32)
                   + jnp.dot(band, lo, preferred_element_type=jnp.float32))
            mean = win * (float(w) / jnp.minimum(pos, float(w)))
            pooled = mean - seg[CHUNK:, :]
            y = _bf16_dot(pooled, wpool_ref[g]) * pscale_ref[:, c0:c0 + GROUP_DIM]
            mixed_ref[r0:r0 + CHUNK, c0:c0 + GROUP_DIM] = y.astype(jnp.bfloat16)
        for hh in range(SGU_HEADS):
            c0 = hh * GROUP_DIM
            wm = jnp.where(tril, wsp_ref[hh], 0.0)
            m = _bf16_dot(wm, vn[r0:r0 + CHUNK, c0:c0 + GROUP_DIM]) + bsp_ref[:, hh:hh + 1]
            s = u[r0:r0 + CHUNK, c0:c0 + GROUP_DIM] * m
            mixed_ref[r0:r0 + CHUNK, POOL_WIDTH + c0:POOL_WIDTH + c0 + GROUP_DIM] = s.astype(jnp.bfloat16)

    aext_ref[0:CHUNK, :] = aext_ref[MIX_BLOCK:MIX_BLOCK + CHUNK, :]

    x1 = x + gate1 * jnp.dot(mixed_ref[...], wout_ref[...], preferred_element_type=jnp.float32)
    x1_ref[0] = x1
    h2_ref[0] = _rms(x1, g2_ref[...]) * (1.0 + scale2) + shift2


def _mixer(x, b0, n_seq, mod3, g1, w_in, band, w_pool, pool_scale, ln_g, ln_b, w_sp, b_sp_t, w_out, g2):
    _, S, D = x.shape
    const2 = lambda b, j: (0, 0)
    const3 = lambda b, j: (0, 0, 0)
    tok = pl.BlockSpec((1, MIX_BLOCK, D), lambda b, j: (b, j, 0))
    return pl.pallas_call(
        _mix_kernel,
        grid=(n_seq, S // MIX_BLOCK),
        in_specs=[
            pl.BlockSpec((1, MIX_BLOCK, D), lambda b, j: (b0 + b, j, 0)),
            pl.BlockSpec((1, N_MOD, D), lambda b, j: (b0 + b, 0, 0)),
            pl.BlockSpec((1, D), const2),
            pl.BlockSpec((D, IN_PROJ_WIDTH), const2),
            pl.BlockSpec((len(POOL_WINDOWS), CHUNK, 2 * CHUNK), const3),
            pl.BlockSpec((len(POOL_WINDOWS), GROUP_DIM, GROUP_DIM), const3),
            pl.BlockSpec((1, POOL_WIDTH), const2),
            pl.BlockSpec((1, SGU_WIDTH), const2),
            pl.BlockSpec((1, SGU_WIDTH), const2),
            pl.BlockSpec((SGU_HEADS, CHUNK, CHUNK), const3),
            pl.BlockSpec((CHUNK, SGU_HEADS), const2),
            pl.BlockSpec((D, D), const2),
            pl.BlockSpec((1, D), const2),
        ],
        out_specs=[tok, tok],
        out_shape=[jax.ShapeDtypeStruct((n_seq, S, D), jnp.float32)] * 2,
        scratch_shapes=[
            pltpu.VMEM((MIX_BLOCK + CHUNK, POOL_WIDTH), jnp.float32),
            pltpu.VMEM((MIX_BLOCK, D), jnp.bfloat16),
        ],
        compiler_params=pltpu.CompilerParams(
            dimension_semantics=("arbitrary", "arbitrary"), vmem_limit_bytes=VMEM_LIMIT_BYTES),
        name="mixer",
    )(x, mod3, g1, w_in, band, w_pool, pool_scale, ln_g, ln_b, w_sp, b_sp_t, w_out, g2)


def _extract_top(s, ids, payload, n_out):
    vals, pays = [], []
    for _ in range(n_out):
        m = jnp.max(s, axis=0, keepdims=True)
        pick = jnp.min(jnp.where(s == m, ids, jnp.float32(1e9)), axis=0, keepdims=True)
        sel = ids == pick
        vals.append(m)
        pays.append(pick if payload is None
                    else jnp.max(jnp.where(sel, payload, -1.0), axis=0, keepdims=True))
        s = jnp.where(sel, -jnp.inf, s)
    return jnp.concatenate(vals, axis=0), jnp.concatenate(pays, axis=0)


def _topk_kernel(h2_ref, wq_ref, keys_ref, idx_ref, gate_ref):
    T = TOPK_BLOCK
    q = _bf16_dot(h2_ref[...], wq_ref[...])
    key_ids = lax.broadcasted_iota(jnp.int32, (N_KEYS, T), 0).astype(jnp.float32)
    row = lax.broadcasted_iota(jnp.int32, (N_CAND, T), 0)
    flat = jnp.where(row < 16, row,
                     jnp.where(row < 72, jnp.right_shift(row - 8, 3) * 16 + jnp.bitwise_and(row, 7),
                               (row - 64) * 16)).astype(jnp.float32)
    idx_rows, gate_rows = [], []
    for hd in range(PEER_HEADS):
        tops = []
        for p in range(2):
            hp = hd * 2 + p
            qs = q[:, hp * HALF_KEY:(hp + 1) * HALF_KEY].astype(jnp.bfloat16)
            st = lax.dot_general(keys_ref[hp], qs, (((1,), (1,)), ((), ())),
                                 preferred_element_type=jnp.float32)
            tops.append(_extract_top(st, key_ids, None, TOPK))
        (s0, i0), (s1, i1) = tops
        i0 = i0 * float(N_KEYS)
        cand = jnp.concatenate(
            [s0[0:1] + s1] + [s0[a:a + 1] + s1[0:8] for a in range(1, 8)] + [s0[8:16] + s1[0:1]], axis=0)
        cidx = jnp.concatenate(
            [i0[0:1] + i1] + [i0[a:a + 1] + i1[0:8] for a in range(1, 8)] + [i0[8:16] + i1[0:1]],
            axis=0)
        best, sel_idx = _extract_top(cand, flat, cidx, TOPK)
        e = jnp.exp(best - jnp.max(best, axis=0, keepdims=True))
        gate_rows.append(e / jnp.sum(e, axis=0, keepdims=True))
        idx_rows.append(sel_idx)
    idx_ref[...] = jnp.concatenate(idx_rows, axis=0).T.astype(jnp.int32)
    gate_ref[...] = jnp.concatenate(gate_rows, axis=0).T


def _query_topk(h2, w_query, keys):
    N, D = h2.shape
    tok = pl.BlockSpec((TOPK_BLOCK, N_SEL), lambda i: (i, 0))
    return pl.pallas_call(
        _topk_kernel,
        grid=(N // TOPK_BLOCK,),
        in_specs=[
            pl.BlockSpec((TOPK_BLOCK, D), lambda i: (i, 0)),
            pl.BlockSpec((D, 2 * PEER_HEADS * HALF_KEY), lambda i: (0, 0)),
            pl.BlockSpec((2 * PEER_HEADS, N_KEYS, HALF_KEY), lambda i: (0, 0, 0)),
        ],
        out_specs=[tok, tok],
        out_shape=[jax.ShapeDtypeStruct((N, N_SEL), jnp.int32), jax.ShapeDtypeStruct((N, N_SEL), jnp.float32)],
        compiler_params=pltpu.CompilerParams(
            dimension_semantics=("arbitrary",), vmem_limit_bytes=VMEM_LIMIT_BYTES),
        name="query_topk",
    )(h2, w_query, keys)


def _sc_mesh_and_workers():
    info = plsc.get_sparse_core_info()
    assert info.num_lanes == SC_LANES
    mesh = plsc.VectorSubcoreMesh(core_axis_name="core", subcore_axis_name="subcore")
    return mesh, info.num_cores, info.num_cores * info.num_subcores


def _gather_ring(tab_hbm, idx_v, rows_v, sems, consume):
    look = SC_RING - 1

    def copy(t, hd):
        slot = hd % SC_RING
        return pltpu.make_async_copy(
            tab_hbm.at[idx_v.at[t, pl.ds(hd * TOPK, TOPK)]], rows_v.at[slot], sems.at[slot])

    for hd in range(look):
        copy(0, hd).start()

    @pl.loop(0, SC_TOKENS)
    def _(t):
        for hd in range(PEER_HEADS):
            nxt = hd + look
            if nxt < PEER_HEADS:
                copy(t, nxt).start()
            else:
                @pl.when(t + 1 < SC_TOKENS)
                def _():
                    copy(t + 1, nxt - PEER_HEADS).start()
            copy(t, hd).wait()
            consume(t, hd, rows_v.at[hd % SC_RING])


def _peer_down(h2, idx, table, N):
    D = h2.shape[1]
    mesh, n_cores, n_workers = _sc_mesh_and_workers()
    per_worker = N // n_workers
    n_slices = D // SC_LANES

    def body(h_hbm, idx_hbm, tab_hbm, out_hbm, idx_v, h_v, rows_v, out_v, tr_v, sems):
        wid = lax.axis_index("subcore") * n_cores + lax.axis_index("core")
        lane = lax.iota(jnp.int32, SC_LANES)

        def consume(t, hd, rows):
            def dot_step(s, accs):
                hv = h_v[t, pl.ds(s * SC_LANES, SC_LANES)]
                return tuple(acc + rows[k, pl.ds(s * SC_LANES, SC_LANES)] * hv for k, acc in enumerate(accs))

            accs = lax.fori_loop(0, n_slices, dot_step,
                                 tuple(jnp.zeros((SC_LANES,), jnp.float32) for _ in range(TOPK)))
            for k in range(TOPK):
                tr_v[k, :] = accs[k]
            tot = jnp.zeros((SC_LANES,), jnp.float32)
            for l in range(SC_LANES):
                tot = tot + plsc.load_gather(tr_v, [lane, jnp.full((SC_LANES,), l, jnp.int32)])
            out_v[t, pl.ds(hd * TOPK, TOPK)] = tot

        @pl.loop(0, per_worker // SC_TOKENS)
        def _(blk):
            base = wid * per_worker + blk * SC_TOKENS
            pltpu.sync_copy(idx_hbm.at[pl.ds(base, SC_TOKENS)], idx_v)
            pltpu.sync_copy(h_hbm.at[pl.ds(base, SC_TOKENS)], h_v)
            _gather_ring(tab_hbm, idx_v, rows_v, sems, consume)
            pltpu.sync_copy(out_v, out_hbm.at[pl.ds(base, SC_TOKENS)])

    return pl.kernel(
        body,
        out_type=jax.ShapeDtypeStruct((N, N_SEL), jnp.float32),
        mesh=mesh,
        scratch_types=[
            pltpu.VMEM((SC_TOKENS, N_SEL), jnp.int32),
            pltpu.VMEM((SC_TOKENS, D), jnp.float32),
            pltpu.VMEM((SC_RING, TOPK, D), jnp.float32),
            pltpu.VMEM((SC_TOKENS, N_SEL), jnp.float32),
            pltpu.VMEM((TOPK, SC_LANES), jnp.float32),
            pltpu.SemaphoreType.DMA((SC_RING,)),
        ],
        compiler_params=pltpu.CompilerParams(needs_layout_passes=False),
        name="peer_down",
    )(h2, idx, table)


def _peer_up(wgt, idx, table, N):
    D = table.shape[1]
    mesh, n_cores, n_workers = _sc_mesh_and_workers()
    per_worker = N // n_workers
    n_slices = D // SC_LANES

    def body(w_hbm, idx_hbm, tab_hbm, out_hbm, idx_v, w_v, rows_v, out_v, sems):
        wid = lax.axis_index("subcore") * n_cores + lax.axis_index("core")

        def consume(t, hd, rows):
            t_vec = jnp.full((SC_LANES,), t, jnp.int32)
            ws = [plsc.load_gather(w_v, [t_vec, jnp.full((SC_LANES,), hd * TOPK + k, jnp.int32)])
                  for k in range(TOPK)]

            @plsc.parallel_loop(0, n_slices, unroll=2)
            def _(s):
                sl = pl.ds(s * SC_LANES, SC_LANES)
                terms = [ws[k] * rows[k, sl] for k in range(TOPK)]
                if hd > 0:
                    terms.append(out_v[t, sl])
                while len(terms) > 1:
                    terms = [terms[i] + terms[i + 1] for i in range(0, len(terms) - 1, 2)] + (
                        [terms[-1]] if len(terms) % 2 else [])
                out_v[t, sl] = terms[0]

        @pl.loop(0, per_worker // SC_TOKENS)
        def _(blk):
            base = wid * per_worker + blk * SC_TOKENS
            pltpu.sync_copy(idx_hbm.at[pl.ds(base, SC_TOKENS)], idx_v)
            pltpu.sync_copy(w_hbm.at[pl.ds(base, SC_TOKENS)], w_v)
            _gather_ring(tab_hbm, idx_v, rows_v, sems, consume)
            pltpu.sync_copy(out_v, out_hbm.at[pl.ds(base, SC_TOKENS)])

    return pl.kernel(
        body,
        out_type=jax.ShapeDtypeStruct((N, D), jnp.float32),
        mesh=mesh,
        scratch_types=[
            pltpu.VMEM((SC_TOKENS, N_SEL), jnp.int32),
            pltpu.VMEM((SC_TOKENS, N_SEL), jnp.float32),
            pltpu.VMEM((SC_RING, TOPK, D), jnp.float32),
            pltpu.VMEM((SC_TOKENS, D), jnp.float32),
            pltpu.SemaphoreType.DMA((SC_RING,)),
        ],
        compiler_params=pltpu.CompilerParams(needs_layout_passes=False),
        name="peer_up",
    )(wgt, idx, table)


def _peer_tc_kernel(idx_cur, idx_nxt, h_ref, gate_ref, tab_hbm, o_ref, gbuf, sems):
    i = pl.program_id(0)
    n = pl.num_programs(0)
    groups = N_SEL // 8

    def issue(idx_ref, slot):
        def body(g, carry):
            for j in range(8):
                e = idx_ref[g * 8 + j]
                pltpu.make_async_copy(
                    tab_hbm.at[pl.ds(e, 1), :], gbuf.at[slot, g, pl.ds(j, 1), :], sems.at[slot]).start()
            return carry
        lax.fori_loop(0, TC_TOKENS * groups, body, 0)

    @pl.when(i == 0)
    def _():
        issue(idx_cur, 0)

    @pl.when(i + 1 < n)
    def _():
        issue(idx_nxt, (i + 1) % 2)

    slot = i % 2
    pltpu.make_async_copy(gbuf.at[slot], gbuf.at[slot], sems.at[slot]).wait()

    def rows(t, c0):
        return gbuf[slot, t * groups:(t + 1) * groups, :, c0:c0 + D_MODEL].reshape(N_SEL, D_MODEL)

    lane = lax.broadcasted_iota(jnp.int32, (N_SEL, 128), 1)
    pre_all = jnp.zeros((N_SEL, 128), jnp.float32)
    for t in range(TC_TOKENS):
        prod = rows(t, 0) * h_ref[t:t + 1, :]
        part = prod[:, 0:128]
        for c in range(1, D_MODEL // 128):
            part = part + prod[:, c * 128:(c + 1) * 128]
        pre = jnp.sum(part, axis=1, keepdims=True)
        pre_all = jnp.where(lane == t, pre, pre_all)
    act = 0.5 * pre_all * (1.0 + lax.erf(pre_all * (1.0 / math.sqrt(2.0))))
    gate_t = jnp.concatenate([gate_ref[...]] * (128 // TC_TOKENS), axis=0).T
    wgt = gate_t * act
    for t in range(TC_TOKENS):
        w_col = jnp.sum(jnp.where(lane == t, wgt, 0.0), axis=1, keepdims=True)
        up = rows(t, D_MODEL) * w_col
        o_ref[t:t + 1, :] = jnp.sum(up, axis=0, keepdims=True)


def _peer_tc(h2, idx, gate, table2, row0, n_tokens):
    D = h2.shape[1]
    steps = n_tokens // TC_TOKENS
    blk0 = row0 // TC_TOKENS
    idx_flat = idx.reshape(-1)
    return pl.pallas_call(
        _peer_tc_kernel,
        grid=(steps,),
        in_specs=[
            pl.BlockSpec((TC_TOKENS * N_SEL,), lambda i: (blk0 + i,), memory_space=pltpu.SMEM),
            pl.BlockSpec((TC_TOKENS * N_SEL,), lambda i: (blk0 + jnp.minimum(i + 1, steps - 1),),
                         memory_space=pltpu.SMEM),
            pl.BlockSpec((TC_TOKENS, D), lambda i: (blk0 + i, 0)),
            pl.BlockSpec((TC_TOKENS, N_SEL), lambda i: (blk0 + i, 0)),
            pl.BlockSpec(memory_space=pl.ANY),
        ],
        out_specs=pl.BlockSpec((TC_TOKENS, D), lambda i: (i, 0)),
        out_shape=jax.ShapeDtypeStruct((n_tokens, D), jnp.float32),
        scratch_shapes=[
            pltpu.VMEM((2, TC_TOKENS * N_SEL // 8, 8, 2 * D), jnp.float32),
            pltpu.SemaphoreType.DMA((2,)),
        ],
        compiler_params=pltpu.CompilerParams(
            dimension_semantics=("arbitrary",), vmem_limit_bytes=VMEM_LIMIT_BYTES),
        name="peer_tc",
    )(idx_flat, idx_flat, h2, gate, table2)


def _gate_act_kernel(pre_ref, gate_ref, o_ref):
    p = pre_ref[...]
    act = 0.5 * p * (1.0 + lax.erf(p * (1.0 / math.sqrt(2.0))))
    o_ref[...] = gate_ref[...] * act


def _gate_act(pre, gate):
    N = pre.shape[0]
    spec = pl.BlockSpec((EW_BLOCK, N_SEL), lambda i: (i, 0))
    return pl.pallas_call(
        _gate_act_kernel, grid=(N // EW_BLOCK,), in_specs=[spec, spec], out_specs=spec,
        out_shape=jax.ShapeDtypeStruct((N, N_SEL), jnp.float32), name="gate_act",
    )(pre, gate)


def _final_kernel(x1_ref, peer_a_ref, peer_b_ref, mod_ref, g_ref, o_ref, *, normalize, blocks_a):
    gate2 = mod_ref[0, 5:6, :]
    peer = jnp.where(pl.program_id(0) < blocks_a, peer_a_ref[...], peer_b_ref[...])
    y = x1_ref[0] + gate2 * peer
    o_ref[0] = _rms(y, g_ref[...]) if normalize else y


def _residual(x1, peer_a, peer_b, b, mod3, g_final, normalize):
    _, S, D = x1.shape
    blocks_a = peer_a.shape[0] // EW_BLOCK
    blocks_b = peer_b.shape[0] // EW_BLOCK
    tok = pl.BlockSpec((1, EW_BLOCK, D), lambda j: (0, j, 0))
    return pl.pallas_call(
        functools.partial(_final_kernel, normalize=normalize, blocks_a=blocks_a),
        grid=(S // EW_BLOCK,),
        in_specs=[tok,
                  pl.BlockSpec((EW_BLOCK, D), lambda j: (jnp.minimum(j, blocks_a - 1), 0)),
                  pl.BlockSpec((EW_BLOCK, D), lambda j: (jnp.clip(j - blocks_a, 0, blocks_b - 1), 0)),
                  pl.BlockSpec((1, N_MOD, D), lambda j: (b, 0, 0)),
                  pl.BlockSpec((1, D), lambda j: (0, 0))],
        out_specs=tok,
        out_shape=jax.ShapeDtypeStruct((1, S, D), jnp.float32),
        compiler_params=pltpu.CompilerParams(vmem_limit_bytes=VMEM_LIMIT_BYTES),
        name="final_norm",
    )(x1, peer_a, peer_b, mod3, g_final)


def kernel(x, c, w_ada, b_ada, g_norm1, w_in, w_pool, pool_scale, sgu_ln_g, sgu_ln_b, w_spatial, b_spatial,
           w_out, g_norm2, w_query, sub_keys, expert_down, expert_up, g_final):
    B, S, D = x.shape
    depth = w_ada.shape[0]
    bf16 = jnp.bfloat16
    band = jnp.asarray(_pool_band(), bf16)
    c_pad = jnp.pad(c, ((0, 8 - B), (0, 0)))
    for l in range(depth):
        mod = _ada_mod(c_pad, w_ada[l].astype(bf16), b_ada[l][None, :])[:B]
        mod3 = mod.reshape(B, N_MOD, D)
        mix_w = (g_norm1[l][None, :], w_in[l].astype(bf16), band, w_pool[l].astype(bf16),
                 pool_scale[l][None, :], sgu_ln_g[l][None, :], sgu_ln_b[l][None, :], w_spatial[l],
                 b_spatial[l].T, w_out[l].astype(bf16), g_norm2[l][None, :])
        keys = sub_keys[l].reshape(2 * PEER_HEADS, N_KEYS, HALF_KEY).astype(bf16)
        wq = w_query[l].astype(bf16)
        table2 = jnp.concatenate([expert_down[l], expert_up[l]], axis=1)
        n_sc = S * SC_SHARE_NUM // SC_SHARE_DEN
        outs = []
        for b in range(B):
            x1, h2 = _mixer(x, b, 1, mod3, *mix_w)
            h2f = h2.reshape(S, D)
            idx, gate = _query_topk(h2f, wq, keys)
            peer_tc = _peer_tc(h2f, idx, gate, table2, n_sc, S - n_sc)
            pre = _peer_down(h2f, idx, expert_down[l], n_sc)
            wgt = _gate_act(pre, gate)
            peer_sc = _peer_up(wgt, idx, expert_up[l], n_sc)
            outs.append(_residual(x1, peer_sc, peer_tc, b, mod3, g_final[None, :], normalize=(l + 1 == depth)))
        x = jnp.concatenate(outs, axis=0)
    return x
```

```python
import functools
import math

import jax
import jax.numpy as jnp
import numpy as np
from jax import lax
from jax.experimental import pallas as pl
from jax.experimental.pallas import tpu as pltpu
from jax.experimental.pallas import tpu_sc as plsc

D_MODEL = 1024
POOL_WIDTH = 512
SGU_WIDTH = 512
POOL_WINDOWS = (2, 4, 8, 16)
GROUP_DIM = 128
CHUNK = 128
SGU_HEADS = 4
IN_PROJ_WIDTH = POOL_WIDTH + 2 * SGU_WIDTH
PEER_HEADS = 8
N_KEYS = 128
HALF_KEY = 128
TOPK = 16
N_MOD = 6
EPS = 1e-6

MIX_BLOCK = 512
TOPK_BLOCK = 256
EW_BLOCK = 1024
VMEM_LIMIT_BYTES = 48 * 1024 * 1024

SC_LANES = 16
SC_TOKENS = 32
SC_RING = 4
N_SEL = PEER_HEADS * TOPK
TC_TOKENS = 8
SC_SHARE_NUM, SC_SHARE_DEN = 3, 4

N_CAND = 16 + 7 * 8 + 8


def _pool_band():
    t = np.arange(CHUNK)[:, None]
    j = np.arange(2 * CHUNK)[None, :]
    bands = []
    for w in POOL_WINDOWS:
        m = (j > CHUNK + t - w) & (j <= CHUNK + t)
        bands.append(np.where(m, 1.0 / w, 0.0))
    return np.stack(bands).astype(np.float32)


def _bf16_dot(a, b):
    return jnp.dot(a.astype(jnp.bfloat16), b.astype(jnp.bfloat16), preferred_element_type=jnp.float32)


def _ada_kernel(c_ref, w_ref, b_ref, o_ref):
    c = c_ref[...]
    c_act = c * jax.nn.sigmoid(c)
    o_ref[...] = _bf16_dot(c_act, w_ref[...]) + b_ref[...]


def _ada_mod(c_pad, w_ada, b_ada):
    rows = c_pad.shape[0]
    return pl.pallas_call(
        _ada_kernel,
        grid=(N_MOD,),
        in_specs=[
            pl.BlockSpec((rows, D_MODEL), lambda i: (0, 0)),
            pl.BlockSpec((D_MODEL, D_MODEL), lambda i: (0, i)),
            pl.BlockSpec((1, D_MODEL), lambda i: (0, i)),
        ],
        out_specs=pl.BlockSpec((rows, D_MODEL), lambda i: (0, i)),
        out_shape=jax.ShapeDtypeStruct((rows, N_MOD * D_MODEL), jnp.float32),
        name="ada_mod",
    )(c_pad, w_ada, b_ada)


def _rms(x, g):
    return x * lax.rsqrt(jnp.mean(x * x, axis=-1, keepdims=True) + EPS) * g


def _mix_kernel(x_ref, mod_ref, g1_ref, win_ref, band_ref, wpool_ref, pscale_ref, lng_ref, lnb_ref,
                wsp_ref, bsp_ref, wout_ref, g2_ref, x1_ref, h2_ref, aext_ref, mixed_ref):
    j = pl.program_id(1)
    x = x_ref[0]
    shift1, scale1, gate1 = mod_ref[0, 0:1, :], mod_ref[0, 1:2, :], mod_ref[0, 2:3, :]
    shift2, scale2 = mod_ref[0, 3:4, :], mod_ref[0, 4:5, :]

    h = _rms(x, g1_ref[...]) * (1.0 + scale1) + shift1
    proj = _bf16_dot(h, win_ref[...])
    a = proj[:, :POOL_WIDTH]
    u = proj[:, POOL_WIDTH:POOL_WIDTH + SGU_WIDTH]
    v = proj[:, POOL_WIDTH + SGU_WIDTH:]

    @pl.when(j == 0)
    def _():
        aext_ref[0:CHUNK, :] = jnp.zeros((CHUNK, POOL_WIDTH), jnp.float32)

    aext_ref[CHUNK:, :] = a

    mu = jnp.mean(v, axis=-1, keepdims=True)
    vc = v - mu
    vn = vc * lax.rsqrt(jnp.mean(vc * vc, axis=-1, keepdims=True) + EPS) * lng_ref[...] + lnb_ref[...]

    row = lax.broadcasted_iota(jnp.int32, (CHUNK, CHUNK), 0)
    col = lax.broadcasted_iota(jnp.int32, (CHUNK, CHUNK), 1)
    tril = col <= row
    t_col = lax.broadcasted_iota(jnp.int32, (CHUNK, 1), 0)

    for c in range(MIX_BLOCK // CHUNK):
        r0 = c * CHUNK
        pos = (j * MIX_BLOCK + r0 + 1 + t_col).astype(jnp.float32)
        for g, w in enumerate(POOL_WINDOWS):
            c0 = g * GROUP_DIM
            seg = aext_ref[r0:r0 + 2 * CHUNK, c0:c0 + GROUP_DIM]
            hi = seg.astype(jnp.bfloat16)
            lo = (seg - hi.astype(jnp.float32)).astype(jnp.bfloat16)
            band = band_ref[g]
            win = (jnp.dot(band, hi, preferred_element_type=jnp.float32)
                   + jnp.dot(band, lo, preferred_element_type=jnp.float32))
            mean = win * (float(w) / jnp.minimum(pos, float(w)))
            pooled = mean - seg[CHUNK:, :]
            y = _bf16_dot(pooled, wpool_ref[g]) * pscale_ref[:, c0:c0 + GROUP_DIM]
            mixed_ref[r0:r0 + CHUNK, c0:c0 + GROUP_DIM] = y.astype(jnp.bfloat16)
        for hh in range(SGU_HEADS):
            c0 = hh * GROUP_DIM
            wm = jnp.where(tril, wsp_ref[hh], 0.0)
            m = _bf16_dot(wm, vn[r0:r0 + CHUNK, c0:c0 + GROUP_DIM]) + bsp_ref[:, hh:hh + 1]
            s = u[r0:r0 + CHUNK, c0:c0 + GROUP_DIM] * m
            mixed_ref[r0:r0 + CHUNK, POOL_WIDTH + c0:POOL_WIDTH + c0 + GROUP_DIM] = s.astype(jnp.bfloat16)

    aext_ref[0:CHUNK, :] = aext_ref[MIX_BLOCK:MIX_BLOCK + CHUNK, :]

    x1 = x + gate1 * jnp.dot(mixed_ref[...], wout_ref[...], preferred_element_type=jnp.float32)
    x1_ref[0] = x1
    h2_ref[0] = _rms(x1, g2_ref[...]) * (1.0 + scale2) + shift2


def _mixer(x, b0, n_seq, mod3, g1, w_in, band, w_pool, pool_scale, ln_g, ln_b, w_sp, b_sp_t, w_out, g2):
    _, S, D = x.shape
    const2 = lambda b, j: (0, 0)
    const3 = lambda b, j: (0, 0, 0)
    tok = pl.BlockSpec((1, MIX_BLOCK, D), lambda b, j: (b, j, 0))
    return pl.pallas_call(
        _mix_kernel,
        grid=(n_seq, S // MIX_BLOCK),
        in_specs=[
            pl.BlockSpec((1, MIX_BLOCK, D), lambda b, j: (b0 + b, j, 0)),
            pl.BlockSpec((1, N_MOD, D), lambda b, j: (b0 + b, 0, 0)),
            pl.BlockSpec((1, D), const2),
            pl.BlockSpec((D, IN_PROJ_WIDTH), const2),
            pl.BlockSpec((len(POOL_WINDOWS), CHUNK, 2 * CHUNK), const3),
            pl.BlockSpec((len(POOL_WINDOWS), GROUP_DIM, GROUP_DIM), const3),
            pl.BlockSpec((1, POOL_WIDTH), const2),
            pl.BlockSpec((1, SGU_WIDTH), const2),
            pl.BlockSpec((1, SGU_WIDTH), const2),
            pl.BlockSpec((SGU_HEADS, CHUNK, CHUNK), const3),
            pl.BlockSpec((CHUNK, SGU_HEADS), const2),
            pl.BlockSpec((D, D), const2),
            pl.BlockSpec((1, D), const2),
        ],
        out_specs=[tok, tok],
        out_shape=[jax.ShapeDtypeStruct((n_seq, S, D), jnp.float32)] * 2,
        scratch_shapes=[
            pltpu.VMEM((MIX_BLOCK + CHUNK, POOL_WIDTH), jnp.float32),
            pltpu.VMEM((MIX_BLOCK, D), jnp.bfloat16),
        ],
        compiler_params=pltpu.CompilerParams(
            dimension_semantics=("arbitrary", "arbitrary"), vmem_limit_bytes=VMEM_LIMIT_BYTES),
        name="mixer",
    )(x, mod3, g1, w_in, band, w_pool, pool_scale, ln_g, ln_b, w_sp, b_sp_t, w_out, g2)


def _extract_top(s, ids, payload, n_out):
    vals, pays = [], []
    for _ in range(n_out):
        m = jnp.max(s, axis=0, keepdims=True)
        pick = jnp.min(jnp.where(s == m, ids, jnp.float32(1e9)), axis=0, keepdims=True)
        sel = ids == pick
        vals.append(m)
        pays.append(pick if payload is None
                    else jnp.max(jnp.where(sel, payload, -1.0), axis=0, keepdims=True))
        s = jnp.where(sel, -jnp.inf, s)
    return jnp.concatenate(vals, axis=0), jnp.concatenate(pays, axis=0)


def _topk_kernel(h2_ref, wq_ref, keys_ref, idx_ref, gate_ref):
    T = TOPK_BLOCK
    q = _bf16_dot(h2_ref[...], wq_ref[...])
    key_ids = lax.broadcasted_iota(jnp.int32, (N_KEYS, T), 0).astype(jnp.float32)
    row = lax.broadcasted_iota(jnp.int32, (N_CAND, T), 0)
    flat = jnp.where(row < 16, row,
                     jnp.where(row < 72, jnp.right_shift(row - 8, 3) * 16 + jnp.bitwise_and(row, 7),
                               (row - 64) * 16)).astype(jnp.float32)
    idx_rows, gate_rows = [], []
    for hd in range(PEER_HEADS):
        tops = []
        for p in range(2):
            hp = hd * 2 + p
            qs = q[:, hp * HALF_KEY:(hp + 1) * HALF_KEY].astype(jnp.bfloat16)
            st = lax.dot_general(keys_ref[hp], qs, (((1,), (1,)), ((), ())),
                                 preferred_element_type=jnp.float32)
            tops.append(_extract_top(st, key_ids, None, TOPK))
        (s0, i0), (s1, i1) = tops
        i0 = i0 * float(N_KEYS)
        cand = jnp.concatenate(
            [s0[0:1] + s1] + [s0[a:a + 1] + s1[0:8] for a in range(1, 8)] + [s0[8:16] + s1[0:1]], axis=0)
        cidx = jnp.concatenate(
            [i0[0:1] + i1] + [i0[a:a + 1] + i1[0:8] for a in range(1, 8)] + [i0[8:16] + i1[0:1]],
            axis=0)
        best, sel_idx = _extract_top(cand, flat, cidx, TOPK)
        e = jnp.exp(best - jnp.max(best, axis=0, keepdims=True))
        gate_rows.append(e / jnp.sum(e, axis=0, keepdims=True))
        idx_rows.append(sel_idx)
    idx_ref[...] = jnp.concatenate(idx_rows, axis=0).T.astype(jnp.int32)
    gate_ref[...] = jnp.concatenate(gate_rows, axis=0).T


def _query_topk(h2, w_query, keys):
    N, D = h2.shape
    tok = pl.BlockSpec((TOPK_BLOCK, N_SEL), lambda i: (i, 0))
    return pl.pallas_call(
        _topk_kernel,
        grid=(N // TOPK_BLOCK,),
        in_specs=[
            pl.BlockSpec((TOPK_BLOCK, D), lambda i: (i, 0)),
            pl.BlockSpec((D, 2 * PEER_HEADS * HALF_KEY), lambda i: (0, 0)),
            pl.BlockSpec((2 * PEER_HEADS, N_KEYS, HALF_KEY), lambda i: (0, 0, 0)),
        ],
        out_specs=[tok, tok],
        out_shape=[jax.ShapeDtypeStruct((N, N_SEL), jnp.int32), jax.ShapeDtypeStruct((N, N_SEL), jnp.float32)],
        compiler_params=pltpu.CompilerParams(
            dimension_semantics=("arbitrary",), vmem_limit_bytes=VMEM_LIMIT_BYTES),
        name="query_topk",
    )(h2, w_query, keys)


def _sc_mesh_and_workers():
    info = plsc.get_sparse_core_info()
    assert info.num_lanes == SC_LANES
    mesh = plsc.VectorSubcoreMesh(core_axis_name="core", subcore_axis_name="subcore")
    return mesh, info.num_cores, info.num_cores * info.num_subcores


def _gather_ring(tab_hbm, idx_v, rows_v, sems, consume):
    look = SC_RING - 1

    def copy(t, hd):
        slot = hd % SC_RING
        return pltpu.make_async_copy(
            tab_hbm.at[idx_v.at[t, pl.ds(hd * TOPK, TOPK)]], rows_v.at[slot], sems.at[slot])

    for hd in range(look):
        copy(0, hd).start()

    @pl.loop(0, SC_TOKENS)
    def _(t):
        for hd in range(PEER_HEADS):
            nxt = hd + look
            if nxt < PEER_HEADS:
                copy(t, nxt).start()
            else:
                @pl.when(t + 1 < SC_TOKENS)
                def _():
                    copy(t + 1, nxt - PEER_HEADS).start()
            copy(t, hd).wait()
            consume(t, hd, rows_v.at[hd % SC_RING])


def _pack_bf16_pairs(table):
    E, D = table.shape
    t = table.astype(jnp.bfloat16).reshape(E, D // (2 * SC_LANES), 2, SC_LANES)
    bits = lax.bitcast_convert_type(t, jnp.uint16).astype(jnp.uint32)
    words = bits[:, :, 0, :] | (bits[:, :, 1, :] << 16)
    return lax.bitcast_convert_type(words, jnp.int32).reshape(E, D // 2)


def _unpack_pair(words):
    lo = lax.bitcast_convert_type(lax.shift_left(words, jnp.int32(16)), jnp.float32)
    hi = lax.bitcast_convert_type(lax.bitwise_and(words, jnp.int32(-65536)), jnp.float32)
    return lo, hi


def _peer_down(h2, idx, packed, N):
    D = h2.shape[1]
    mesh, n_cores, n_workers = _sc_mesh_and_workers()
    per_worker = N // n_workers
    n_slices = D // (2 * SC_LANES)

    def body(h_hbm, idx_hbm, tab_hbm, out_hbm, idx_v, h_v, rows_v, out_v, tr_v, sems):
        wid = lax.axis_index("subcore") * n_cores + lax.axis_index("core")
        lane = lax.iota(jnp.int32, SC_LANES)

        def consume(t, hd, rows):
            def dot_step(s, accs):
                h_lo = h_v[t, pl.ds(s * 2 * SC_LANES, SC_LANES)]
                h_hi = h_v[t, pl.ds(s * 2 * SC_LANES + SC_LANES, SC_LANES)]
                out = []
                for k, acc in enumerate(accs):
                    lo, hi = _unpack_pair(rows[k, pl.ds(s * SC_LANES, SC_LANES)])
                    out.append(acc + (lo * h_lo + hi * h_hi))
                return tuple(out)

            accs = lax.fori_loop(0, n_slices, dot_step,
                                 tuple(jnp.zeros((SC_LANES,), jnp.float32) for _ in range(TOPK)))
            for k in range(TOPK):
                tr_v[k, :] = accs[k]
            tot = jnp.zeros((SC_LANES,), jnp.float32)
            for l in range(SC_LANES):
                tot = tot + plsc.load_gather(tr_v, [lane, jnp.full((SC_LANES,), l, jnp.int32)])
            out_v[t, pl.ds(hd * TOPK, TOPK)] = tot

        @pl.loop(0, per_worker // SC_TOKENS)
        def _(blk):
            base = wid * per_worker + blk * SC_TOKENS
            pltpu.sync_copy(idx_hbm.at[pl.ds(base, SC_TOKENS)], idx_v)
            pltpu.sync_copy(h_hbm.at[pl.ds(base, SC_TOKENS)], h_v)
            _gather_ring(tab_hbm, idx_v, rows_v, sems, consume)
            pltpu.sync_copy(out_v, out_hbm.at[pl.ds(base, SC_TOKENS)])

    return pl.kernel(
        body,
        out_type=jax.ShapeDtypeStruct((N, N_SEL), jnp.float32),
        mesh=mesh,
        scratch_types=[
            pltpu.VMEM((SC_TOKENS, N_SEL), jnp.int32),
            pltpu.VMEM((SC_TOKENS, D), jnp.float32),
            pltpu.VMEM((SC_RING, TOPK, D // 2), jnp.int32),
            pltpu.VMEM((SC_TOKENS, N_SEL), jnp.float32),
            pltpu.VMEM((TOPK, SC_LANES), jnp.float32),
            pltpu.SemaphoreType.DMA((SC_RING,)),
        ],
        compiler_params=pltpu.CompilerParams(needs_layout_passes=False),
        name="peer_down",
    )(h2, idx, packed)


def _peer_up(wgt, idx, packed, N):
    D = 2 * packed.shape[1]
    mesh, n_cores, n_workers = _sc_mesh_and_workers()
    per_worker = N // n_workers
    n_slices = D // (2 * SC_LANES)

    def body(w_hbm, idx_hbm, tab_hbm, out_hbm, idx_v, w_v, rows_v, out_v, sems):
        wid = lax.axis_index("subcore") * n_cores + lax.axis_index("core")

        def consume(t, hd, rows):
            t_vec = jnp.full((SC_LANES,), t, jnp.int32)
            ws = [plsc.load_gather(w_v, [t_vec, jnp.full((SC_LANES,), hd * TOPK + k, jnp.int32)])
                  for k in range(TOPK)]

            def tree_sum(terms):
                while len(terms) > 1:
                    terms = [terms[i] + terms[i + 1] for i in range(0, len(terms) - 1, 2)] + (
                        [terms[-1]] if len(terms) % 2 else [])
                return terms[0]

            @plsc.parallel_loop(0, n_slices)
            def _(s):
                sl_lo = pl.ds(s * 2 * SC_LANES, SC_LANES)
                sl_hi = pl.ds(s * 2 * SC_LANES + SC_LANES, SC_LANES)
                pairs = [_unpack_pair(rows[k, pl.ds(s * SC_LANES, SC_LANES)]) for k in range(TOPK)]
                lo_terms = [ws[k] * pairs[k][0] for k in range(TOPK)]
                hi_terms = [ws[k] * pairs[k][1] for k in range(TOPK)]
                if hd > 0:
                    lo_terms.append(out_v[t, sl_lo])
                    hi_terms.append(out_v[t, sl_hi])
                out_v[t, sl_lo] = tree_sum(lo_terms)
                out_v[t, sl_hi] = tree_sum(hi_terms)

        @pl.loop(0, per_worker // SC_TOKENS)
        def _(blk):
            base = wid * per_worker + blk * SC_TOKENS
            pltpu.sync_copy(idx_hbm.at[pl.ds(base, SC_TOKENS)], idx_v)
            pltpu.sync_copy(w_hbm.at[pl.ds(base, SC_TOKENS)], w_v)
            _gather_ring(tab_hbm, idx_v, rows_v, sems, consume)
            pltpu.sync_copy(out_v, out_hbm.at[pl.ds(base, SC_TOKENS)])

    return pl.kernel(
        body,
        out_type=jax.ShapeDtypeStruct((N, D), jnp.float32),
        mesh=mesh,
        scratch_types=[
            pltpu.VMEM((SC_TOKENS, N_SEL), jnp.int32),
            pltpu.VMEM((SC_TOKENS, N_SEL), jnp.float32),
            pltpu.VMEM((SC_RING, TOPK, D // 2), jnp.int32),
            pltpu.VMEM((SC_TOKENS, D), jnp.float32),
            pltpu.SemaphoreType.DMA((SC_RING,)),
        ],
        compiler_params=pltpu.CompilerParams(needs_layout_passes=False),
        name="peer_up",
    )(wgt, idx, packed)


def _peer_tc_kernel(idx_cur, idx_nxt, h_ref, gate_ref, tab_hbm, o_ref, gbuf, sems):
    i = pl.program_id(0)
    n = pl.num_programs(0)
    groups = N_SEL // 8

    def issue(idx_ref, slot):
        def body(g, carry):
            for j in range(8):
                e = idx_ref[g * 8 + j]
                pltpu.make_async_copy(
                    tab_hbm.at[pl.ds(e, 1), :], gbuf.at[slot, g, pl.ds(j, 1), :], sems.at[slot]).start()
            return carry
        lax.fori_loop(0, TC_TOKENS * groups, body, 0)

    @pl.when(i == 0)
    def _():
        issue(idx_cur, 0)

    @pl.when(i + 1 < n)
    def _():
        issue(idx_nxt, (i + 1) % 2)

    slot = i % 2
    pltpu.make_async_copy(gbuf.at[slot], gbuf.at[slot], sems.at[slot]).wait()

    def rows(t, c0):
        return gbuf[slot, t * groups:(t + 1) * groups, :, c0:c0 + D_MODEL].reshape(N_SEL, D_MODEL)

    lane = lax.broadcasted_iota(jnp.int32, (N_SEL, 128), 1)
    pre_all = jnp.zeros((N_SEL, 128), jnp.float32)
    for t in range(TC_TOKENS):
        prod = rows(t, 0) * h_ref[t:t + 1, :]
        part = prod[:, 0:128]
        for c in range(1, D_MODEL // 128):
            part = part + prod[:, c * 128:(c + 1) * 128]
        pre = jnp.sum(part, axis=1, keepdims=True)
        pre_all = jnp.where(lane == t, pre, pre_all)
    act = 0.5 * pre_all * (1.0 + lax.erf(pre_all * (1.0 / math.sqrt(2.0))))
    gate_t = jnp.concatenate([gate_ref[...]] * (128 // TC_TOKENS), axis=0).T
    wgt = gate_t * act
    for t in range(TC_TOKENS):
        w_col = jnp.sum(jnp.where(lane == t, wgt, 0.0), axis=1, keepdims=True)
        up = rows(t, D_MODEL) * w_col
        o_ref[t:t + 1, :] = jnp.sum(up, axis=0, keepdims=True)


def _peer_tc(h2, idx, gate, table2, row0, n_tokens):
    D = h2.shape[1]
    steps = n_tokens // TC_TOKENS
    blk0 = row0 // TC_TOKENS
    idx_flat = idx.reshape(-1)
    return pl.pallas_call(
        _peer_tc_kernel,
        grid=(steps,),
        in_specs=[
            pl.BlockSpec((TC_TOKENS * N_SEL,), lambda i: (blk0 + i,), memory_space=pltpu.SMEM),
            pl.BlockSpec((TC_TOKENS * N_SEL,), lambda i: (blk0 + jnp.minimum(i + 1, steps - 1),),
                         memory_space=pltpu.SMEM),
            pl.BlockSpec((TC_TOKENS, D), lambda i: (blk0 + i, 0)),
            pl.BlockSpec((TC_TOKENS, N_SEL), lambda i: (blk0 + i, 0)),
            pl.BlockSpec(memory_space=pl.ANY),
        ],
        out_specs=pl.BlockSpec((TC_TOKENS, D), lambda i: (i, 0)),
        out_shape=jax.ShapeDtypeStruct((n_tokens, D), jnp.float32),
        scratch_shapes=[
            pltpu.VMEM((2, TC_TOKENS * N_SEL // 8, 8, 2 * D), jnp.float32),
            pltpu.SemaphoreType.DMA((2,)),
        ],
        compiler_params=pltpu.CompilerParams(
            dimension_semantics=("arbitrary",), vmem_limit_bytes=VMEM_LIMIT_BYTES),
        name="peer_tc",
    )(idx_flat, idx_flat, h2, gate, table2)


def _gate_act_kernel(pre_ref, gate_ref, o_ref):
    p = pre_ref[...]
    act = 0.5 * p * (1.0 + lax.erf(p * (1.0 / math.sqrt(2.0))))
    o_ref[...] = gate_ref[...] * act


def _gate_act(pre, gate):
    N = pre.shape[0]
    spec = pl.BlockSpec((EW_BLOCK, N_SEL), lambda i: (i, 0))
    return pl.pallas_call(
        _gate_act_kernel, grid=(N // EW_BLOCK,), in_specs=[spec, spec], out_specs=spec,
        out_shape=jax.ShapeDtypeStruct((N, N_SEL), jnp.float32), name="gate_act",
    )(pre, gate)


def _final_kernel(x1_ref, peer_a_ref, peer_b_ref, mod_ref, g_ref, o_ref, *, normalize, blocks_a):
    gate2 = mod_ref[0, 5:6, :]
    peer = jnp.where(pl.program_id(0) < blocks_a, peer_a_ref[...], peer_b_ref[...])
    y = x1_ref[0] + gate2 * peer
    o_ref[0] = _rms(y, g_ref[...]) if normalize else y


def _residual(x1, peer_a, peer_b, b, mod3, g_final, normalize):
    _, S, D = x1.shape
    blocks_a = peer_a.shape[0] // EW_BLOCK
    blocks_b = peer_b.shape[0] // EW_BLOCK
    tok = pl.BlockSpec((1, EW_BLOCK, D), lambda j: (0, j, 0))
    return pl.pallas_call(
        functools.partial(_final_kernel, normalize=normalize, blocks_a=blocks_a),
        grid=(S // EW_BLOCK,),
        in_specs=[tok,
                  pl.BlockSpec((EW_BLOCK, D), lambda j: (jnp.minimum(j, blocks_a - 1), 0)),
                  pl.BlockSpec((EW_BLOCK, D), lambda j: (jnp.clip(j - blocks_a, 0, blocks_b - 1), 0)),
                  pl.BlockSpec((1, N_MOD, D), lambda j: (b, 0, 0)),
                  pl.BlockSpec((1, D), lambda j: (0, 0))],
        out_specs=tok,
        out_shape=jax.ShapeDtypeStruct((1, S, D), jnp.float32),
        compiler_params=pltpu.CompilerParams(vmem_limit_bytes=VMEM_LIMIT_BYTES),
        name="final_norm",
    )(x1, peer_a, peer_b, mod3, g_final)


def kernel(x, c, w_ada, b_ada, g_norm1, w_in, w_pool, pool_scale, sgu_ln_g, sgu_ln_b, w_spatial, b_spatial,
           w_out, g_norm2, w_query, sub_keys, expert_down, expert_up, g_final):
    B, S, D = x.shape
    depth = w_ada.shape[0]
    bf16 = jnp.bfloat16
    band = jnp.asarray(_pool_band(), bf16)
    c_pad = jnp.pad(c, ((0, 8 - B), (0, 0)))
    for l in range(depth):
        mod = _ada_mod(c_pad, w_ada[l].astype(bf16), b_ada[l][None, :])[:B]
        mod3 = mod.reshape(B, N_MOD, D)
        mix_w = (g_norm1[l][None, :], w_in[l].astype(bf16), band, w_pool[l].astype(bf16),
                 pool_scale[l][None, :], sgu_ln_g[l][None, :], sgu_ln_b[l][None, :], w_spatial[l],
                 b_spatial[l].T, w_out[l].astype(bf16), g_norm2[l][None, :])
        keys = sub_keys[l].reshape(2 * PEER_HEADS, N_KEYS, HALF_KEY).astype(bf16)
        wq = w_query[l].astype(bf16)
        table2 = jnp.concatenate([expert_down[l], expert_up[l]], axis=1)
        down_packed = _pack_bf16_pairs(expert_down[l])
        up_packed = _pack_bf16_pairs(expert_up[l])
        n_sc = S * SC_SHARE_NUM // SC_SHARE_DEN
        outs = []
        mod3_b = mod3
        for b in range(B):
            x1, h2 = _mixer(x, b, 1, mod3_b, *mix_w)
            h2f = h2.reshape(S, D)
            idx, gate = _query_topk(h2f, wq, keys)
            peer_tc = _peer_tc(h2f, idx, gate, table2, n_sc, S - n_sc)
            pre = _peer_down(h2f, idx, down_packed, n_sc)
            pre, peer_tc = lax.optimization_barrier((pre, peer_tc))
            wgt = _gate_act(pre, gate)
            peer_sc = _peer_up(wgt, idx, up_packed, n_sc)
            mod3_b, wgt = lax.optimization_barrier((mod3, wgt))
            outs.append(_residual(x1, peer_sc, peer_tc, b, mod3, g_final[None, :], normalize=(l + 1 == depth)))
        x = jnp.concatenate(outs, axis=0)
    return x
```

```python
import functools
import math

import jax
import jax.numpy as jnp
import numpy as np
from jax import lax
from jax.experimental import pallas as pl
from jax.experimental.pallas import tpu as pltpu
from jax.experimental.pallas import tpu_sc as plsc

D_MODEL = 1024
POOL_WIDTH = 512
SGU_WIDTH = 512
POOL_WINDOWS = (2, 4, 8, 16)
GROUP_DIM = 128
CHUNK = 128
SGU_HEADS = 4
IN_PROJ_WIDTH = POOL_WIDTH + 2 * SGU_WIDTH
PEER_HEADS = 8
N_KEYS = 128
HALF_KEY = 128
TOPK = 16
N_MOD = 6
EPS = 1e-6

MIX_BLOCK = 512
TOPK_BLOCK = 256
EW_BLOCK = 1024
VMEM_LIMIT_BYTES = 48 * 1024 * 1024

SC_LANES = 16
SC_TOKENS = 32
SC_RING = 4
N_SEL = PEER_HEADS * TOPK
TC_TOKENS = 8
SC_SHARE_NUM, SC_SHARE_DEN = 3, 4

N_CAND = 16 + 7 * 8 + 8


def _pool_band():
    t = np.arange(CHUNK)[:, None]
    j = np.arange(2 * CHUNK)[None, :]
    bands = []
    for w in POOL_WINDOWS:
        m = (j > CHUNK + t - w) & (j <= CHUNK + t)
        bands.append(np.where(m, 1.0 / w, 0.0))
    return np.stack(bands).astype(np.float32)


def _bf16_dot(a, b):
    return jnp.dot(a.astype(jnp.bfloat16), b.astype(jnp.bfloat16), preferred_element_type=jnp.float32)


def _ada_kernel(c_ref, w_ref, b_ref, o_ref):
    c = c_ref[...]
    c_act = c * jax.nn.sigmoid(c)
    o_ref[...] = _bf16_dot(c_act, w_ref[...]) + b_ref[...]


def _ada_mod(c_pad, w_ada, b_ada):
    rows = c_pad.shape[0]
    return pl.pallas_call(
        _ada_kernel,
        grid=(N_MOD,),
        in_specs=[
            pl.BlockSpec((rows, D_MODEL), lambda i: (0, 0)),
            pl.BlockSpec((D_MODEL, D_MODEL), lambda i: (0, i)),
            pl.BlockSpec((1, D_MODEL), lambda i: (0, i)),
        ],
        out_specs=pl.BlockSpec((rows, D_MODEL), lambda i: (0, i)),
        out_shape=jax.ShapeDtypeStruct((rows, N_MOD * D_MODEL), jnp.float32),
        name="ada_mod",
    )(c_pad, w_ada, b_ada)


def _rms(x, g):
    return x * lax.rsqrt(jnp.mean(x * x, axis=-1, keepdims=True) + EPS) * g


def _mix_kernel(x_ref, mod_ref, g1_ref, win_ref, band_ref, wpool_ref, pscale_ref, lng_ref, lnb_ref,
                wsp_ref, bsp_ref, wout_ref, g2_ref, x1_ref, h2_ref, aext_ref, mixed_ref):
    j = pl.program_id(1)
    x = x_ref[0]
    shift1, scale1, gate1 = mod_ref[0, 0:1, :], mod_ref[0, 1:2, :], mod_ref[0, 2:3, :]
    shift2, scale2 = mod_ref[0, 3:4, :], mod_ref[0, 4:5, :]

    h = _rms(x, g1_ref[...]) * (1.0 + scale1) + shift1
    proj = _bf16_dot(h, win_ref[...])
    a = proj[:, :POOL_WIDTH]
    u = proj[:, POOL_WIDTH:POOL_WIDTH + SGU_WIDTH]
    v = proj[:, POOL_WIDTH + SGU_WIDTH:]

    @pl.when(j == 0)
    def _():
        aext_ref[0:CHUNK, :] = jnp.zeros((CHUNK, POOL_WIDTH), jnp.float32)

    aext_ref[CHUNK:, :] = a

    mu = jnp.mean(v, axis=-1, keepdims=True)
    vc = v - mu
    vn = vc * lax.rsqrt(jnp.mean(vc * vc, axis=-1, keepdims=True) + EPS) * lng_ref[...] + lnb_ref[...]

    row = lax.broadcasted_iota(jnp.int32, (CHUNK, CHUNK), 0)
    col = lax.broadcasted_iota(jnp.int32, (CHUNK, CHUNK), 1)
    tril = col <= row
    t_col = lax.broadcasted_iota(jnp.int32, (CHUNK, 1), 0)

    for c in range(MIX_BLOCK // CHUNK):
        r0 = c * CHUNK
        pos = (j * MIX_BLOCK + r0 + 1 + t_col).astype(jnp.float32)
        for g, w in enumerate(POOL_WINDOWS):
            c0 = g * GROUP_DIM
            seg = aext_ref[r0:r0 + 2 * CHUNK, c0:c0 + GROUP_DIM]
            hi = seg.astype(jnp.bfloat16)
            lo = (seg - hi.astype(jnp.float32)).astype(jnp.bfloat16)
            band = band_ref[g]
            win = (jnp.dot(band, hi, preferred_element_type=jnp.float32)
                   + jnp.dot(band, lo, preferred_element_type=jnp.float32))
            mean = win * (float(w) / jnp.minimum(pos, float(w)))
            pooled = mean - seg[CHUNK:, :]
            y = _bf16_dot(pooled, wpool_ref[g]) * pscale_ref[:, c0:c0 + GROUP_DIM]
            mixed_ref[r0:r0 + CHUNK, c0:c0 + GROUP_DIM] = y.astype(jnp.bfloat16)
        for hh in range(SGU_HEADS):
            c0 = hh * GROUP_DIM
            wm = jnp.where(tril, wsp_ref[hh], 0.0)
            m = _bf16_dot(wm, vn[r0:r0 + CHUNK, c0:c0 + GROUP_DIM]) + bsp_ref[:, hh:hh + 1]
            s = u[r0:r0 + CHUNK, c0:c0 + GROUP_DIM] * m
            mixed_ref[r0:r0 + CHUNK, POOL_WIDTH + c0:POOL_WIDTH + c0 + GROUP_DIM] = s.astype(jnp.bfloat16)

    aext_ref[0:CHUNK, :] = aext_ref[MIX_BLOCK:MIX_BLOCK + CHUNK, :]

    x1 = x + gate1 * jnp.dot(mixed_ref[...], wout_ref[...], preferred_element_type=jnp.float32)
    x1_ref[0] = x1
    h2_ref[0] = _rms(x1, g2_ref[...]) * (1.0 + scale2) + shift2


def _mixer(x, b0, n_seq, mod3, g1, w_in, band, w_pool, pool_scale, ln_g, ln_b, w_sp, b_sp_t, w_out, g2):
    _, S, D = x.shape
    const2 = lambda b, j: (0, 0)
    const3 = lambda b, j: (0, 0, 0)
    tok = pl.BlockSpec((1, MIX_BLOCK, D), lambda b, j: (b, j, 0))
    return pl.pallas_call(
        _mix_kernel,
        grid=(n_seq, S // MIX_BLOCK),
        in_specs=[
            pl.BlockSpec((1, MIX_BLOCK, D), lambda b, j: (b0 + b, j, 0)),
            pl.BlockSpec((1, N_MOD, D), lambda b, j: (b0 + b, 0, 0)),
            pl.BlockSpec((1, D), const2),
            pl.BlockSpec((D, IN_PROJ_WIDTH), const2),
            pl.BlockSpec((len(POOL_WINDOWS), CHUNK, 2 * CHUNK), const3),
            pl.BlockSpec((len(POOL_WINDOWS), GROUP_DIM, GROUP_DIM), const3),
            pl.BlockSpec((1, POOL_WIDTH), const2),
            pl.BlockSpec((1, SGU_WIDTH), const2),
            pl.BlockSpec((1, SGU_WIDTH), const2),
            pl.BlockSpec((SGU_HEADS, CHUNK, CHUNK), const3),
            pl.BlockSpec((CHUNK, SGU_HEADS), const2),
            pl.BlockSpec((D, D), const2),
            pl.BlockSpec((1, D), const2),
        ],
        out_specs=[tok, tok],
        out_shape=[jax.ShapeDtypeStruct((n_seq, S, D), jnp.float32)] * 2,
        scratch_shapes=[
            pltpu.VMEM((MIX_BLOCK + CHUNK, POOL_WIDTH), jnp.float32),
            pltpu.VMEM((MIX_BLOCK, D), jnp.bfloat16),
        ],
        compiler_params=pltpu.CompilerParams(
            dimension_semantics=("arbitrary", "arbitrary"), vmem_limit_bytes=VMEM_LIMIT_BYTES),
        name="mixer",
    )(x, mod3, g1, w_in, band, w_pool, pool_scale, ln_g, ln_b, w_sp, b_sp_t, w_out, g2)


def _extract_top(s, ids, payload, n_out):
    vals, pays = [], []
    for _ in range(n_out):
        m = jnp.max(s, axis=0, keepdims=True)
        pick = jnp.min(jnp.where(s == m, ids, jnp.float32(1e9)), axis=0, keepdims=True)
        sel = ids == pick
        vals.append(m)
        pays.append(pick if payload is None
                    else jnp.max(jnp.where(sel, payload, -1.0), axis=0, keepdims=True))
        s = jnp.where(sel, -jnp.inf, s)
    return jnp.concatenate(vals, axis=0), jnp.concatenate(pays, axis=0)


def _topk_kernel(h2_ref, wq_ref, keys_ref, idx_ref, gate_ref):
    T = TOPK_BLOCK
    q = _bf16_dot(h2_ref[...], wq_ref[...])
    key_ids = lax.broadcasted_iota(jnp.int32, (N_KEYS, T), 0).astype(jnp.float32)
    row = lax.broadcasted_iota(jnp.int32, (N_CAND, T), 0)
    flat = jnp.where(row < 16, row,
                     jnp.where(row < 72, jnp.right_shift(row - 8, 3) * 16 + jnp.bitwise_and(row, 7),
                               (row - 64) * 16)).astype(jnp.float32)
    idx_rows, gate_rows = [], []
    for hd in range(PEER_HEADS):
        tops = []
        for p in range(2):
            hp = hd * 2 + p
            qs = q[:, hp * HALF_KEY:(hp + 1) * HALF_KEY].astype(jnp.bfloat16)
            st = lax.dot_general(keys_ref[hp], qs, (((1,), (1,)), ((), ())),
                                 preferred_element_type=jnp.float32)
            tops.append(_extract_top(st, key_ids, None, TOPK))
        (s0, i0), (s1, i1) = tops
        i0 = i0 * float(N_KEYS)
        cand = jnp.concatenate(
            [s0[0:1] + s1] + [s0[a:a + 1] + s1[0:8] for a in range(1, 8)] + [s0[8:16] + s1[0:1]], axis=0)
        cidx = jnp.concatenate(
            [i0[0:1] + i1] + [i0[a:a + 1] + i1[0:8] for a in range(1, 8)] + [i0[8:16] + i1[0:1]],
            axis=0)
        best, sel_idx = _extract_top(cand, flat, cidx, TOPK)
        e = jnp.exp(best - jnp.max(best, axis=0, keepdims=True))
        gate_rows.append(e / jnp.sum(e, axis=0, keepdims=True))
        idx_rows.append(sel_idx)
    idx_ref[...] = jnp.concatenate(idx_rows, axis=0).T.astype(jnp.int32)
    gate_ref[...] = jnp.concatenate(gate_rows, axis=0).T


def _query_topk(h2, w_query, keys):
    N, D = h2.shape
    tok = pl.BlockSpec((TOPK_BLOCK, N_SEL), lambda i: (i, 0))
    return pl.pallas_call(
        _topk_kernel,
        grid=(N // TOPK_BLOCK,),
        in_specs=[
            pl.BlockSpec((TOPK_BLOCK, D), lambda i: (i, 0)),
            pl.BlockSpec((D, 2 * PEER_HEADS * HALF_KEY), lambda i: (0, 0)),
            pl.BlockSpec((2 * PEER_HEADS, N_KEYS, HALF_KEY), lambda i: (0, 0, 0)),
        ],
        out_specs=[tok, tok],
        out_shape=[jax.ShapeDtypeStruct((N, N_SEL), jnp.int32), jax.ShapeDtypeStruct((N, N_SEL), jnp.float32)],
        compiler_params=pltpu.CompilerParams(
            dimension_semantics=("arbitrary",), vmem_limit_bytes=VMEM_LIMIT_BYTES),
        name="query_topk",
    )(h2, w_query, keys)


def _sc_mesh_and_workers():
    info = plsc.get_sparse_core_info()
    assert info.num_lanes == SC_LANES
    mesh = plsc.VectorSubcoreMesh(core_axis_name="core", subcore_axis_name="subcore")
    return mesh, info.num_cores, info.num_cores * info.num_subcores


def _gather_ring(tab_hbm, idx_v, rows_v, sems, consume):
    look = SC_RING - 1

    def copy(t, hd):
        slot = hd % SC_RING
        return pltpu.make_async_copy(
            tab_hbm.at[idx_v.at[t, pl.ds(hd * TOPK, TOPK)]], rows_v.at[slot], sems.at[slot])

    for hd in range(look):
        copy(0, hd).start()

    @pl.loop(0, SC_TOKENS)
    def _(t):
        for hd in range(PEER_HEADS):
            nxt = hd + look
            if nxt < PEER_HEADS:
                copy(t, nxt).start()
            else:
                @pl.when(t + 1 < SC_TOKENS)
                def _():
                    copy(t + 1, nxt - PEER_HEADS).start()
            copy(t, hd).wait()
            consume(t, hd, rows_v.at[hd % SC_RING])


def _pack_bf16_pairs(table):
    E, D = table.shape
    t = table.astype(jnp.bfloat16).reshape(E, D // (2 * SC_LANES), 2, SC_LANES)
    bits = lax.bitcast_convert_type(t, jnp.uint16).astype(jnp.uint32)
    words = bits[:, :, 0, :] | (bits[:, :, 1, :] << 16)
    return lax.bitcast_convert_type(words, jnp.int32).reshape(E, D // 2)


def _unpack_pair(words):
    lo = lax.bitcast_convert_type(lax.shift_left(words, jnp.int32(16)), jnp.float32)
    hi = lax.bitcast_convert_type(lax.bitwise_and(words, jnp.int32(-65536)), jnp.float32)
    return lo, hi


def _peer_down(h2, idx, packed, N):
    D = h2.shape[1]
    mesh, n_cores, n_workers = _sc_mesh_and_workers()
    per_worker = N // n_workers
    n_slices = D // (2 * SC_LANES)

    def body(h_hbm, idx_hbm, tab_hbm, out_hbm, idx_v, h_v, rows_v, out_v, tr_v, sems):
        wid = lax.axis_index("subcore") * n_cores + lax.axis_index("core")
        lane = lax.iota(jnp.int32, SC_LANES)

        def consume(t, hd, rows):
            def dot_step(s, accs):
                h_lo = h_v[t, pl.ds(s * 2 * SC_LANES, SC_LANES)]
                h_hi = h_v[t, pl.ds(s * 2 * SC_LANES + SC_LANES, SC_LANES)]
                out = []
                for k, acc in enumerate(accs):
                    lo, hi = _unpack_pair(rows[k, pl.ds(s * SC_LANES, SC_LANES)])
                    out.append(acc + (lo * h_lo + hi * h_hi))
                return tuple(out)

            accs = lax.fori_loop(0, n_slices, dot_step,
                                 tuple(jnp.zeros((SC_LANES,), jnp.float32) for _ in range(TOPK)))
            for k in range(TOPK):
                tr_v[k, :] = accs[k]
            tot = jnp.zeros((SC_LANES,), jnp.float32)
            for l in range(SC_LANES):
                tot = tot + plsc.load_gather(tr_v, [lane, jnp.full((SC_LANES,), l, jnp.int32)])
            out_v[t, pl.ds(hd * TOPK, TOPK)] = tot

        @pl.loop(0, per_worker // SC_TOKENS)
        def _(blk):
            base = wid * per_worker + blk * SC_TOKENS
            pltpu.sync_copy(idx_hbm.at[pl.ds(base, SC_TOKENS)], idx_v)
            pltpu.sync_copy(h_hbm.at[pl.ds(base, SC_TOKENS)], h_v)
            _gather_ring(tab_hbm, idx_v, rows_v, sems, consume)
            pltpu.sync_copy(out_v, out_hbm.at[pl.ds(base, SC_TOKENS)])

    return pl.kernel(
        body,
        out_type=jax.ShapeDtypeStruct((N, N_SEL), jnp.float32),
        mesh=mesh,
        scratch_types=[
            pltpu.VMEM((SC_TOKENS, N_SEL), jnp.int32),
            pltpu.VMEM((SC_TOKENS, D), jnp.float32),
            pltpu.VMEM((SC_RING, TOPK, D // 2), jnp.int32),
            pltpu.VMEM((SC_TOKENS, N_SEL), jnp.float32),
            pltpu.VMEM((TOPK, SC_LANES), jnp.float32),
            pltpu.SemaphoreType.DMA((SC_RING,)),
        ],
        compiler_params=pltpu.CompilerParams(needs_layout_passes=False),
        name="peer_down",
    )(h2, idx, packed)


def _peer_up(wgt, idx, packed, N):
    D = 2 * packed.shape[1]
    mesh, n_cores, n_workers = _sc_mesh_and_workers()
    per_worker = N // n_workers
    n_slices = D // (2 * SC_LANES)

    def body(w_hbm, idx_hbm, tab_hbm, out_hbm, idx_v, w_v, rows_v, out_v, sems):
        wid = lax.axis_index("subcore") * n_cores + lax.axis_index("core")

        def consume(t, hd, rows):
            t_vec = jnp.full((SC_LANES,), t, jnp.int32)
            ws = [plsc.load_gather(w_v, [t_vec, jnp.full((SC_LANES,), hd * TOPK + k, jnp.int32)])
                  for k in range(TOPK)]

            def tree_sum(terms):
                while len(terms) > 1:
                    terms = [terms[i] + terms[i + 1] for i in range(0, len(terms) - 1, 2)] + (
                        [terms[-1]] if len(terms) % 2 else [])
                return terms[0]

            @plsc.parallel_loop(0, n_slices)
            def _(s):
                sl_lo = pl.ds(s * 2 * SC_LANES, SC_LANES)
                sl_hi = pl.ds(s * 2 * SC_LANES + SC_LANES, SC_LANES)
                pairs = [_unpack_pair(rows[k, pl.ds(s * SC_LANES, SC_LANES)]) for k in range(TOPK)]
                lo_terms = [ws[k] * pairs[k][0] for k in range(TOPK)]
                hi_terms = [ws[k] * pairs[k][1] for k in range(TOPK)]
                if hd > 0:
                    lo_terms.append(out_v[t, sl_lo])
                    hi_terms.append(out_v[t, sl_hi])
                out_v[t, sl_lo] = tree_sum(lo_terms)
                out_v[t, sl_hi] = tree_sum(hi_terms)

        @pl.loop(0, per_worker // SC_TOKENS)
        def _(blk):
            base = wid * per_worker + blk * SC_TOKENS
            pltpu.sync_copy(idx_hbm.at[pl.ds(base, SC_TOKENS)], idx_v)
            pltpu.sync_copy(w_hbm.at[pl.ds(base, SC_TOKENS)], w_v)
            _gather_ring(tab_hbm, idx_v, rows_v, sems, consume)
            pltpu.sync_copy(out_v, out_hbm.at[pl.ds(base, SC_TOKENS)])

    return pl.kernel(
        body,
        out_type=jax.ShapeDtypeStruct((N, D), jnp.float32),
        mesh=mesh,
        scratch_types=[
            pltpu.VMEM((SC_TOKENS, N_SEL), jnp.int32),
            pltpu.VMEM((SC_TOKENS, N_SEL), jnp.float32),
            pltpu.VMEM((SC_RING, TOPK, D // 2), jnp.int32),
            pltpu.VMEM((SC_TOKENS, D), jnp.float32),
            pltpu.SemaphoreType.DMA((SC_RING,)),
        ],
        compiler_params=pltpu.CompilerParams(needs_layout_passes=False),
        name="peer_up",
    )(wgt, idx, packed)


def _peer_tc_kernel(idx_cur, idx_nxt, h_ref, gate_ref, tab_hbm, o_ref, gbuf, sems):
    i = pl.program_id(0)
    n = pl.num_programs(0)
    groups = N_SEL // 8

    def issue(idx_ref, slot):
        def body(g, carry):
            for j in range(8):
                e = idx_ref[g * 8 + j]
                pltpu.make_async_copy(
                    tab_hbm.at[pl.ds(e, 1), :], gbuf.at[slot, g, pl.ds(j, 1), :], sems.at[slot]).start()
            return carry
        lax.fori_loop(0, TC_TOKENS * groups, body, 0)

    @pl.when(i == 0)
    def _():
        issue(idx_cur, 0)

    @pl.when(i + 1 < n)
    def _():
        issue(idx_nxt, (i + 1) % 2)

    slot = i % 2
    pltpu.make_async_copy(gbuf.at[slot], gbuf.at[slot], sems.at[slot]).wait()

    def rows(t, c0):
        return gbuf[slot, t * groups:(t + 1) * groups, :, c0:c0 + D_MODEL].reshape(N_SEL, D_MODEL)

    lane = lax.broadcasted_iota(jnp.int32, (N_SEL, 128), 1)
    pre_all = jnp.zeros((N_SEL, 128), jnp.float32)
    for t in range(TC_TOKENS):
        prod = rows(t, 0) * h_ref[t:t + 1, :]
        part = prod[:, 0:128]
        for c in range(1, D_MODEL // 128):
            part = part + prod[:, c * 128:(c + 1) * 128]
        pre = jnp.sum(part, axis=1, keepdims=True)
        pre_all = jnp.where(lane == t, pre, pre_all)
    act = 0.5 * pre_all * (1.0 + lax.erf(pre_all * (1.0 / math.sqrt(2.0))))
    gate_t = jnp.concatenate([gate_ref[...]] * (128 // TC_TOKENS), axis=0).T
    wgt = gate_t * act
    for t in range(TC_TOKENS):
        w_col = jnp.sum(jnp.where(lane == t, wgt, 0.0), axis=1, keepdims=True)
        up = rows(t, D_MODEL) * w_col
        o_ref[t:t + 1, :] = jnp.sum(up, axis=0, keepdims=True)


def _peer_tc(h2, idx, gate, table2, row0, n_tokens):
    D = h2.shape[1]
    steps = n_tokens // TC_TOKENS
    blk0 = row0 // TC_TOKENS
    idx_flat = idx.reshape(-1)
    return pl.pallas_call(
        _peer_tc_kernel,
        grid=(steps,),
        in_specs=[
            pl.BlockSpec((TC_TOKENS * N_SEL,), lambda i: (blk0 + i,), memory_space=pltpu.SMEM),
            pl.BlockSpec((TC_TOKENS * N_SEL,), lambda i: (blk0 + jnp.minimum(i + 1, steps - 1),),
                         memory_space=pltpu.SMEM),
            pl.BlockSpec((TC_TOKENS, D), lambda i: (blk0 + i, 0)),
            pl.BlockSpec((TC_TOKENS, N_SEL), lambda i: (blk0 + i, 0)),
            pl.BlockSpec(memory_space=pl.ANY),
        ],
        out_specs=pl.BlockSpec((TC_TOKENS, D), lambda i: (i, 0)),
        out_shape=jax.ShapeDtypeStruct((n_tokens, D), jnp.float32),
        scratch_shapes=[
            pltpu.VMEM((2, TC_TOKENS * N_SEL // 8, 8, 2 * D), jnp.float32),
            pltpu.SemaphoreType.DMA((2,)),
        ],
        compiler_params=pltpu.CompilerParams(
            dimension_semantics=("arbitrary",), vmem_limit_bytes=VMEM_LIMIT_BYTES),
        name="peer_tc",
    )(idx_flat, idx_flat, h2, gate, table2)


def _gate_act_kernel(pre_ref, gate_ref, o_ref):
    p = pre_ref[...]
    act = 0.5 * p * (1.0 + lax.erf(p * (1.0 / math.sqrt(2.0))))
    o_ref[...] = gate_ref[...] * act


def _gate_act(pre, gate):
    N = pre.shape[0]
    spec = pl.BlockSpec((EW_BLOCK, N_SEL), lambda i: (i, 0))
    return pl.pallas_call(
        _gate_act_kernel, grid=(N // EW_BLOCK,), in_specs=[spec, spec], out_specs=spec,
        out_shape=jax.ShapeDtypeStruct((N, N_SEL), jnp.float32), name="gate_act",
    )(pre, gate)


def _final_kernel(x1_ref, mod_ref, g_ref, *refs, normalize, starts):
    peer_refs, o_ref = refs[:-1], refs[-1]
    gate2 = mod_ref[0, 5:6, :]
    peer = peer_refs[0][...]
    for p in range(1, len(peer_refs)):
        peer = jnp.where(pl.program_id(0) >= starts[p], peer_refs[p][...], peer)
    y = x1_ref[0] + gate2 * peer
    o_ref[0] = _rms(y, g_ref[...]) if normalize else y


def _residual(x1, peers, b, mod3, g_final, normalize):
    _, S, D = x1.shape
    blocks = [p.shape[0] // EW_BLOCK for p in peers]
    starts = [sum(blocks[:p]) for p in range(len(peers))]
    tok = pl.BlockSpec((1, EW_BLOCK, D), lambda j: (0, j, 0))

    def piece_spec(start, n):
        return pl.BlockSpec((EW_BLOCK, D), lambda j: (jnp.clip(j - start, 0, n - 1), 0))

    return pl.pallas_call(
        functools.partial(_final_kernel, normalize=normalize, starts=tuple(starts)),
        grid=(S // EW_BLOCK,),
        in_specs=[tok,
                  pl.BlockSpec((1, N_MOD, D), lambda j: (b, 0, 0)),
                  pl.BlockSpec((1, D), lambda j: (0, 0))]
                 + [piece_spec(s, n) for s, n in zip(starts, blocks)],
        out_specs=tok,
        out_shape=jax.ShapeDtypeStruct((1, S, D), jnp.float32),
        compiler_params=pltpu.CompilerParams(vmem_limit_bytes=VMEM_LIMIT_BYTES),
        name="final_norm",
    )(x1, mod3, g_final, *peers)


def kernel(x, c, w_ada, b_ada, g_norm1, w_in, w_pool, pool_scale, sgu_ln_g, sgu_ln_b, w_spatial, b_spatial,
           w_out, g_norm2, w_query, sub_keys, expert_down, expert_up, g_final):
    B, S, D = x.shape
    depth = w_ada.shape[0]
    bf16 = jnp.bfloat16
    band = jnp.asarray(_pool_band(), bf16)
    c_pad = jnp.pad(c, ((0, 8 - B), (0, 0)))
    for l in range(depth):
        mod = _ada_mod(c_pad, w_ada[l].astype(bf16), b_ada[l][None, :])[:B]
        mod3 = mod.reshape(B, N_MOD, D)
        mix_w = (g_norm1[l][None, :], w_in[l].astype(bf16), band, w_pool[l].astype(bf16),
                 pool_scale[l][None, :], sgu_ln_g[l][None, :], sgu_ln_b[l][None, :], w_spatial[l],
                 b_spatial[l].T, w_out[l].astype(bf16), g_norm2[l][None, :])
        keys = sub_keys[l].reshape(2 * PEER_HEADS, N_KEYS, HALF_KEY).astype(bf16)
        wq = w_query[l].astype(bf16)
        table2 = jnp.concatenate([expert_down[l], expert_up[l]], axis=1)
        down_packed = _pack_bf16_pairs(expert_down[l])
        up_packed = _pack_bf16_pairs(expert_up[l])
        n_sc = S * SC_SHARE_NUM // SC_SHARE_DEN
        n_tc_a = (S - n_sc) // 2
        n_tc_b = S - n_sc - n_tc_a
        last = l + 1 == depth
        outs = []
        pending = None
        mod3_b = mod3
        for b in range(B):
            x1, h2 = _mixer(x, b, 1, mod3_b, *mix_w)
            h2f = h2.reshape(S, D)
            idx, gate = _query_topk(h2f, wq, keys)
            tc_a = _peer_tc(h2f, idx, gate, table2, n_sc, n_tc_a)
            pre = _peer_down(h2f, idx, down_packed, n_sc)
            gate_b = gate
            if pending is not None:
                p_x1, p_sc, p_a, p_b, p_seq = pending
                p_sc, tc_a = lax.optimization_barrier((p_sc, tc_a))
                out_prev = _residual(p_x1, [p_sc, p_a, p_b], p_seq, mod3, g_final[None, :], last)
                outs.append(out_prev)
                gate_b, _ = lax.optimization_barrier((gate, out_prev))
            tc_b = _peer_tc(h2f, idx, gate_b, table2, n_sc + n_tc_a, n_tc_b)
            pre, tc_b = lax.optimization_barrier((pre, tc_b))
            wgt = _gate_act(pre, gate)
            peer_sc = _peer_up(wgt, idx, up_packed, n_sc)
            mod3_b, _ = lax.optimization_barrier((mod3, wgt))
            pending = (x1, peer_sc, tc_a, tc_b, b)
        p_x1, p_sc, p_a, p_b, p_seq = pending
        outs.append(_residual(p_x1, [p_sc, p_a, p_b], p_seq, mod3, g_final[None, :], last))
        x = jnp.concatenate(outs, axis=0)
    return x
```

```python
import functools
import math

import jax
import jax.numpy as jnp
import numpy as np
from jax import lax
from jax.experimental import pallas as pl
from jax.experimental.pallas import tpu as pltpu
from jax.experimental.pallas import tpu_sc as plsc

D_MODEL = 1024
POOL_WIDTH = 512
SGU_WIDTH = 512
POOL_WINDOWS = (2, 4, 8, 16)
GROUP_DIM = 128
CHUNK = 128
SGU_HEADS = 4
IN_PROJ_WIDTH = POOL_WIDTH + 2 * SGU_WIDTH
PEER_HEADS = 8
N_KEYS = 128
HALF_KEY = 128
TOPK = 16
N_MOD = 6
EPS = 1e-6

MIX_BLOCK = 512
TOPK_BLOCK = 256
EW_BLOCK = 1024
VMEM_LIMIT_BYTES = 48 * 1024 * 1024

SC_LANES = 16
SC_TOKENS = 32
SC_RING = 4
N_SEL = PEER_HEADS * TOPK
TC_TOKENS = 8
SC_SHARE_NUM, SC_SHARE_DEN = 3, 4

N_CAND = 16 + 7 * 8 + 8


def _pool_band():
    t = np.arange(CHUNK)[:, None]
    j = np.arange(2 * CHUNK)[None, :]
    bands = []
    for w in POOL_WINDOWS:
        m = (j > CHUNK + t - w) & (j <= CHUNK + t)
        bands.append(np.where(m, 1.0 / w, 0.0))
    return np.stack(bands).astype(np.float32)


def _bf16_dot(a, b):
    return jnp.dot(a.astype(jnp.bfloat16), b.astype(jnp.bfloat16), preferred_element_type=jnp.float32)


def _ada_kernel(c_ref, w_ref, b_ref, o_ref):
    c = c_ref[...]
    c_act = c * jax.nn.sigmoid(c)
    o_ref[...] = _bf16_dot(c_act, w_ref[...]) + b_ref[...]


def _ada_mod(c_pad, w_ada, b_ada):
    rows = c_pad.shape[0]
    return pl.pallas_call(
        _ada_kernel,
        grid=(N_MOD,),
        in_specs=[
            pl.BlockSpec((rows, D_MODEL), lambda i: (0, 0)),
            pl.BlockSpec((D_MODEL, D_MODEL), lambda i: (0, i)),
            pl.BlockSpec((1, D_MODEL), lambda i: (0, i)),
        ],
        out_specs=pl.BlockSpec((rows, D_MODEL), lambda i: (0, i)),
        out_shape=jax.ShapeDtypeStruct((rows, N_MOD * D_MODEL), jnp.float32),
        name="ada_mod",
    )(c_pad, w_ada, b_ada)


def _rms(x, g):
    return x * lax.rsqrt(jnp.mean(x * x, axis=-1, keepdims=True) + EPS) * g


def _mix_kernel(x_ref, mod_ref, g1_ref, win_ref, band_ref, wpool_ref, pscale_ref, lng_ref, lnb_ref,
                wsp_ref, bsp_ref, wout_ref, g2_ref, x1_ref, h2_ref, aext_ref, mixed_ref):
    j = pl.program_id(1)
    x = x_ref[0]
    shift1, scale1, gate1 = mod_ref[0, 0:1, :], mod_ref[0, 1:2, :], mod_ref[0, 2:3, :]
    shift2, scale2 = mod_ref[0, 3:4, :], mod_ref[0, 4:5, :]

    h = _rms(x, g1_ref[...]) * (1.0 + scale1) + shift1
    proj = _bf16_dot(h, win_ref[...])
    a = proj[:, :POOL_WIDTH]
    u = proj[:, POOL_WIDTH:POOL_WIDTH + SGU_WIDTH]
    v = proj[:, POOL_WIDTH + SGU_WIDTH:]

    @pl.when(j == 0)
    def _():
        aext_ref[0:CHUNK, :] = jnp.zeros((CHUNK, POOL_WIDTH), jnp.float32)

    aext_ref[CHUNK:, :] = a

    mu = jnp.mean(v, axis=-1, keepdims=True)
    vc = v - mu
    vn = vc * lax.rsqrt(jnp.mean(vc * vc, axis=-1, keepdims=True) + EPS) * lng_ref[...] + lnb_ref[...]

    row = lax.broadcasted_iota(jnp.int32, (CHUNK, CHUNK), 0)
    col = lax.broadcasted_iota(jnp.int32, (CHUNK, CHUNK), 1)
    tril = col <= row
    t_col = lax.broadcasted_iota(jnp.int32, (CHUNK, 1), 0)

    for c in range(MIX_BLOCK // CHUNK):
        r0 = c * CHUNK
        pos = (j * MIX_BLOCK + r0 + 1 + t_col).astype(jnp.float32)
        for g, w in enumerate(POOL_WINDOWS):
            c0 = g * GROUP_DIM
            seg = aext_ref[r0:r0 + 2 * CHUNK, c0:c0 + GROUP_DIM]
            hi = seg.astype(jnp.bfloat16)
            lo = (seg - hi.astype(jnp.float32)).astype(jnp.bfloat16)
            band = band_ref[g]
            win = (jnp.dot(band, hi, preferred_element_type=jnp.float32)
                   + jnp.dot(band, lo, preferred_element_type=jnp.float32))
            mean = win * (float(w) / jnp.minimum(pos, float(w)))
            pooled = mean - seg[CHUNK:, :]
            y = _bf16_dot(pooled, wpool_ref[g]) * pscale_ref[:, c0:c0 + GROUP_DIM]
            mixed_ref[r0:r0 + CHUNK, c0:c0 + GROUP_DIM] = y.astype(jnp.bfloat16)
        for hh in range(SGU_HEADS):
            c0 = hh * GROUP_DIM
            wm = jnp.where(tril, wsp_ref[hh], 0.0)
            m = _bf16_dot(wm, vn[r0:r0 + CHUNK, c0:c0 + GROUP_DIM]) + bsp_ref[:, hh:hh + 1]
            s = u[r0:r0 + CHUNK, c0:c0 + GROUP_DIM] * m
            mixed_ref[r0:r0 + CHUNK, POOL_WIDTH + c0:POOL_WIDTH + c0 + GROUP_DIM] = s.astype(jnp.bfloat16)

    aext_ref[0:CHUNK, :] = aext_ref[MIX_BLOCK:MIX_BLOCK + CHUNK, :]

    x1 = x + gate1 * jnp.dot(mixed_ref[...], wout_ref[...], preferred_element_type=jnp.float32)
    x1_ref[0] = x1
    h2_ref[0] = _rms(x1, g2_ref[...]) * (1.0 + scale2) + shift2


def _mixer(x, b0, n_seq, mod3, g1, w_in, band, w_pool, pool_scale, ln_g, ln_b, w_sp, b_sp_t, w_out, g2):
    _, S, D = x.shape
    const2 = lambda b, j: (0, 0)
    const3 = lambda b, j: (0, 0, 0)
    tok = pl.BlockSpec((1, MIX_BLOCK, D), lambda b, j: (b, j, 0))
    return pl.pallas_call(
        _mix_kernel,
        grid=(n_seq, S // MIX_BLOCK),
        in_specs=[
            pl.BlockSpec((1, MIX_BLOCK, D), lambda b, j: (b0 + b, j, 0)),
            pl.BlockSpec((1, N_MOD, D), lambda b, j: (b0 + b, 0, 0)),
            pl.BlockSpec((1, D), const2),
            pl.BlockSpec((D, IN_PROJ_WIDTH), const2),
            pl.BlockSpec((len(POOL_WINDOWS), CHUNK, 2 * CHUNK), const3),
            pl.BlockSpec((len(POOL_WINDOWS), GROUP_DIM, GROUP_DIM), const3),
            pl.BlockSpec((1, POOL_WIDTH), const2),
            pl.BlockSpec((1, SGU_WIDTH), const2),
            pl.BlockSpec((1, SGU_WIDTH), const2),
            pl.BlockSpec((SGU_HEADS, CHUNK, CHUNK), const3),
            pl.BlockSpec((CHUNK, SGU_HEADS), const2),
            pl.BlockSpec((D, D), const2),
            pl.BlockSpec((1, D), const2),
        ],
        out_specs=[tok, tok],
        out_shape=[jax.ShapeDtypeStruct((n_seq, S, D), jnp.float32)] * 2,
        scratch_shapes=[
            pltpu.VMEM((MIX_BLOCK + CHUNK, POOL_WIDTH), jnp.float32),
            pltpu.VMEM((MIX_BLOCK, D), jnp.bfloat16),
        ],
        compiler_params=pltpu.CompilerParams(
            dimension_semantics=("arbitrary", "arbitrary"), vmem_limit_bytes=VMEM_LIMIT_BYTES),
        name="mixer",
    )(x, mod3, g1, w_in, band, w_pool, pool_scale, ln_g, ln_b, w_sp, b_sp_t, w_out, g2)


def _extract_top(s, ids, payload, n_out):
    vals, pays = [], []
    for _ in range(n_out):
        m = jnp.max(s, axis=0, keepdims=True)
        pick = jnp.min(jnp.where(s == m, ids, jnp.float32(1e9)), axis=0, keepdims=True)
        sel = ids == pick
        vals.append(m)
        pays.append(pick if payload is None
                    else jnp.max(jnp.where(sel, payload, -1.0), axis=0, keepdims=True))
        s = jnp.where(sel, -jnp.inf, s)
    return jnp.concatenate(vals, axis=0), jnp.concatenate(pays, axis=0)


def _topk_kernel(h2_ref, wq_ref, keys_ref, idx_ref, gate_ref):
    T = TOPK_BLOCK
    q = _bf16_dot(h2_ref[...], wq_ref[...])
    key_ids = lax.broadcasted_iota(jnp.int32, (N_KEYS, T), 0).astype(jnp.float32)
    row = lax.broadcasted_iota(jnp.int32, (N_CAND, T), 0)
    flat = jnp.where(row < 16, row,
                     jnp.where(row < 72, jnp.right_shift(row - 8, 3) * 16 + jnp.bitwise_and(row, 7),
                               (row - 64) * 16)).astype(jnp.float32)
    idx_rows, gate_rows = [], []
    for hd in range(PEER_HEADS):
        tops = []
        for p in range(2):
            hp = hd * 2 + p
            qs = q[:, hp * HALF_KEY:(hp + 1) * HALF_KEY].astype(jnp.bfloat16)
            st = lax.dot_general(keys_ref[hp], qs, (((1,), (1,)), ((), ())),
                                 preferred_element_type=jnp.float32)
            tops.append(_extract_top(st, key_ids, None, TOPK))
        (s0, i0), (s1, i1) = tops
        i0 = i0 * float(N_KEYS)
        cand = jnp.concatenate(
            [s0[0:1] + s1] + [s0[a:a + 1] + s1[0:8] for a in range(1, 8)] + [s0[8:16] + s1[0:1]], axis=0)
        cidx = jnp.concatenate(
            [i0[0:1] + i1] + [i0[a:a + 1] + i1[0:8] for a in range(1, 8)] + [i0[8:16] + i1[0:1]],
            axis=0)
        best, sel_idx = _extract_top(cand, flat, cidx, TOPK)
        e = jnp.exp(best - jnp.max(best, axis=0, keepdims=True))
        gate_rows.append(e / jnp.sum(e, axis=0, keepdims=True))
        idx_rows.append(sel_idx)
    idx_ref[...] = jnp.concatenate(idx_rows, axis=0).T.astype(jnp.int32)
    gate_ref[...] = jnp.concatenate(gate_rows, axis=0).T


def _query_topk(h2, w_query, keys):
    N, D = h2.shape
    tok = pl.BlockSpec((TOPK_BLOCK, N_SEL), lambda i: (i, 0))
    return pl.pallas_call(
        _topk_kernel,
        grid=(N // TOPK_BLOCK,),
        in_specs=[
            pl.BlockSpec((TOPK_BLOCK, D), lambda i: (i, 0)),
            pl.BlockSpec((D, 2 * PEER_HEADS * HALF_KEY), lambda i: (0, 0)),
            pl.BlockSpec((2 * PEER_HEADS, N_KEYS, HALF_KEY), lambda i: (0, 0, 0)),
        ],
        out_specs=[tok, tok],
        out_shape=[jax.ShapeDtypeStruct((N, N_SEL), jnp.int32), jax.ShapeDtypeStruct((N, N_SEL), jnp.float32)],
        compiler_params=pltpu.CompilerParams(
            dimension_semantics=("arbitrary",), vmem_limit_bytes=VMEM_LIMIT_BYTES),
        name="query_topk",
    )(h2, w_query, keys)


def _sc_mesh_and_workers():
    info = plsc.get_sparse_core_info()
    assert info.num_lanes == SC_LANES
    mesh = plsc.VectorSubcoreMesh(core_axis_name="core", subcore_axis_name="subcore")
    return mesh, info.num_cores, info.num_cores * info.num_subcores


def _gather_ring(tab_hbm, idx_v, rows_v, sems, consume):
    look = SC_RING - 1

    def copy(t, hd):
        slot = hd % SC_RING
        return pltpu.make_async_copy(
            tab_hbm.at[idx_v.at[t, pl.ds(hd * TOPK, TOPK)]], rows_v.at[slot], sems.at[slot])

    for hd in range(look):
        copy(0, hd).start()

    @pl.loop(0, SC_TOKENS)
    def _(t):
        for hd in range(PEER_HEADS):
            nxt = hd + look
            if nxt < PEER_HEADS:
                copy(t, nxt).start()
            else:
                @pl.when(t + 1 < SC_TOKENS)
                def _():
                    copy(t + 1, nxt - PEER_HEADS).start()
            copy(t, hd).wait()
            consume(t, hd, rows_v.at[hd % SC_RING])


def _pack_rows(table):
    E, D = table.shape
    exact = lax.bitcast_convert_type(table[:, :D // 2], jnp.int32)
    bits = lax.bitcast_convert_type(table[:, D // 2:].astype(jnp.bfloat16), jnp.uint16).astype(jnp.uint32)
    words = bits[:, :D // 4] | (bits[:, D // 4:] << 16)
    return jnp.concatenate([exact, lax.bitcast_convert_type(words, jnp.int32)], axis=1)


def _unpack_pair(words):
    lo = lax.bitcast_convert_type(lax.shift_left(words, jnp.int32(16)), jnp.float32)
    hi = lax.bitcast_convert_type(lax.bitwise_and(words, jnp.int32(-65536)), jnp.float32)
    return lo, hi


def _row_vectors(rows, k, s):
    a = lax.bitcast_convert_type(rows[k, pl.ds(s * 2 * SC_LANES, SC_LANES)], jnp.float32)
    b = lax.bitcast_convert_type(rows[k, pl.ds(s * 2 * SC_LANES + SC_LANES, SC_LANES)], jnp.float32)
    lo, hi = _unpack_pair(rows[k, pl.ds(D_MODEL // 2 + s * SC_LANES, SC_LANES)])
    return (a, b, lo, hi)


def _step_columns(s):
    half = D_MODEL // 2
    quarter = D_MODEL // 4
    return (s * 2 * SC_LANES, s * 2 * SC_LANES + SC_LANES, half + s * SC_LANES, half + quarter + s * SC_LANES)


SC_STEPS = D_MODEL // (4 * SC_LANES)


def _peer_down(h2, idx, packed, N):
    D = h2.shape[1]
    mesh, n_cores, n_workers = _sc_mesh_and_workers()
    per_worker = N // n_workers

    def body(h_hbm, idx_hbm, tab_hbm, out_hbm, idx_v, h_v, rows_v, out_v, tr_v, sems):
        wid = lax.axis_index("subcore") * n_cores + lax.axis_index("core")
        lane = lax.iota(jnp.int32, SC_LANES)

        def consume(t, hd, rows):
            def dot_step(s, accs):
                hs = [h_v[t, pl.ds(c, SC_LANES)] for c in _step_columns(s)]
                out = []
                for k, acc in enumerate(accs):
                    r = _row_vectors(rows, k, s)
                    out.append(acc + ((r[0] * hs[0] + r[1] * hs[1]) + (r[2] * hs[2] + r[3] * hs[3])))
                return tuple(out)

            accs = lax.fori_loop(0, SC_STEPS, dot_step,
                                 tuple(jnp.zeros((SC_LANES,), jnp.float32) for _ in range(TOPK)))
            for k in range(TOPK):
                tr_v[k, :] = accs[k]
            tot = jnp.zeros((SC_LANES,), jnp.float32)
            for l in range(SC_LANES):
                tot = tot + plsc.load_gather(tr_v, [lane, jnp.full((SC_LANES,), l, jnp.int32)])
            out_v[t, pl.ds(hd * TOPK, TOPK)] = tot

        @pl.loop(0, per_worker // SC_TOKENS)
        def _(blk):
            base = wid * per_worker + blk * SC_TOKENS
            pltpu.sync_copy(idx_hbm.at[pl.ds(base, SC_TOKENS)], idx_v)
            pltpu.sync_copy(h_hbm.at[pl.ds(base, SC_TOKENS)], h_v)
            _gather_ring(tab_hbm, idx_v, rows_v, sems, consume)
            pltpu.sync_copy(out_v, out_hbm.at[pl.ds(base, SC_TOKENS)])

    return pl.kernel(
        body,
        out_type=jax.ShapeDtypeStruct((N, N_SEL), jnp.float32),
        mesh=mesh,
        scratch_types=[
            pltpu.VMEM((SC_TOKENS, N_SEL), jnp.int32),
            pltpu.VMEM((SC_TOKENS, D), jnp.float32),
            pltpu.VMEM((SC_RING, TOPK, packed.shape[1]), jnp.int32),
            pltpu.VMEM((SC_TOKENS, N_SEL), jnp.float32),
            pltpu.VMEM((TOPK, SC_LANES), jnp.float32),
            pltpu.SemaphoreType.DMA((SC_RING,)),
        ],
        compiler_params=pltpu.CompilerParams(needs_layout_passes=False),
        name="peer_down",
    )(h2, idx, packed)


def _peer_up(wgt, idx, packed, N):
    D = D_MODEL
    mesh, n_cores, n_workers = _sc_mesh_and_workers()
    per_worker = N // n_workers

    def body(w_hbm, idx_hbm, tab_hbm, out_hbm, idx_v, w_v, rows_v, out_v, sems):
        wid = lax.axis_index("subcore") * n_cores + lax.axis_index("core")

        def consume(t, hd, rows):
            t_vec = jnp.full((SC_LANES,), t, jnp.int32)
            ws = [plsc.load_gather(w_v, [t_vec, jnp.full((SC_LANES,), hd * TOPK + k, jnp.int32)])
                  for k in range(TOPK)]

            def tree_sum(terms):
                while len(terms) > 1:
                    terms = [terms[i] + terms[i + 1] for i in range(0, len(terms) - 1, 2)] + (
                        [terms[-1]] if len(terms) % 2 else [])
                return terms[0]

            @plsc.parallel_loop(0, SC_STEPS)
            def _(s):
                vecs = [_row_vectors(rows, k, s) for k in range(TOPK)]
                for part, c in enumerate(_step_columns(s)):
                    sl = pl.ds(c, SC_LANES)
                    terms = [ws[k] * vecs[k][part] for k in range(TOPK)]
                    if hd > 0:
                        terms.append(out_v[t, sl])
                    out_v[t, sl] = tree_sum(terms)

        @pl.loop(0, per_worker // SC_TOKENS)
        def _(blk):
            base = wid * per_worker + blk * SC_TOKENS
            pltpu.sync_copy(idx_hbm.at[pl.ds(base, SC_TOKENS)], idx_v)
            pltpu.sync_copy(w_hbm.at[pl.ds(base, SC_TOKENS)], w_v)
            _gather_ring(tab_hbm, idx_v, rows_v, sems, consume)
            pltpu.sync_copy(out_v, out_hbm.at[pl.ds(base, SC_TOKENS)])

    return pl.kernel(
        body,
        out_type=jax.ShapeDtypeStruct((N, D), jnp.float32),
        mesh=mesh,
        scratch_types=[
            pltpu.VMEM((SC_TOKENS, N_SEL), jnp.int32),
            pltpu.VMEM((SC_TOKENS, N_SEL), jnp.float32),
            pltpu.VMEM((SC_RING, TOPK, packed.shape[1]), jnp.int32),
            pltpu.VMEM((SC_TOKENS, D), jnp.float32),
            pltpu.SemaphoreType.DMA((SC_RING,)),
        ],
        compiler_params=pltpu.CompilerParams(needs_layout_passes=False),
        name="peer_up",
    )(wgt, idx, packed)


def _peer_tc_kernel(idx_cur, idx_nxt, h_ref, gate_ref, tab_hbm, o_ref, gbuf, sems):
    i = pl.program_id(0)
    n = pl.num_programs(0)
    groups = N_SEL // 8

    def issue(idx_ref, slot):
        def body(g, carry):
            for j in range(8):
                e = idx_ref[g * 8 + j]
                pltpu.make_async_copy(
                    tab_hbm.at[pl.ds(e, 1), :], gbuf.at[slot, g, pl.ds(j, 1), :], sems.at[slot]).start()
            return carry
        lax.fori_loop(0, TC_TOKENS * groups, body, 0)

    @pl.when(i == 0)
    def _():
        issue(idx_cur, 0)

    @pl.when(i + 1 < n)
    def _():
        issue(idx_nxt, (i + 1) % 2)

    slot = i % 2
    pltpu.make_async_copy(gbuf.at[slot], gbuf.at[slot], sems.at[slot]).wait()

    def rows(t, c0):
        return gbuf[slot, t * groups:(t + 1) * groups, :, c0:c0 + D_MODEL].reshape(N_SEL, D_MODEL)

    lane = lax.broadcasted_iota(jnp.int32, (N_SEL, 128), 1)
    pre_all = jnp.zeros((N_SEL, 128), jnp.float32)
    for t in range(TC_TOKENS):
        prod = rows(t, 0) * h_ref[t:t + 1, :]
        part = prod[:, 0:128]
        for c in range(1, D_MODEL // 128):
            part = part + prod[:, c * 128:(c + 1) * 128]
        pre = jnp.sum(part, axis=1, keepdims=True)
        pre_all = jnp.where(lane == t, pre, pre_all)
    act = 0.5 * pre_all * (1.0 + lax.erf(pre_all * (1.0 / math.sqrt(2.0))))
    gate_t = jnp.concatenate([gate_ref[...]] * (128 // TC_TOKENS), axis=0).T
    wgt = gate_t * act
    for t in range(TC_TOKENS):
        w_col = jnp.sum(jnp.where(lane == t, wgt, 0.0), axis=1, keepdims=True)
        up = rows(t, D_MODEL) * w_col
        o_ref[t:t + 1, :] = jnp.sum(up, axis=0, keepdims=True)


def _peer_tc(h2, idx, gate, table2, row0, n_tokens):
    D = h2.shape[1]
    steps = n_tokens // TC_TOKENS
    blk0 = row0 // TC_TOKENS
    idx_flat = idx.reshape(-1)
    return pl.pallas_call(
        _peer_tc_kernel,
        grid=(steps,),
        in_specs=[
            pl.BlockSpec((TC_TOKENS * N_SEL,), lambda i: (blk0 + i,), memory_space=pltpu.SMEM),
            pl.BlockSpec((TC_TOKENS * N_SEL,), lambda i: (blk0 + jnp.minimum(i + 1, steps - 1),),
                         memory_space=pltpu.SMEM),
            pl.BlockSpec((TC_TOKENS, D), lambda i: (blk0 + i, 0)),
            pl.BlockSpec((TC_TOKENS, N_SEL), lambda i: (blk0 + i, 0)),
            pl.BlockSpec(memory_space=pl.ANY),
        ],
        out_specs=pl.BlockSpec((TC_TOKENS, D), lambda i: (i, 0)),
        out_shape=jax.ShapeDtypeStruct((n_tokens, D), jnp.float32),
        scratch_shapes=[
            pltpu.VMEM((2, TC_TOKENS * N_SEL // 8, 8, 2 * D), jnp.float32),
            pltpu.SemaphoreType.DMA((2,)),
        ],
        compiler_params=pltpu.CompilerParams(
            dimension_semantics=("arbitrary",), vmem_limit_bytes=VMEM_LIMIT_BYTES),
        name="peer_tc",
    )(idx_flat, idx_flat, h2, gate, table2)


def _gate_act_kernel(pre_ref, gate_ref, o_ref):
    p = pre_ref[...]
    act = 0.5 * p * (1.0 + lax.erf(p * (1.0 / math.sqrt(2.0))))
    o_ref[...] = gate_ref[...] * act


def _gate_act(pre, gate):
    N = pre.shape[0]
    spec = pl.BlockSpec((EW_BLOCK, N_SEL), lambda i: (i, 0))
    return pl.pallas_call(
        _gate_act_kernel, grid=(N // EW_BLOCK,), in_specs=[spec, spec], out_specs=spec,
        out_shape=jax.ShapeDtypeStruct((N, N_SEL), jnp.float32), name="gate_act",
    )(pre, gate)


def _final_kernel(x1_ref, mod_ref, g_ref, *refs, normalize, starts):
    peer_refs, o_ref = refs[:-1], refs[-1]
    gate2 = mod_ref[0, 5:6, :]
    peer = peer_refs[0][...]
    for p in range(1, len(peer_refs)):
        peer = jnp.where(pl.program_id(0) >= starts[p], peer_refs[p][...], peer)
    y = x1_ref[0] + gate2 * peer
    o_ref[0] = _rms(y, g_ref[...]) if normalize else y


def _residual(x1, peers, b, mod3, g_final, normalize):
    _, S, D = x1.shape
    blocks = [p.shape[0] // EW_BLOCK for p in peers]
    starts = [sum(blocks[:p]) for p in range(len(peers))]
    tok = pl.BlockSpec((1, EW_BLOCK, D), lambda j: (0, j, 0))

    def piece_spec(start, n):
        return pl.BlockSpec((EW_BLOCK, D), lambda j: (jnp.clip(j - start, 0, n - 1), 0))

    return pl.pallas_call(
        functools.partial(_final_kernel, normalize=normalize, starts=tuple(starts)),
        grid=(S // EW_BLOCK,),
        in_specs=[tok,
                  pl.BlockSpec((1, N_MOD, D), lambda j: (b, 0, 0)),
                  pl.BlockSpec((1, D), lambda j: (0, 0))]
                 + [piece_spec(s, n) for s, n in zip(starts, blocks)],
        out_specs=tok,
        out_shape=jax.ShapeDtypeStruct((1, S, D), jnp.float32),
        compiler_params=pltpu.CompilerParams(vmem_limit_bytes=VMEM_LIMIT_BYTES),
        name="final_norm",
    )(x1, mod3, g_final, *peers)


def kernel(x, c, w_ada, b_ada, g_norm1, w_in, w_pool, pool_scale, sgu_ln_g, sgu_ln_b, w_spatial, b_spatial,
           w_out, g_norm2, w_query, sub_keys, expert_down, expert_up, g_final):
    B, S, D = x.shape
    depth = w_ada.shape[0]
    bf16 = jnp.bfloat16
    band = jnp.asarray(_pool_band(), bf16)
    c_pad = jnp.pad(c, ((0, 8 - B), (0, 0)))
    for l in range(depth):
        mod = _ada_mod(c_pad, w_ada[l].astype(bf16), b_ada[l][None, :])[:B]
        mod3 = mod.reshape(B, N_MOD, D)
        mix_w = (g_norm1[l][None, :], w_in[l].astype(bf16), band, w_pool[l].astype(bf16),
                 pool_scale[l][None, :], sgu_ln_g[l][None, :], sgu_ln_b[l][None, :], w_spatial[l],
                 b_spatial[l].T, w_out[l].astype(bf16), g_norm2[l][None, :])
        keys = sub_keys[l].reshape(2 * PEER_HEADS, N_KEYS, HALF_KEY).astype(bf16)
        wq = w_query[l].astype(bf16)
        table2 = jnp.concatenate([expert_down[l], expert_up[l]], axis=1)
        down_packed = _pack_rows(expert_down[l])
        up_packed = _pack_rows(expert_up[l])
        n_sc = S * SC_SHARE_NUM // SC_SHARE_DEN
        n_tc_a = (S - n_sc) // 2
        n_tc_b = S - n_sc - n_tc_a
        last = l + 1 == depth
        outs = []
        pending = None
        mod3_b = mod3
        for b in range(B):
            x1, h2 = _mixer(x, b, 1, mod3_b, *mix_w)
            h2f = h2.reshape(S, D)
            idx, gate = _query_topk(h2f, wq, keys)
            tc_a = _peer_tc(h2f, idx, gate, table2, n_sc, n_tc_a)
            pre = _peer_down(h2f, idx, down_packed, n_sc)
            gate_b = gate
            if pending is not None:
                p_x1, p_sc, p_a, p_b, p_seq = pending
                p_sc, tc_a = lax.optimization_barrier((p_sc, tc_a))
                out_prev = _residual(p_x1, [p_sc, p_a, p_b], p_seq, mod3, g_final[None, :], last)
                outs.append(out_prev)
                gate_b, _ = lax.optimization_barrier((gate, out_prev))
            tc_b = _peer_tc(h2f, idx, gate_b, table2, n_sc + n_tc_a, n_tc_b)
            pre, tc_a, tc_b = lax.optimization_barrier((pre, tc_a, tc_b))
            wgt = _gate_act(pre, gate)
            peer_sc = _peer_up(wgt, idx, up_packed, n_sc)
            mod3_b, _ = lax.optimization_barrier((mod3, wgt))
            pending = (x1, peer_sc, tc_a, tc_b, b)
        p_x1, p_sc, p_a, p_b, p_seq = pending
        outs.append(_residual(p_x1, [p_sc, p_a, p_b], p_seq, mod3, g_final[None, :], last))
        x = jnp.concatenate(outs, axis=0)
    return x
```

```python
import functools
import math

import jax
import jax.numpy as jnp
import numpy as np
from jax import lax
from jax.experimental import pallas as pl
from jax.experimental.pallas import tpu as pltpu
from jax.experimental.pallas import tpu_sc as plsc

D_MODEL = 1024
POOL_WIDTH = 512
SGU_WIDTH = 512
POOL_WINDOWS = (2, 4, 8, 16)
GROUP_DIM = 128
CHUNK = 128
SGU_HEADS = 4
IN_PROJ_WIDTH = POOL_WIDTH + 2 * SGU_WIDTH
PEER_HEADS = 8
N_KEYS = 128
HALF_KEY = 128
TOPK = 16
N_MOD = 6
EPS = 1e-6

MIX_BLOCK = 512
TOPK_BLOCK = 256
EW_BLOCK = 1024
VMEM_LIMIT_BYTES = 48 * 1024 * 1024

SC_LANES = 16
SC_TOKENS = 32
SC_RING = 8
N_SEL = PEER_HEADS * TOPK
TC_TOKENS = 8
SC_SHARE_NUM, SC_SHARE_DEN = 3, 4

N_CAND = 16 + 7 * 8 + 8


def _pool_band():
    t = np.arange(CHUNK)[:, None]
    j = np.arange(2 * CHUNK)[None, :]
    bands = []
    for w in POOL_WINDOWS:
        m = (j > CHUNK + t - w) & (j <= CHUNK + t)
        bands.append(np.where(m, 1.0 / w, 0.0))
    return np.stack(bands).astype(np.float32)


def _bf16_dot(a, b):
    return jnp.dot(a.astype(jnp.bfloat16), b.astype(jnp.bfloat16), preferred_element_type=jnp.float32)


def _ada_kernel(c_ref, w_ref, b_ref, o_ref):
    c = c_ref[...]
    c_act = c * jax.nn.sigmoid(c)
    o_ref[...] = _bf16_dot(c_act, w_ref[...]) + b_ref[...]


def _ada_mod(c_pad, w_ada, b_ada):
    rows = c_pad.shape[0]
    return pl.pallas_call(
        _ada_kernel,
        grid=(N_MOD,),
        in_specs=[
            pl.BlockSpec((rows, D_MODEL), lambda i: (0, 0)),
            pl.BlockSpec((D_MODEL, D_MODEL), lambda i: (0, i)),
            pl.BlockSpec((1, D_MODEL), lambda i: (0, i)),
        ],
        out_specs=pl.BlockSpec((rows, D_MODEL), lambda i: (0, i)),
        out_shape=jax.ShapeDtypeStruct((rows, N_MOD * D_MODEL), jnp.float32),
        name="ada_mod",
    )(c_pad, w_ada, b_ada)


def _rms(x, g):
    return x * lax.rsqrt(jnp.mean(x * x, axis=-1, keepdims=True) + EPS) * g


def _mix_kernel(x_ref, mod_ref, g1_ref, win_ref, band_ref, wpool_ref, pscale_ref, lng_ref, lnb_ref,
                wsp_ref, bsp_ref, wout_ref, g2_ref, x1_ref, h2_ref, aext_ref, mixed_ref):
    j = pl.program_id(1)
    x = x_ref[0]
    shift1, scale1, gate1 = mod_ref[0, 0:1, :], mod_ref[0, 1:2, :], mod_ref[0, 2:3, :]
    shift2, scale2 = mod_ref[0, 3:4, :], mod_ref[0, 4:5, :]

    h = _rms(x, g1_ref[...]) * (1.0 + scale1) + shift1
    proj = _bf16_dot(h, win_ref[...])
    a = proj[:, :POOL_WIDTH]
    u = proj[:, POOL_WIDTH:POOL_WIDTH + SGU_WIDTH]
    v = proj[:, POOL_WIDTH + SGU_WIDTH:]

    @pl.when(j == 0)
    def _():
        aext_ref[0:CHUNK, :] = jnp.zeros((CHUNK, POOL_WIDTH), jnp.float32)

    aext_ref[CHUNK:, :] = a

    mu = jnp.mean(v, axis=-1, keepdims=True)
    vc = v - mu
    vn = vc * lax.rsqrt(jnp.mean(vc * vc, axis=-1, keepdims=True) + EPS) * lng_ref[...] + lnb_ref[...]

    row = lax.broadcasted_iota(jnp.int32, (CHUNK, CHUNK), 0)
    col = lax.broadcasted_iota(jnp.int32, (CHUNK, CHUNK), 1)
    tril = col <= row
    t_col = lax.broadcasted_iota(jnp.int32, (CHUNK, 1), 0)

    for c in range(MIX_BLOCK // CHUNK):
        r0 = c * CHUNK
        pos = (j * MIX_BLOCK + r0 + 1 + t_col).astype(jnp.float32)
        for g, w in enumerate(POOL_WINDOWS):
            c0 = g * GROUP_DIM
            seg = aext_ref[r0:r0 + 2 * CHUNK, c0:c0 + GROUP_DIM]
            hi = seg.astype(jnp.bfloat16)
            lo = (seg - hi.astype(jnp.float32)).astype(jnp.bfloat16)
            band = band_ref[g]
            win = (jnp.dot(band, hi, preferred_element_type=jnp.float32)
                   + jnp.dot(band, lo, preferred_element_type=jnp.float32))
            mean = win * (float(w) / jnp.minimum(pos, float(w)))
            pooled = mean - seg[CHUNK:, :]
            y = _bf16_dot(pooled, wpool_ref[g]) * pscale_ref[:, c0:c0 + GROUP_DIM]
            mixed_ref[r0:r0 + CHUNK, c0:c0 + GROUP_DIM] = y.astype(jnp.bfloat16)
        for hh in range(SGU_HEADS):
            c0 = hh * GROUP_DIM
            wm = jnp.where(tril, wsp_ref[hh], 0.0)
            m = _bf16_dot(wm, vn[r0:r0 + CHUNK, c0:c0 + GROUP_DIM]) + bsp_ref[:, hh:hh + 1]
            s = u[r0:r0 + CHUNK, c0:c0 + GROUP_DIM] * m
            mixed_ref[r0:r0 + CHUNK, POOL_WIDTH + c0:POOL_WIDTH + c0 + GROUP_DIM] = s.astype(jnp.bfloat16)

    aext_ref[0:CHUNK, :] = aext_ref[MIX_BLOCK:MIX_BLOCK + CHUNK, :]

    x1 = x + gate1 * jnp.dot(mixed_ref[...], wout_ref[...], preferred_element_type=jnp.float32)
    x1_ref[0] = x1
    h2_ref[0] = _rms(x1, g2_ref[...]) * (1.0 + scale2) + shift2


def _mixer(x, b0, n_seq, mod3, g1, w_in, band, w_pool, pool_scale, ln_g, ln_b, w_sp, b_sp_t, w_out, g2):
    _, S, D = x.shape
    const2 = lambda b, j: (0, 0)
    const3 = lambda b, j: (0, 0, 0)
    tok = pl.BlockSpec((1, MIX_BLOCK, D), lambda b, j: (b, j, 0))
    return pl.pallas_call(
        _mix_kernel,
        grid=(n_seq, S // MIX_BLOCK),
        in_specs=[
            pl.BlockSpec((1, MIX_BLOCK, D), lambda b, j: (b0 + b, j, 0)),
            pl.BlockSpec((1, N_MOD, D), lambda b, j: (b0 + b, 0, 0)),
            pl.BlockSpec((1, D), const2),
            pl.BlockSpec((D, IN_PROJ_WIDTH), const2),
            pl.BlockSpec((len(POOL_WINDOWS), CHUNK, 2 * CHUNK), const3),
            pl.BlockSpec((len(POOL_WINDOWS), GROUP_DIM, GROUP_DIM), const3),
            pl.BlockSpec((1, POOL_WIDTH), const2),
            pl.BlockSpec((1, SGU_WIDTH), const2),
            pl.BlockSpec((1, SGU_WIDTH), const2),
            pl.BlockSpec((SGU_HEADS, CHUNK, CHUNK), const3),
            pl.BlockSpec((CHUNK, SGU_HEADS), const2),
            pl.BlockSpec((D, D), const2),
            pl.BlockSpec((1, D), const2),
        ],
        out_specs=[tok, tok],
        out_shape=[jax.ShapeDtypeStruct((n_seq, S, D), jnp.float32)] * 2,
        scratch_shapes=[
            pltpu.VMEM((MIX_BLOCK + CHUNK, POOL_WIDTH), jnp.float32),
            pltpu.VMEM((MIX_BLOCK, D), jnp.bfloat16),
        ],
        compiler_params=pltpu.CompilerParams(
            dimension_semantics=("arbitrary", "arbitrary"), vmem_limit_bytes=VMEM_LIMIT_BYTES),
        name="mixer",
    )(x, mod3, g1, w_in, band, w_pool, pool_scale, ln_g, ln_b, w_sp, b_sp_t, w_out, g2)


def _extract_top(s, ids, payload, n_out):
    vals, pays = [], []
    for _ in range(n_out):
        m = jnp.max(s, axis=0, keepdims=True)
        pick = jnp.min(jnp.where(s == m, ids, jnp.float32(1e9)), axis=0, keepdims=True)
        sel = ids == pick
        vals.append(m)
        pays.append(pick if payload is None
                    else jnp.max(jnp.where(sel, payload, -1.0), axis=0, keepdims=True))
        s = jnp.where(sel, -jnp.inf, s)
    return jnp.concatenate(vals, axis=0), jnp.concatenate(pays, axis=0)


def _topk_kernel(h2_ref, wq_ref, keys_ref, idx_ref, gate_ref):
    T = TOPK_BLOCK
    q = _bf16_dot(h2_ref[...], wq_ref[...])
    key_ids = lax.broadcasted_iota(jnp.int32, (N_KEYS, T), 0).astype(jnp.float32)
    row = lax.broadcasted_iota(jnp.int32, (N_CAND, T), 0)
    flat = jnp.where(row < 16, row,
                     jnp.where(row < 72, jnp.right_shift(row - 8, 3) * 16 + jnp.bitwise_and(row, 7),
                               (row - 64) * 16)).astype(jnp.float32)
    idx_rows, gate_rows = [], []
    for hd in range(PEER_HEADS):
        tops = []
        for p in range(2):
            hp = hd * 2 + p
            qs = q[:, hp * HALF_KEY:(hp + 1) * HALF_KEY].astype(jnp.bfloat16)
            st = lax.dot_general(keys_ref[hp], qs, (((1,), (1,)), ((), ())),
                                 preferred_element_type=jnp.float32)
            tops.append(_extract_top(st, key_ids, None, TOPK))
        (s0, i0), (s1, i1) = tops
        i0 = i0 * float(N_KEYS)
        cand = jnp.concatenate(
            [s0[0:1] + s1] + [s0[a:a + 1] + s1[0:8] for a in range(1, 8)] + [s0[8:16] + s1[0:1]], axis=0)
        cidx = jnp.concatenate(
            [i0[0:1] + i1] + [i0[a:a + 1] + i1[0:8] for a in range(1, 8)] + [i0[8:16] + i1[0:1]],
            axis=0)
        best, sel_idx = _extract_top(cand, flat, cidx, TOPK)
        e = jnp.exp(best - jnp.max(best, axis=0, keepdims=True))
        gate_rows.append(e / jnp.sum(e, axis=0, keepdims=True))
        idx_rows.append(sel_idx)
    idx_ref[...] = jnp.concatenate(idx_rows, axis=0).T.astype(jnp.int32)
    gate_ref[...] = jnp.concatenate(gate_rows, axis=0).T


def _query_topk(h2, w_query, keys):
    N, D = h2.shape
    tok = pl.BlockSpec((TOPK_BLOCK, N_SEL), lambda i: (i, 0))
    return pl.pallas_call(
        _topk_kernel,
        grid=(N // TOPK_BLOCK,),
        in_specs=[
            pl.BlockSpec((TOPK_BLOCK, D), lambda i: (i, 0)),
            pl.BlockSpec((D, 2 * PEER_HEADS * HALF_KEY), lambda i: (0, 0)),
            pl.BlockSpec((2 * PEER_HEADS, N_KEYS, HALF_KEY), lambda i: (0, 0, 0)),
        ],
        out_specs=[tok, tok],
        out_shape=[jax.ShapeDtypeStruct((N, N_SEL), jnp.int32), jax.ShapeDtypeStruct((N, N_SEL), jnp.float32)],
        compiler_params=pltpu.CompilerParams(
            dimension_semantics=("arbitrary",), vmem_limit_bytes=VMEM_LIMIT_BYTES),
        name="query_topk",
    )(h2, w_query, keys)


def _sc_mesh_and_workers():
    info = plsc.get_sparse_core_info()
    assert info.num_lanes == SC_LANES
    mesh = plsc.VectorSubcoreMesh(core_axis_name="core", subcore_axis_name="subcore")
    return mesh, info.num_cores, info.num_cores * info.num_subcores


def _gather_ring(tab_hbm, idx_v, rows_v, sems, consume):
    look = SC_RING - 1

    def copy(t, hd):
        slot = hd % SC_RING
        return pltpu.make_async_copy(
            tab_hbm.at[idx_v.at[t, pl.ds(hd * TOPK, TOPK)]], rows_v.at[slot], sems.at[slot])

    for hd in range(look):
        copy(0, hd).start()

    @pl.loop(0, SC_TOKENS)
    def _(t):
        for hd in range(PEER_HEADS):
            nxt = hd + look
            if nxt < PEER_HEADS:
                copy(t, nxt).start()
            else:
                @pl.when(t + 1 < SC_TOKENS)
                def _():
                    copy(t + 1, nxt - PEER_HEADS).start()
            copy(t, hd).wait()
            consume(t, hd, rows_v.at[hd % SC_RING])


SC_EXACT_COLS = 0
SC_PAIR_WORDS = (D_MODEL - SC_EXACT_COLS) // 2
SC_ROW_WORDS = SC_EXACT_COLS + SC_PAIR_WORDS
SC_STEPS = SC_PAIR_WORDS // SC_LANES
SC_EXACT_RUNS = SC_EXACT_COLS // (SC_LANES * SC_STEPS)
assert SC_EXACT_RUNS * SC_LANES * SC_STEPS == SC_EXACT_COLS


def _pack_rows(table):
    rest = table[:, SC_EXACT_COLS:].astype(jnp.bfloat16)
    bits = lax.bitcast_convert_type(rest, jnp.uint16).astype(jnp.uint32)
    words = lax.bitcast_convert_type(bits[:, :SC_PAIR_WORDS] | (bits[:, SC_PAIR_WORDS:] << 16), jnp.int32)
    if SC_EXACT_COLS == 0:
        return words
    return jnp.concatenate([lax.bitcast_convert_type(table[:, :SC_EXACT_COLS], jnp.int32), words], axis=1)


def _unpack_pair(words):
    lo = lax.bitcast_convert_type(lax.shift_left(words, jnp.int32(16)), jnp.float32)
    hi = lax.bitcast_convert_type(lax.bitwise_and(words, jnp.int32(-65536)), jnp.float32)
    return lo, hi


def _step_columns(s):
    exact = [(s * SC_EXACT_RUNS + r) * SC_LANES for r in range(SC_EXACT_RUNS)]
    return exact + [SC_EXACT_COLS + s * SC_LANES, SC_EXACT_COLS + SC_PAIR_WORDS + s * SC_LANES]


def _row_vectors(rows, k, s):
    exact = [lax.bitcast_convert_type(rows[k, pl.ds((s * SC_EXACT_RUNS + r) * SC_LANES, SC_LANES)], jnp.float32)
             for r in range(SC_EXACT_RUNS)]
    return exact + list(_unpack_pair(rows[k, pl.ds(SC_EXACT_COLS + s * SC_LANES, SC_LANES)]))


def _tree_sum(terms):
    terms = list(terms)
    while len(terms) > 1:
        terms = [terms[i] + terms[i + 1] for i in range(0, len(terms) - 1, 2)] + (
            [terms[-1]] if len(terms) % 2 else [])
    return terms[0]


def _peer_down(h2, idx, packed, N):
    D = h2.shape[1]
    mesh, n_cores, n_workers = _sc_mesh_and_workers()
    per_worker = N // n_workers

    def body(h_hbm, idx_hbm, tab_hbm, out_hbm, idx_v, h_v, rows_v, out_v, tr_v, sems):
        wid = lax.axis_index("subcore") * n_cores + lax.axis_index("core")
        lane = lax.iota(jnp.int32, SC_LANES)

        def consume(t, hd, rows):
            def dot_step(s, accs):
                hs = [h_v[t, pl.ds(c, SC_LANES)] for c in _step_columns(s)]
                out = []
                for k, acc in enumerate(accs):
                    r = _row_vectors(rows, k, s)
                    out.append(acc + _tree_sum(ri * hi for ri, hi in zip(r, hs)))
                return tuple(out)

            accs = lax.fori_loop(0, SC_STEPS, dot_step,
                                 tuple(jnp.zeros((SC_LANES,), jnp.float32) for _ in range(TOPK)))
            for k in range(TOPK):
                tr_v[k, :] = accs[k]
            tot = jnp.zeros((SC_LANES,), jnp.float32)
            for l in range(SC_LANES):
                tot = tot + plsc.load_gather(tr_v, [lane, jnp.full((SC_LANES,), l, jnp.int32)])
            out_v[t, pl.ds(hd * TOPK, TOPK)] = tot

        @pl.loop(0, per_worker // SC_TOKENS)
        def _(blk):
            base = wid * per_worker + blk * SC_TOKENS
            pltpu.sync_copy(idx_hbm.at[pl.ds(base, SC_TOKENS)], idx_v)
            pltpu.sync_copy(h_hbm.at[pl.ds(base, SC_TOKENS)], h_v)
            _gather_ring(tab_hbm, idx_v, rows_v, sems, consume)
            pltpu.sync_copy(out_v, out_hbm.at[pl.ds(base, SC_TOKENS)])

    return pl.kernel(
        body,
        out_type=jax.ShapeDtypeStruct((N, N_SEL), jnp.float32),
        mesh=mesh,
        scratch_types=[
            pltpu.VMEM((SC_TOKENS, N_SEL), jnp.int32),
            pltpu.VMEM((SC_TOKENS, D), jnp.float32),
            pltpu.VMEM((SC_RING, TOPK, packed.shape[1]), jnp.int32),
            pltpu.VMEM((SC_TOKENS, N_SEL), jnp.float32),
            pltpu.VMEM((TOPK, SC_LANES), jnp.float32),
            pltpu.SemaphoreType.DMA((SC_RING,)),
        ],
        compiler_params=pltpu.CompilerParams(needs_layout_passes=False),
        name="peer_down",
    )(h2, idx, packed)


def _peer_up(wgt, idx, packed, N):
    D = D_MODEL
    mesh, n_cores, n_workers = _sc_mesh_and_workers()
    per_worker = N // n_workers

    def body(w_hbm, idx_hbm, tab_hbm, out_hbm, idx_v, w_v, rows_v, out_v, sems):
        wid = lax.axis_index("subcore") * n_cores + lax.axis_index("core")

        def consume(t, hd, rows):
            t_vec = jnp.full((SC_LANES,), t, jnp.int32)
            ws = [plsc.load_gather(w_v, [t_vec, jnp.full((SC_LANES,), hd * TOPK + k, jnp.int32)])
                  for k in range(TOPK)]

            @plsc.parallel_loop(0, SC_STEPS)
            def _(s):
                vecs = [_row_vectors(rows, k, s) for k in range(TOPK)]
                for part, c in enumerate(_step_columns(s)):
                    sl = pl.ds(c, SC_LANES)
                    terms = [ws[k] * vecs[k][part] for k in range(TOPK)]
                    if hd > 0:
                        terms.append(out_v[t, sl])
                    out_v[t, sl] = _tree_sum(terms)

        @pl.loop(0, per_worker // SC_TOKENS)
        def _(blk):
            base = wid * per_worker + blk * SC_TOKENS
            pltpu.sync_copy(idx_hbm.at[pl.ds(base, SC_TOKENS)], idx_v)
            pltpu.sync_copy(w_hbm.at[pl.ds(base, SC_TOKENS)], w_v)
            _gather_ring(tab_hbm, idx_v, rows_v, sems, consume)
            pltpu.sync_copy(out_v, out_hbm.at[pl.ds(base, SC_TOKENS)])

    return pl.kernel(
        body,
        out_type=jax.ShapeDtypeStruct((N, D), jnp.float32),
        mesh=mesh,
        scratch_types=[
            pltpu.VMEM((SC_TOKENS, N_SEL), jnp.int32),
            pltpu.VMEM((SC_TOKENS, N_SEL), jnp.float32),
            pltpu.VMEM((SC_RING, TOPK, packed.shape[1]), jnp.int32),
            pltpu.VMEM((SC_TOKENS, D), jnp.float32),
            pltpu.SemaphoreType.DMA((SC_RING,)),
        ],
        compiler_params=pltpu.CompilerParams(needs_layout_passes=False),
        name="peer_up",
    )(wgt, idx, packed)


def _peer_tc_kernel(idx_cur, idx_nxt, h_ref, gate_ref, tab_hbm, o_ref, gbuf, sems):
    i = pl.program_id(0)
    n = pl.num_programs(0)
    groups = N_SEL // 8

    def issue(idx_ref, slot):
        def body(g, carry):
            for j in range(8):
                e = idx_ref[g * 8 + j]
                pltpu.make_async_copy(
                    tab_hbm.at[pl.ds(e, 1), :], gbuf.at[slot, g, pl.ds(j, 1), :], sems.at[slot]).start()
            return carry
        lax.fori_loop(0, TC_TOKENS * groups, body, 0)

    @pl.when(i == 0)
    def _():
        issue(idx_cur, 0)

    @pl.when(i + 1 < n)
    def _():
        issue(idx_nxt, (i + 1) % 2)

    slot = i % 2
    pltpu.make_async_copy(gbuf.at[slot], gbuf.at[slot], sems.at[slot]).wait()

    def rows(t, c0):
        return gbuf[slot, t * groups:(t + 1) * groups, :, c0:c0 + D_MODEL].reshape(N_SEL, D_MODEL)

    lane = lax.broadcasted_iota(jnp.int32, (N_SEL, 128), 1)
    pre_all = jnp.zeros((N_SEL, 128), jnp.float32)
    for t in range(TC_TOKENS):
        prod = rows(t, 0) * h_ref[t:t + 1, :]
        part = prod[:, 0:128]
        for c in range(1, D_MODEL // 128):
            part = part + prod[:, c * 128:(c + 1) * 128]
        pre = jnp.sum(part, axis=1, keepdims=True)
        pre_all = jnp.where(lane == t, pre, pre_all)
    act = 0.5 * pre_all * (1.0 + lax.erf(pre_all * (1.0 / math.sqrt(2.0))))
    gate_t = jnp.concatenate([gate_ref[...]] * (128 // TC_TOKENS), axis=0).T
    wgt = gate_t * act
    for t in range(TC_TOKENS):
        w_col = jnp.sum(jnp.where(lane == t, wgt, 0.0), axis=1, keepdims=True)
        up = rows(t, D_MODEL) * w_col
        o_ref[t:t + 1, :] = jnp.sum(up, axis=0, keepdims=True)


def _peer_tc(h2, idx, gate, table2, row0, n_tokens):
    D = h2.shape[1]
    steps = n_tokens // TC_TOKENS
    blk0 = row0 // TC_TOKENS
    idx_flat = idx.reshape(-1)
    return pl.pallas_call(
        _peer_tc_kernel,
        grid=(steps,),
        in_specs=[
            pl.BlockSpec((TC_TOKENS * N_SEL,), lambda i: (blk0 + i,), memory_space=pltpu.SMEM),
            pl.BlockSpec((TC_TOKENS * N_SEL,), lambda i: (blk0 + jnp.minimum(i + 1, steps - 1),),
                         memory_space=pltpu.SMEM),
            pl.BlockSpec((TC_TOKENS, D), lambda i: (blk0 + i, 0)),
            pl.BlockSpec((TC_TOKENS, N_SEL), lambda i: (blk0 + i, 0)),
            pl.BlockSpec(memory_space=pl.ANY),
        ],
        out_specs=pl.BlockSpec((TC_TOKENS, D), lambda i: (i, 0)),
        out_shape=jax.ShapeDtypeStruct((n_tokens, D), jnp.float32),
        scratch_shapes=[
            pltpu.VMEM((2, TC_TOKENS * N_SEL // 8, 8, 2 * D), jnp.float32),
            pltpu.SemaphoreType.DMA((2,)),
        ],
        compiler_params=pltpu.CompilerParams(
            dimension_semantics=("arbitrary",), vmem_limit_bytes=VMEM_LIMIT_BYTES),
        name="peer_tc",
    )(idx_flat, idx_flat, h2, gate, table2)


def _gate_act_kernel(pre_ref, gate_ref, o_ref):
    p = pre_ref[...]
    act = 0.5 * p * (1.0 + lax.erf(p * (1.0 / math.sqrt(2.0))))
    o_ref[...] = gate_ref[...] * act


def _gate_act(pre, gate):
    N = pre.shape[0]
    spec = pl.BlockSpec((EW_BLOCK, N_SEL), lambda i: (i, 0))
    return pl.pallas_call(
        _gate_act_kernel, grid=(N // EW_BLOCK,), in_specs=[spec, spec], out_specs=spec,
        out_shape=jax.ShapeDtypeStruct((N, N_SEL), jnp.float32), name="gate_act",
    )(pre, gate)


def _final_kernel(x1_ref, mod_ref, g_ref, *refs, normalize, starts):
    peer_refs, o_ref = refs[:-1], refs[-1]
    gate2 = mod_ref[0, 5:6, :]
    peer = peer_refs[0][...]
    for p in range(1, len(peer_refs)):
        peer = jnp.where(pl.program_id(0) >= starts[p], peer_refs[p][...], peer)
    y = x1_ref[0] + gate2 * peer
    o_ref[0] = _rms(y, g_ref[...]) if normalize else y


def _residual(x1, peers, b, mod3, g_final, normalize):
    _, S, D = x1.shape
    blocks = [p.shape[0] // EW_BLOCK for p in peers]
    starts = [sum(blocks[:p]) for p in range(len(peers))]
    tok = pl.BlockSpec((1, EW_BLOCK, D), lambda j: (0, j, 0))

    def piece_spec(start, n):
        return pl.BlockSpec((EW_BLOCK, D), lambda j: (jnp.clip(j - start, 0, n - 1), 0))

    return pl.pallas_call(
        functools.partial(_final_kernel, normalize=normalize, starts=tuple(starts)),
        grid=(S // EW_BLOCK,),
        in_specs=[tok,
                  pl.BlockSpec((1, N_MOD, D), lambda j: (b, 0, 0)),
                  pl.BlockSpec((1, D), lambda j: (0, 0))]
                 + [piece_spec(s, n) for s, n in zip(starts, blocks)],
        out_specs=tok,
        out_shape=jax.ShapeDtypeStruct((1, S, D), jnp.float32),
        compiler_params=pltpu.CompilerParams(vmem_limit_bytes=VMEM_LIMIT_BYTES),
        name="final_norm",
    )(x1, mod3, g_final, *peers)


def kernel(x, c, w_ada, b_ada, g_norm1, w_in, w_pool, pool_scale, sgu_ln_g, sgu_ln_b, w_spatial, b_spatial,
           w_out, g_norm2, w_query, sub_keys, expert_down, expert_up, g_final):
    B, S, D = x.shape
    depth = w_ada.shape[0]
    bf16 = jnp.bfloat16
    band = jnp.asarray(_pool_band(), bf16)
    c_pad = jnp.pad(c, ((0, 8 - B), (0, 0)))
    for l in range(depth):
        mod = _ada_mod(c_pad, w_ada[l].astype(bf16), b_ada[l][None, :])[:B]
        mod3 = mod.reshape(B, N_MOD, D)
        mix_w = (g_norm1[l][None, :], w_in[l].astype(bf16), band, w_pool[l].astype(bf16),
                 pool_scale[l][None, :], sgu_ln_g[l][None, :], sgu_ln_b[l][None, :], w_spatial[l],
                 b_spatial[l].T, w_out[l].astype(bf16), g_norm2[l][None, :])
        keys = sub_keys[l].reshape(2 * PEER_HEADS, N_KEYS, HALF_KEY).astype(bf16)
        wq = w_query[l].astype(bf16)
        table2 = jnp.concatenate([expert_down[l], expert_up[l]], axis=1)
        down_packed = _pack_rows(expert_down[l])
        up_packed = _pack_rows(expert_up[l])
        n_sc = S * SC_SHARE_NUM // SC_SHARE_DEN
        n_tc_a = (S - n_sc) // 2
        n_tc_b = S - n_sc - n_tc_a
        last = l + 1 == depth
        outs = []
        pending = None
        mod3_b = mod3
        for b in range(B):
            x1, h2 = _mixer(x, b, 1, mod3_b, *mix_w)
            h2f = h2.reshape(S, D)
            idx, gate = _query_topk(h2f, wq, keys)
            tc_a = _peer_tc(h2f, idx, gate, table2, n_sc, n_tc_a)
            pre = _peer_down(h2f, idx, down_packed, n_sc)
            gate_b = gate
            if pending is not None:
                p_x1, p_sc, p_a, p_b, p_seq = pending
                p_sc, tc_a = lax.optimization_barrier((p_sc, tc_a))
                out_prev = _residual(p_x1, [p_sc, p_a, p_b], p_seq, mod3, g_final[None, :], last)
                outs.append(out_prev)
                gate_b, _ = lax.optimization_barrier((gate, out_prev))
            tc_b = _peer_tc(h2f, idx, gate_b, table2, n_sc + n_tc_a, n_tc_b)
            pre, tc_a, tc_b = lax.optimization_barrier((pre, tc_a, tc_b))
            wgt = _gate_act(pre, gate)
            peer_sc = _peer_up(wgt, idx, up_packed, n_sc)
            mod3_b, _ = lax.optimization_barrier((mod3, wgt))
            pending = (x1, peer_sc, tc_a, tc_b, b)
        p_x1, p_sc, p_a, p_b, p_seq = pending
        outs.append(_residual(p_x1, [p_sc, p_a, p_b], p_seq, mod3, g_final[None, :], last))
        x = jnp.concatenate(outs, axis=0)
    return x
```

```python
import functools
import math

import jax
import jax.numpy as jnp
import numpy as np
from jax import lax
from jax.experimental import pallas as pl
from jax.experimental.pallas import tpu as pltpu
from jax.experimental.pallas import tpu_sc as plsc

D_MODEL = 1024
POOL_WIDTH = 512
SGU_WIDTH = 512
POOL_WINDOWS = (2, 4, 8, 16)
GROUP_DIM = 128
CHUNK = 128
SGU_HEADS = 4
IN_PROJ_WIDTH = POOL_WIDTH + 2 * SGU_WIDTH
PEER_HEADS = 8
N_KEYS = 128
HALF_KEY = 128
TOPK = 16
N_MOD = 6
EPS = 1e-6

MIX_BLOCK = 512
TOPK_BLOCK = 256
EW_BLOCK = 256
VMEM_LIMIT_BYTES = 48 * 1024 * 1024

SC_LANES = 16
SC_TOKENS = 32
SC_RING = 8
N_SEL = PEER_HEADS * TOPK
TC_TOKENS = 8

N_CAND = 16 + 7 * 8 + 8


def _pool_band():
    t = np.arange(CHUNK)[:, None]
    j = np.arange(2 * CHUNK)[None, :]
    bands = []
    for w in POOL_WINDOWS:
        m = (j > CHUNK + t - w) & (j <= CHUNK + t)
        bands.append(np.where(m, 1.0 / w, 0.0))
    return np.stack(bands).astype(np.float32)


def _bf16_dot(a, b):
    return jnp.dot(a.astype(jnp.bfloat16), b.astype(jnp.bfloat16), preferred_element_type=jnp.float32)


def _ada_kernel(c_ref, w_ref, b_ref, o_ref):
    c = c_ref[...]
    c_act = c * jax.nn.sigmoid(c)
    o_ref[...] = _bf16_dot(c_act, w_ref[...]) + b_ref[...]


def _ada_mod(c_pad, w_ada, b_ada):
    rows = c_pad.shape[0]
    return pl.pallas_call(
        _ada_kernel,
        grid=(N_MOD,),
        in_specs=[
            pl.BlockSpec((rows, D_MODEL), lambda i: (0, 0)),
            pl.BlockSpec((D_MODEL, D_MODEL), lambda i: (0, i)),
            pl.BlockSpec((1, D_MODEL), lambda i: (0, i)),
        ],
        out_specs=pl.BlockSpec((rows, D_MODEL), lambda i: (0, i)),
        out_shape=jax.ShapeDtypeStruct((rows, N_MOD * D_MODEL), jnp.float32),
        name="ada_mod",
    )(c_pad, w_ada, b_ada)


def _rms(x, g):
    return x * lax.rsqrt(jnp.mean(x * x, axis=-1, keepdims=True) + EPS) * g


def _mix_kernel(x_ref, mod_ref, g1_ref, win_ref, band_ref, wpool_ref, pscale_ref, lng_ref, lnb_ref,
                wsp_ref, bsp_ref, wout_ref, g2_ref, x1_ref, h2_ref, aext_ref, mixed_ref):
    j = pl.program_id(1)
    x = x_ref[0]
    shift1, scale1, gate1 = mod_ref[0, 0:1, :], mod_ref[0, 1:2, :], mod_ref[0, 2:3, :]
    shift2, scale2 = mod_ref[0, 3:4, :], mod_ref[0, 4:5, :]

    h = _rms(x, g1_ref[...]) * (1.0 + scale1) + shift1
    proj = _bf16_dot(h, win_ref[...])
    a = proj[:, :POOL_WIDTH]
    u = proj[:, POOL_WIDTH:POOL_WIDTH + SGU_WIDTH]
    v = proj[:, POOL_WIDTH + SGU_WIDTH:]

    @pl.when(j == 0)
    def _():
        aext_ref[0:CHUNK, :] = jnp.zeros((CHUNK, POOL_WIDTH), jnp.float32)

    aext_ref[CHUNK:, :] = a

    mu = jnp.mean(v, axis=-1, keepdims=True)
    vc = v - mu
    vn = vc * lax.rsqrt(jnp.mean(vc * vc, axis=-1, keepdims=True) + EPS) * lng_ref[...] + lnb_ref[...]

    row = lax.broadcasted_iota(jnp.int32, (CHUNK, CHUNK), 0)
    col = lax.broadcasted_iota(jnp.int32, (CHUNK, CHUNK), 1)
    tril = col <= row
    t_col = lax.broadcasted_iota(jnp.int32, (CHUNK, 1), 0)

    for c in range(MIX_BLOCK // CHUNK):
        r0 = c * CHUNK
        pos = (j * MIX_BLOCK + r0 + 1 + t_col).astype(jnp.float32)
        for g, w in enumerate(POOL_WINDOWS):
            c0 = g * GROUP_DIM
            seg = aext_ref[r0:r0 + 2 * CHUNK, c0:c0 + GROUP_DIM]
            hi = seg.astype(jnp.bfloat16)
            lo = (seg - hi.astype(jnp.float32)).astype(jnp.bfloat16)
            band = band_ref[g]
            win = (jnp.dot(band, hi, preferred_element_type=jnp.float32)
                   + jnp.dot(band, lo, preferred_element_type=jnp.float32))
            mean = win * (float(w) / jnp.minimum(pos, float(w)))
            pooled = mean - seg[CHUNK:, :]
            y = _bf16_dot(pooled, wpool_ref[g]) * pscale_ref[:, c0:c0 + GROUP_DIM]
            mixed_ref[r0:r0 + CHUNK, c0:c0 + GROUP_DIM] = y.astype(jnp.bfloat16)
        for hh in range(SGU_HEADS):
            c0 = hh * GROUP_DIM
            wm = jnp.where(tril, wsp_ref[hh], 0.0)
            m = _bf16_dot(wm, vn[r0:r0 + CHUNK, c0:c0 + GROUP_DIM]) + bsp_ref[:, hh:hh + 1]
            s = u[r0:r0 + CHUNK, c0:c0 + GROUP_DIM] * m
            mixed_ref[r0:r0 + CHUNK, POOL_WIDTH + c0:POOL_WIDTH + c0 + GROUP_DIM] = s.astype(jnp.bfloat16)

    aext_ref[0:CHUNK, :] = aext_ref[MIX_BLOCK:MIX_BLOCK + CHUNK, :]

    x1 = x + gate1 * jnp.dot(mixed_ref[...], wout_ref[...], preferred_element_type=jnp.float32)
    x1_ref[0] = x1
    h2_ref[0] = _rms(x1, g2_ref[...]) * (1.0 + scale2) + shift2


def _mixer(x, b0, n_seq, mod3, g1, w_in, band, w_pool, pool_scale, ln_g, ln_b, w_sp, b_sp_t, w_out, g2):
    _, S, D = x.shape
    const2 = lambda b, j: (0, 0)
    const3 = lambda b, j: (0, 0, 0)
    tok = pl.BlockSpec((1, MIX_BLOCK, D), lambda b, j: (b, j, 0))
    return pl.pallas_call(
        _mix_kernel,
        grid=(n_seq, S // MIX_BLOCK),
        in_specs=[
            pl.BlockSpec((1, MIX_BLOCK, D), lambda b, j: (b0 + b, j, 0)),
            pl.BlockSpec((1, N_MOD, D), lambda b, j: (b0 + b, 0, 0)),
            pl.BlockSpec((1, D), const2),
            pl.BlockSpec((D, IN_PROJ_WIDTH), const2),
            pl.BlockSpec((len(POOL_WINDOWS), CHUNK, 2 * CHUNK), const3),
            pl.BlockSpec((len(POOL_WINDOWS), GROUP_DIM, GROUP_DIM), const3),
            pl.BlockSpec((1, POOL_WIDTH), const2),
            pl.BlockSpec((1, SGU_WIDTH), const2),
            pl.BlockSpec((1, SGU_WIDTH), const2),
            pl.BlockSpec((SGU_HEADS, CHUNK, CHUNK), const3),
            pl.BlockSpec((CHUNK, SGU_HEADS), const2),
            pl.BlockSpec((D, D), const2),
            pl.BlockSpec((1, D), const2),
        ],
        out_specs=[tok, tok],
        out_shape=[jax.ShapeDtypeStruct((n_seq, S, D), jnp.float32)] * 2,
        scratch_shapes=[
            pltpu.VMEM((MIX_BLOCK + CHUNK, POOL_WIDTH), jnp.float32),
            pltpu.VMEM((MIX_BLOCK, D), jnp.bfloat16),
        ],
        compiler_params=pltpu.CompilerParams(
            dimension_semantics=("arbitrary", "arbitrary"), vmem_limit_bytes=VMEM_LIMIT_BYTES),
        name="mixer",
    )(x, mod3, g1, w_in, band, w_pool, pool_scale, ln_g, ln_b, w_sp, b_sp_t, w_out, g2)


def _extract_top(s, ids, payload, n_out):
    vals, pays = [], []
    for _ in range(n_out):
        m = jnp.max(s, axis=0, keepdims=True)
        pick = jnp.min(jnp.where(s == m, ids, jnp.float32(1e9)), axis=0, keepdims=True)
        sel = ids == pick
        vals.append(m)
        pays.append(pick if payload is None
                    else jnp.max(jnp.where(sel, payload, -1.0), axis=0, keepdims=True))
        s = jnp.where(sel, -jnp.inf, s)
    return jnp.concatenate(vals, axis=0), jnp.concatenate(pays, axis=0)


def _topk_kernel(h2_ref, wq_ref, keys_ref, idx_ref, gate_ref):
    T = TOPK_BLOCK
    q = _bf16_dot(h2_ref[...], wq_ref[...])
    key_ids = lax.broadcasted_iota(jnp.int32, (N_KEYS, T), 0).astype(jnp.float32)
    row = lax.broadcasted_iota(jnp.int32, (N_CAND, T), 0)
    flat = jnp.where(row < 16, row,
                     jnp.where(row < 72, jnp.right_shift(row - 8, 3) * 16 + jnp.bitwise_and(row, 7),
                               (row - 64) * 16)).astype(jnp.float32)
    idx_rows, gate_rows = [], []
    for hd in range(PEER_HEADS):
        tops = []
        for p in range(2):
            hp = hd * 2 + p
            qs = q[:, hp * HALF_KEY:(hp + 1) * HALF_KEY].astype(jnp.bfloat16)
            st = lax.dot_general(keys_ref[hp], qs, (((1,), (1,)), ((), ())),
                                 preferred_element_type=jnp.float32)
            tops.append(_extract_top(st, key_ids, None, TOPK))
        (s0, i0), (s1, i1) = tops
        i0 = i0 * float(N_KEYS)
        cand = jnp.concatenate(
            [s0[0:1] + s1] + [s0[a:a + 1] + s1[0:8] for a in range(1, 8)] + [s0[8:16] + s1[0:1]], axis=0)
        cidx = jnp.concatenate(
            [i0[0:1] + i1] + [i0[a:a + 1] + i1[0:8] for a in range(1, 8)] + [i0[8:16] + i1[0:1]],
            axis=0)
        best, sel_idx = _extract_top(cand, flat, cidx, TOPK)
        e = jnp.exp(best - jnp.max(best, axis=0, keepdims=True))
        gate_rows.append(e / jnp.sum(e, axis=0, keepdims=True))
        idx_rows.append(sel_idx)
    idx_ref[...] = jnp.concatenate(idx_rows, axis=0).T.astype(jnp.int32)
    gate_ref[...] = jnp.concatenate(gate_rows, axis=0).T


def _query_topk(h2, w_query, keys):
    N, D = h2.shape
    tok = pl.BlockSpec((TOPK_BLOCK, N_SEL), lambda i: (i, 0))
    return pl.pallas_call(
        _topk_kernel,
        grid=(N // TOPK_BLOCK,),
        in_specs=[
            pl.BlockSpec((TOPK_BLOCK, D), lambda i: (i, 0)),
            pl.BlockSpec((D, 2 * PEER_HEADS * HALF_KEY), lambda i: (0, 0)),
            pl.BlockSpec((2 * PEER_HEADS, N_KEYS, HALF_KEY), lambda i: (0, 0, 0)),
        ],
        out_specs=[tok, tok],
        out_shape=[jax.ShapeDtypeStruct((N, N_SEL), jnp.int32), jax.ShapeDtypeStruct((N, N_SEL), jnp.float32)],
        compiler_params=pltpu.CompilerParams(
            dimension_semantics=("arbitrary",), vmem_limit_bytes=VMEM_LIMIT_BYTES),
        name="query_topk",
    )(h2, w_query, keys)


def _sc_mesh_and_workers():
    info = plsc.get_sparse_core_info()
    assert info.num_lanes == SC_LANES
    mesh = plsc.VectorSubcoreMesh(core_axis_name="core", subcore_axis_name="subcore")
    return mesh, info.num_cores, info.num_cores * info.num_subcores


def _gather_ring(tab_hbm, idx_v, rows_v, sems, consume, n_tokens):
    look = SC_RING - 1

    def copy(t, hd):
        slot = hd % SC_RING
        return pltpu.make_async_copy(
            tab_hbm.at[idx_v.at[t, pl.ds(hd * TOPK, TOPK)]], rows_v.at[slot], sems.at[slot])

    for hd in range(look):
        copy(0, hd).start()

    @pl.loop(0, n_tokens)
    def _(t):
        for hd in range(PEER_HEADS):
            nxt = hd + look
            if nxt < PEER_HEADS:
                copy(t, nxt).start()
            else:
                @pl.when(t + 1 < n_tokens)
                def _():
                    copy(t + 1, nxt - PEER_HEADS).start()
            copy(t, hd).wait()
            consume(t, hd, rows_v.at[hd % SC_RING])


def _for_token_blocks(first, per_worker, block):
    n_full, tail = divmod(per_worker, SC_TOKENS)
    assert tail % 8 == 0, "HBM row offsets of the staged slices must stay 8-aligned"

    @pl.loop(0, n_full)
    def _(blk):
        block(first + blk * SC_TOKENS, SC_TOKENS)

    if tail:
        block(first + n_full * SC_TOKENS, tail)


SC_EXACT_COLS = 0
SC_PAIR_WORDS = (D_MODEL - SC_EXACT_COLS) // 2
SC_ROW_WORDS = SC_EXACT_COLS + SC_PAIR_WORDS
SC_STEPS = SC_PAIR_WORDS // SC_LANES
SC_EXACT_RUNS = SC_EXACT_COLS // (SC_LANES * SC_STEPS)
assert SC_EXACT_RUNS * SC_LANES * SC_STEPS == SC_EXACT_COLS


def _pack_rows(table):
    rest = table[:, SC_EXACT_COLS:].astype(jnp.bfloat16)
    bits = lax.bitcast_convert_type(rest, jnp.uint16).astype(jnp.uint32)
    words = lax.bitcast_convert_type(bits[:, :SC_PAIR_WORDS] | (bits[:, SC_PAIR_WORDS:] << 16), jnp.int32)
    if SC_EXACT_COLS == 0:
        return words
    return jnp.concatenate([lax.bitcast_convert_type(table[:, :SC_EXACT_COLS], jnp.int32), words], axis=1)


def _unpack_pair(words):
    lo = lax.bitcast_convert_type(lax.shift_left(words, jnp.int32(16)), jnp.float32)
    hi = lax.bitcast_convert_type(lax.bitwise_and(words, jnp.int32(-65536)), jnp.float32)
    return lo, hi


def _step_columns(s):
    exact = [(s * SC_EXACT_RUNS + r) * SC_LANES for r in range(SC_EXACT_RUNS)]
    return exact + [SC_EXACT_COLS + s * SC_LANES, SC_EXACT_COLS + SC_PAIR_WORDS + s * SC_LANES]


def _row_vectors(rows, k, s):
    exact = [lax.bitcast_convert_type(rows[k, pl.ds((s * SC_EXACT_RUNS + r) * SC_LANES, SC_LANES)], jnp.float32)
             for r in range(SC_EXACT_RUNS)]
    return exact + list(_unpack_pair(rows[k, pl.ds(SC_EXACT_COLS + s * SC_LANES, SC_LANES)]))


def _tree_sum(terms):
    terms = list(terms)
    while len(terms) > 1:
        terms = [terms[i] + terms[i + 1] for i in range(0, len(terms) - 1, 2)] + (
            [terms[-1]] if len(terms) % 2 else [])
    return terms[0]


def _peer_down(h2, idx, packed, N):
    D = h2.shape[1]
    mesh, n_cores, n_workers = _sc_mesh_and_workers()
    per_worker = N // n_workers

    def body(h_hbm, idx_hbm, tab_hbm, out_hbm, idx_v, h_v, rows_v, out_v, tr_v, sems):
        wid = lax.axis_index("subcore") * n_cores + lax.axis_index("core")
        lane = lax.iota(jnp.int32, SC_LANES)

        def consume(t, hd, rows):
            def dot_step(s, accs):
                hs = [h_v[t, pl.ds(c, SC_LANES)] for c in _step_columns(s)]
                out = []
                for k, acc in enumerate(accs):
                    r = _row_vectors(rows, k, s)
                    out.append(acc + _tree_sum(ri * hi for ri, hi in zip(r, hs)))
                return tuple(out)

            accs = lax.fori_loop(0, SC_STEPS, dot_step,
                                 tuple(jnp.zeros((SC_LANES,), jnp.float32) for _ in range(TOPK)))
            for k in range(TOPK):
                tr_v[k, :] = accs[k]
            tot = jnp.zeros((SC_LANES,), jnp.float32)
            for l in range(SC_LANES):
                tot = tot + plsc.load_gather(tr_v, [lane, jnp.full((SC_LANES,), l, jnp.int32)])
            out_v[t, pl.ds(hd * TOPK, TOPK)] = tot

        def block(base, n):
            pltpu.sync_copy(idx_hbm.at[pl.ds(base, n)], idx_v.at[pl.ds(0, n)])
            pltpu.sync_copy(h_hbm.at[pl.ds(base, n)], h_v.at[pl.ds(0, n)])
            _gather_ring(tab_hbm, idx_v, rows_v, sems, consume, n)
            pltpu.sync_copy(out_v.at[pl.ds(0, n)], out_hbm.at[pl.ds(base, n)])

        _for_token_blocks(wid * per_worker, per_worker, block)

    return pl.kernel(
        body,
        out_type=jax.ShapeDtypeStruct((N, N_SEL), jnp.float32),
        mesh=mesh,
        scratch_types=[
            pltpu.VMEM((SC_TOKENS, N_SEL), jnp.int32),
            pltpu.VMEM((SC_TOKENS, D), jnp.float32),
            pltpu.VMEM((SC_RING, TOPK, packed.shape[1]), jnp.int32),
            pltpu.VMEM((SC_TOKENS, N_SEL), jnp.float32),
            pltpu.VMEM((TOPK, SC_LANES), jnp.float32),
            pltpu.SemaphoreType.DMA((SC_RING,)),
        ],
        compiler_params=pltpu.CompilerParams(needs_layout_passes=False),
        name="peer_down",
    )(h2, idx, packed)


def _peer_up(wgt, idx, packed, N):
    D = D_MODEL
    mesh, n_cores, n_workers = _sc_mesh_and_workers()
    per_worker = N // n_workers

    def body(w_hbm, idx_hbm, tab_hbm, out_hbm, idx_v, w_v, rows_v, out_v, sems):
        wid = lax.axis_index("subcore") * n_cores + lax.axis_index("core")

        def consume(t, hd, rows):
            t_vec = jnp.full((SC_LANES,), t, jnp.int32)
            ws = [plsc.load_gather(w_v, [t_vec, jnp.full((SC_LANES,), hd * TOPK + k, jnp.int32)])
                  for k in range(TOPK)]

            @plsc.parallel_loop(0, SC_STEPS)
            def _(s):
                vecs = [_row_vectors(rows, k, s) for k in range(TOPK)]
                for part, c in enumerate(_step_columns(s)):
                    sl = pl.ds(c, SC_LANES)
                    terms = [ws[k] * vecs[k][part] for k in range(TOPK)]
                    if hd > 0:
                        terms.append(out_v[t, sl])
                    out_v[t, sl] = _tree_sum(terms)

        def block(base, n):
            pltpu.sync_copy(idx_hbm.at[pl.ds(base, n)], idx_v.at[pl.ds(0, n)])
            pltpu.sync_copy(w_hbm.at[pl.ds(base, n)], w_v.at[pl.ds(0, n)])
            _gather_ring(tab_hbm, idx_v, rows_v, sems, consume, n)
            pltpu.sync_copy(out_v.at[pl.ds(0, n)], out_hbm.at[pl.ds(base, n)])

        _for_token_blocks(wid * per_worker, per_worker, block)

    return pl.kernel(
        body,
        out_type=jax.ShapeDtypeStruct((N, D), jnp.float32),
        mesh=mesh,
        scratch_types=[
            pltpu.VMEM((SC_TOKENS, N_SEL), jnp.int32),
            pltpu.VMEM((SC_TOKENS, N_SEL), jnp.float32),
            pltpu.VMEM((SC_RING, TOPK, packed.shape[1]), jnp.int32),
            pltpu.VMEM((SC_TOKENS, D), jnp.float32),
            pltpu.SemaphoreType.DMA((SC_RING,)),
        ],
        compiler_params=pltpu.CompilerParams(needs_layout_passes=False),
        name="peer_up",
    )(wgt, idx, packed)


def _peer_tc_kernel(idx_cur, idx_nxt, h_ref, gate_ref, tab_hbm, o_ref, gbuf, sems):
    i = pl.program_id(0)
    n = pl.num_programs(0)
    groups = N_SEL // 8

    def issue(idx_ref, slot):
        def body(g, carry):
            for j in range(8):
                e = idx_ref[g * 8 + j]
                pltpu.make_async_copy(
                    tab_hbm.at[pl.ds(e, 1), :], gbuf.at[slot, g, pl.ds(j, 1), :], sems.at[slot]).start()
            return carry
        lax.fori_loop(0, TC_TOKENS * groups, body, 0)

    @pl.when(i == 0)
    def _():
        issue(idx_cur, 0)

    @pl.when(i + 1 < n)
    def _():
        issue(idx_nxt, (i + 1) % 2)

    slot = i % 2
    pltpu.make_async_copy(gbuf.at[slot], gbuf.at[slot], sems.at[slot]).wait()

    def rows(t, c0):
        return gbuf[slot, t * groups:(t + 1) * groups, :, c0:c0 + D_MODEL].reshape(N_SEL, D_MODEL)

    lane = lax.broadcasted_iota(jnp.int32, (N_SEL, 128), 1)
    pre_all = jnp.zeros((N_SEL, 128), jnp.float32)
    for t in range(TC_TOKENS):
        prod = rows(t, 0) * h_ref[t:t + 1, :]
        part = prod[:, 0:128]
        for c in range(1, D_MODEL // 128):
            part = part + prod[:, c * 128:(c + 1) * 128]
        pre = jnp.sum(part, axis=1, keepdims=True)
        pre_all = jnp.where(lane == t, pre, pre_all)
    act = 0.5 * pre_all * (1.0 + lax.erf(pre_all * (1.0 / math.sqrt(2.0))))
    gate_t = jnp.concatenate([gate_ref[...]] * (128 // TC_TOKENS), axis=0).T
    wgt = gate_t * act
    for t in range(TC_TOKENS):
        w_col = jnp.sum(jnp.where(lane == t, wgt, 0.0), axis=1, keepdims=True)
        up = rows(t, D_MODEL) * w_col
        o_ref[t:t + 1, :] = jnp.sum(up, axis=0, keepdims=True)


def _peer_tc(h2, idx, gate, table2, row0, n_tokens):
    D = h2.shape[1]
    steps = n_tokens // TC_TOKENS
    blk0 = row0 // TC_TOKENS
    idx_flat = idx.reshape(-1)
    return pl.pallas_call(
        _peer_tc_kernel,
        grid=(steps,),
        in_specs=[
            pl.BlockSpec((TC_TOKENS * N_SEL,), lambda i: (blk0 + i,), memory_space=pltpu.SMEM),
            pl.BlockSpec((TC_TOKENS * N_SEL,), lambda i: (blk0 + jnp.minimum(i + 1, steps - 1),),
                         memory_space=pltpu.SMEM),
            pl.BlockSpec((TC_TOKENS, D), lambda i: (blk0 + i, 0)),
            pl.BlockSpec((TC_TOKENS, N_SEL), lambda i: (blk0 + i, 0)),
            pl.BlockSpec(memory_space=pl.ANY),
        ],
        out_specs=pl.BlockSpec((TC_TOKENS, D), lambda i: (i, 0)),
        out_shape=jax.ShapeDtypeStruct((n_tokens, D), jnp.float32),
        scratch_shapes=[
            pltpu.VMEM((2, TC_TOKENS * N_SEL // 8, 8, 2 * D), jnp.float32),
            pltpu.SemaphoreType.DMA((2,)),
        ],
        compiler_params=pltpu.CompilerParams(
            dimension_semantics=("arbitrary",), vmem_limit_bytes=VMEM_LIMIT_BYTES),
        name="peer_tc",
    )(idx_flat, idx_flat, h2, gate, table2)


def _gate_act_kernel(pre_ref, gate_ref, o_ref):
    p = pre_ref[...]
    act = 0.5 * p * (1.0 + lax.erf(p * (1.0 / math.sqrt(2.0))))
    o_ref[...] = gate_ref[...] * act


def _gate_act(pre, gate):
    N = pre.shape[0]
    spec = pl.BlockSpec((EW_BLOCK, N_SEL), lambda i: (i, 0))
    return pl.pallas_call(
        _gate_act_kernel, grid=(N // EW_BLOCK,), in_specs=[spec, spec], out_specs=spec,
        out_shape=jax.ShapeDtypeStruct((N, N_SEL), jnp.float32), name="gate_act",
    )(pre, gate)


def _final_kernel(x1_ref, mod_ref, g_ref, *refs, normalize, starts):
    peer_refs, o_ref = refs[:-1], refs[-1]
    gate2 = mod_ref[0, 5:6, :]
    peer = peer_refs[0][...]
    for p in range(1, len(peer_refs)):
        peer = jnp.where(pl.program_id(0) >= starts[p], peer_refs[p][...], peer)
    y = x1_ref[0] + gate2 * peer
    o_ref[0] = _rms(y, g_ref[...]) if normalize else y


def _residual(x1, peers, b, mod3, g_final, normalize):
    _, S, D = x1.shape
    blocks = [p.shape[0] // EW_BLOCK for p in peers]
    starts = [sum(blocks[:p]) for p in range(len(peers))]
    tok = pl.BlockSpec((1, EW_BLOCK, D), lambda j: (0, j, 0))

    def piece_spec(start, n):
        return pl.BlockSpec((EW_BLOCK, D), lambda j: (jnp.clip(j - start, 0, n - 1), 0))

    return pl.pallas_call(
        functools.partial(_final_kernel, normalize=normalize, starts=tuple(starts)),
        grid=(S // EW_BLOCK,),
        in_specs=[tok,
                  pl.BlockSpec((1, N_MOD, D), lambda j: (b, 0, 0)),
                  pl.BlockSpec((1, D), lambda j: (0, 0))]
                 + [piece_spec(s, n) for s, n in zip(starts, blocks)],
        out_specs=tok,
        out_shape=jax.ShapeDtypeStruct((1, S, D), jnp.float32),
        compiler_params=pltpu.CompilerParams(vmem_limit_bytes=VMEM_LIMIT_BYTES),
        name="final_norm",
    )(x1, mod3, g_final, *peers)


def _token_split(b, n_seq, S):
    unit = S // 16
    if b == n_seq - 1 and n_seq > 1:
        sc, c = 10, 2
    elif b == 0:
        sc, c = 13, 0
    else:
        sc, c = (11 if b % 2 else 12), 0
    n_sc, n_c = sc * unit, c * unit
    n_a = ((S - n_sc - n_c) // 2) // EW_BLOCK * EW_BLOCK
    return n_sc, n_a, S - n_sc - n_c - n_a, n_c


def kernel(x, c, w_ada, b_ada, g_norm1, w_in, w_pool, pool_scale, sgu_ln_g, sgu_ln_b, w_spatial, b_spatial,
           w_out, g_norm2, w_query, sub_keys, expert_down, expert_up, g_final):
    B, S, D = x.shape
    depth = w_ada.shape[0]
    bf16 = jnp.bfloat16
    band = jnp.asarray(_pool_band(), bf16)
    c_pad = jnp.pad(c, ((0, 8 - B), (0, 0)))
    for l in range(depth):
        mod = _ada_mod(c_pad, w_ada[l].astype(bf16), b_ada[l][None, :])[:B]
        mod3 = mod.reshape(B, N_MOD, D)
        mix_w = (g_norm1[l][None, :], w_in[l].astype(bf16), band, w_pool[l].astype(bf16),
                 pool_scale[l][None, :], sgu_ln_g[l][None, :], sgu_ln_b[l][None, :], w_spatial[l],
                 b_spatial[l].T, w_out[l].astype(bf16), g_norm2[l][None, :])
        keys = sub_keys[l].reshape(2 * PEER_HEADS, N_KEYS, HALF_KEY).astype(bf16)
        wq = w_query[l].astype(bf16)
        table2 = jnp.concatenate([expert_down[l], expert_up[l]], axis=1)
        down_packed = _pack_rows(expert_down[l])
        up_packed = _pack_rows(expert_up[l])
        last = l + 1 == depth
        outs = []
        pending = None
        mod3_b = mod3
        for b in range(B):
            n_sc, n_a, n_b, n_c = _token_split(b, B, S)
            x1, h2 = _mixer(x, b, 1, mod3_b, *mix_w)
            h2f = h2.reshape(S, D)
            idx, gate = _query_topk(h2f, wq, keys)
            tc_a = _peer_tc(h2f, idx, gate, table2, n_sc, n_a)
            pre = _peer_down(h2f, idx, down_packed, n_sc)
            gate_b = gate
            if pending is not None:
                p_x1, p_peers, p_seq = pending
                p_peers[0], tc_a = lax.optimization_barrier((p_peers[0], tc_a))
                out_prev = _residual(p_x1, p_peers, p_seq, mod3, g_final[None, :], last)
                outs.append(out_prev)
                gate_b, _ = lax.optimization_barrier((gate, out_prev))
            tc_b = _peer_tc(h2f, idx, gate_b, table2, n_sc + n_a, n_b)
            pre, tc_a, tc_b = lax.optimization_barrier((pre, tc_a, tc_b))
            wgt = _gate_act(pre, gate)
            peers = [_peer_up(wgt, idx, up_packed, n_sc), tc_a, tc_b]
            if n_c:
                gate_c, _ = lax.optimization_barrier((gate, wgt))
                peers.append(_peer_tc(h2f, idx, gate_c, table2, n_sc + n_a + n_b, n_c))
            mod3_b, _ = lax.optimization_barrier((mod3, wgt))
            pending = (x1, peers, b)
        p_x1, p_peers, p_seq = pending
        outs.append(_residual(p_x1, p_peers, p_seq, mod3, g_final[None, :], last))
        x = jnp.concatenate(outs, axis=0)
    return x
```

```python
import functools
import math

import jax
import jax.numpy as jnp
import numpy as np
from jax import lax
from jax.experimental import pallas as pl
from jax.experimental.pallas import tpu as pltpu
from jax.experimental.pallas import tpu_sc as plsc

D_MODEL = 1024
POOL_WIDTH = 512
SGU_WIDTH = 512
POOL_WINDOWS = (2, 4, 8, 16)
GROUP_DIM = 128
CHUNK = 128
SGU_HEADS = 4
IN_PROJ_WIDTH = POOL_WIDTH + 2 * SGU_WIDTH
PEER_HEADS = 8
N_KEYS = 128
HALF_KEY = 128
TOPK = 16
N_MOD = 6
EPS = 1e-6

MIX_BLOCK = 512
TOPK_BLOCK = 256
EW_BLOCK = 256
VMEM_LIMIT_BYTES = 48 * 1024 * 1024

SC_LANES = 16
SC_TOKENS = 32
SC_RING = 8
N_SEL = PEER_HEADS * TOPK
TC_TOKENS = 8

N_CAND = 16 + 7 * 8 + 8


def _pool_band():
    t = np.arange(CHUNK)[:, None]
    j = np.arange(2 * CHUNK)[None, :]
    bands = []
    for w in POOL_WINDOWS:
        m = (j > CHUNK + t - w) & (j <= CHUNK + t)
        bands.append(np.where(m, 1.0 / w, 0.0))
    return np.stack(bands).astype(np.float32)


def _bf16_dot(a, b):
    return jnp.dot(a.astype(jnp.bfloat16), b.astype(jnp.bfloat16), preferred_element_type=jnp.float32)


def _ada_kernel(c_ref, w_ref, b_ref, o_ref):
    c = c_ref[...]
    c_act = c * jax.nn.sigmoid(c)
    o_ref[...] = _bf16_dot(c_act, w_ref[...]) + b_ref[...]


def _ada_mod(c_pad, w_ada, b_ada):
    rows = c_pad.shape[0]
    return pl.pallas_call(
        _ada_kernel,
        grid=(N_MOD,),
        in_specs=[
            pl.BlockSpec((rows, D_MODEL), lambda i: (0, 0)),
            pl.BlockSpec((D_MODEL, D_MODEL), lambda i: (0, i)),
            pl.BlockSpec((1, D_MODEL), lambda i: (0, i)),
        ],
        out_specs=pl.BlockSpec((rows, D_MODEL), lambda i: (0, i)),
        out_shape=jax.ShapeDtypeStruct((rows, N_MOD * D_MODEL), jnp.float32),
        name="ada_mod",
    )(c_pad, w_ada, b_ada)


def _rms(x, g):
    return x * lax.rsqrt(jnp.mean(x * x, axis=-1, keepdims=True) + EPS) * g


def _mix_kernel(x_ref, mod_ref, g1_ref, win_ref, band_ref, wpool_ref, pscale_ref, lng_ref, lnb_ref,
                wsp_ref, bsp_ref, wout_ref, g2_ref, x1_ref, h2_ref, aext_ref, mixed_ref):
    j = pl.program_id(1)
    x = x_ref[0]
    shift1, scale1, gate1 = mod_ref[0, 0:1, :], mod_ref[0, 1:2, :], mod_ref[0, 2:3, :]
    shift2, scale2 = mod_ref[0, 3:4, :], mod_ref[0, 4:5, :]

    h = _rms(x, g1_ref[...]) * (1.0 + scale1) + shift1
    proj = _bf16_dot(h, win_ref[...])
    a = proj[:, :POOL_WIDTH]
    u = proj[:, POOL_WIDTH:POOL_WIDTH + SGU_WIDTH]
    v = proj[:, POOL_WIDTH + SGU_WIDTH:]

    @pl.when(j == 0)
    def _():
        aext_ref[0:CHUNK, :] = jnp.zeros((CHUNK, POOL_WIDTH), jnp.float32)

    aext_ref[CHUNK:, :] = a

    mu = jnp.mean(v, axis=-1, keepdims=True)
    vc = v - mu
    vn = vc * lax.rsqrt(jnp.mean(vc * vc, axis=-1, keepdims=True) + EPS) * lng_ref[...] + lnb_ref[...]

    row = lax.broadcasted_iota(jnp.int32, (CHUNK, CHUNK), 0)
    col = lax.broadcasted_iota(jnp.int32, (CHUNK, CHUNK), 1)
    tril = col <= row
    t_col = lax.broadcasted_iota(jnp.int32, (CHUNK, 1), 0)

    for c in range(MIX_BLOCK // CHUNK):
        r0 = c * CHUNK
        pos = (j * MIX_BLOCK + r0 + 1 + t_col).astype(jnp.float32)
        for g, w in enumerate(POOL_WINDOWS):
            c0 = g * GROUP_DIM
            seg = aext_ref[r0:r0 + 2 * CHUNK, c0:c0 + GROUP_DIM]
            hi = seg.astype(jnp.bfloat16)
            lo = (seg - hi.astype(jnp.float32)).astype(jnp.bfloat16)
            band = band_ref[g]
            win = (jnp.dot(band, hi, preferred_element_type=jnp.float32)
                   + jnp.dot(band, lo, preferred_element_type=jnp.float32))
            mean = win * (float(w) / jnp.minimum(pos, float(w)))
            pooled = mean - seg[CHUNK:, :]
            y = _bf16_dot(pooled, wpool_ref[g]) * pscale_ref[:, c0:c0 + GROUP_DIM]
            mixed_ref[r0:r0 + CHUNK, c0:c0 + GROUP_DIM] = y.astype(jnp.bfloat16)
        for hh in range(SGU_HEADS):
            c0 = hh * GROUP_DIM
            wm = jnp.where(tril, wsp_ref[hh], 0.0)
            m = _bf16_dot(wm, vn[r0:r0 + CHUNK, c0:c0 + GROUP_DIM]) + bsp_ref[:, hh:hh + 1]
            s = u[r0:r0 + CHUNK, c0:c0 + GROUP_DIM] * m
            mixed_ref[r0:r0 + CHUNK, POOL_WIDTH + c0:POOL_WIDTH + c0 + GROUP_DIM] = s.astype(jnp.bfloat16)

    aext_ref[0:CHUNK, :] = aext_ref[MIX_BLOCK:MIX_BLOCK + CHUNK, :]

    x1 = x + gate1 * jnp.dot(mixed_ref[...], wout_ref[...], preferred_element_type=jnp.float32)
    x1_ref[0] = x1
    h2_ref[0] = _rms(x1, g2_ref[...]) * (1.0 + scale2) + shift2


def _mixer(x, b0, n_seq, mod3, g1, w_in, band, w_pool, pool_scale, ln_g, ln_b, w_sp, b_sp_t, w_out, g2):
    _, S, D = x.shape
    const2 = lambda b, j: (0, 0)
    const3 = lambda b, j: (0, 0, 0)
    tok = pl.BlockSpec((1, MIX_BLOCK, D), lambda b, j: (b, j, 0))
    return pl.pallas_call(
        _mix_kernel,
        grid=(n_seq, S // MIX_BLOCK),
        in_specs=[
            pl.BlockSpec((1, MIX_BLOCK, D), lambda b, j: (b0 + b, j, 0)),
            pl.BlockSpec((1, N_MOD, D), lambda b, j: (b0 + b, 0, 0)),
            pl.BlockSpec((1, D), const2),
            pl.BlockSpec((D, IN_PROJ_WIDTH), const2),
            pl.BlockSpec((len(POOL_WINDOWS), CHUNK, 2 * CHUNK), const3),
            pl.BlockSpec((len(POOL_WINDOWS), GROUP_DIM, GROUP_DIM), const3),
            pl.BlockSpec((1, POOL_WIDTH), const2),
            pl.BlockSpec((1, SGU_WIDTH), const2),
            pl.BlockSpec((1, SGU_WIDTH), const2),
            pl.BlockSpec((SGU_HEADS, CHUNK, CHUNK), const3),
            pl.BlockSpec((CHUNK, SGU_HEADS), const2),
            pl.BlockSpec((D, D), const2),
            pl.BlockSpec((1, D), const2),
        ],
        out_specs=[tok, tok],
        out_shape=[jax.ShapeDtypeStruct((n_seq, S, D), jnp.float32)] * 2,
        scratch_shapes=[
            pltpu.VMEM((MIX_BLOCK + CHUNK, POOL_WIDTH), jnp.float32),
            pltpu.VMEM((MIX_BLOCK, D), jnp.bfloat16),
        ],
        compiler_params=pltpu.CompilerParams(
            dimension_semantics=("arbitrary", "arbitrary"), vmem_limit_bytes=VMEM_LIMIT_BYTES),
        name="mixer",
    )(x, mod3, g1, w_in, band, w_pool, pool_scale, ln_g, ln_b, w_sp, b_sp_t, w_out, g2)


def _extract_top(s, ids, payload, n_out):
    vals, pays = [], []
    for _ in range(n_out):
        m = jnp.max(s, axis=0, keepdims=True)
        pick = jnp.min(jnp.where(s == m, ids, jnp.float32(1e9)), axis=0, keepdims=True)
        sel = ids == pick
        vals.append(m)
        pays.append(pick if payload is None
                    else jnp.max(jnp.where(sel, payload, -1.0), axis=0, keepdims=True))
        s = jnp.where(sel, -jnp.inf, s)
    return jnp.concatenate(vals, axis=0), jnp.concatenate(pays, axis=0)


def _topk_kernel(h2_ref, wq_ref, keys_ref, idx_ref, gate_ref):
    T = TOPK_BLOCK
    q = _bf16_dot(h2_ref[...], wq_ref[...])
    key_ids = lax.broadcasted_iota(jnp.int32, (N_KEYS, T), 0).astype(jnp.float32)
    row = lax.broadcasted_iota(jnp.int32, (N_CAND, T), 0)
    flat = jnp.where(row < 16, row,
                     jnp.where(row < 72, jnp.right_shift(row - 8, 3) * 16 + jnp.bitwise_and(row, 7),
                               (row - 64) * 16)).astype(jnp.float32)
    idx_rows, gate_rows = [], []
    for hd in range(PEER_HEADS):
        tops = []
        for p in range(2):
            hp = hd * 2 + p
            qs = q[:, hp * HALF_KEY:(hp + 1) * HALF_KEY].astype(jnp.bfloat16)
            st = lax.dot_general(keys_ref[hp], qs, (((1,), (1,)), ((), ())),
                                 preferred_element_type=jnp.float32)
            tops.append(_extract_top(st, key_ids, None, TOPK))
        (s0, i0), (s1, i1) = tops
        i0 = i0 * float(N_KEYS)
        cand = jnp.concatenate(
            [s0[0:1] + s1] + [s0[a:a + 1] + s1[0:8] for a in range(1, 8)] + [s0[8:16] + s1[0:1]], axis=0)
        cidx = jnp.concatenate(
            [i0[0:1] + i1] + [i0[a:a + 1] + i1[0:8] for a in range(1, 8)] + [i0[8:16] + i1[0:1]],
            axis=0)
        best, sel_idx = _extract_top(cand, flat, cidx, TOPK)
        e = jnp.exp(best - jnp.max(best, axis=0, keepdims=True))
        gate_rows.append(e / jnp.sum(e, axis=0, keepdims=True))
        idx_rows.append(sel_idx)
    idx_ref[...] = jnp.concatenate(idx_rows, axis=0).T.astype(jnp.int32)
    gate_ref[...] = jnp.concatenate(gate_rows, axis=0).T


def _query_topk(h2, w_query, keys):
    N, D = h2.shape
    tok = pl.BlockSpec((TOPK_BLOCK, N_SEL), lambda i: (i, 0))
    return pl.pallas_call(
        _topk_kernel,
        grid=(N // TOPK_BLOCK,),
        in_specs=[
            pl.BlockSpec((TOPK_BLOCK, D), lambda i: (i, 0)),
            pl.BlockSpec((D, 2 * PEER_HEADS * HALF_KEY), lambda i: (0, 0)),
            pl.BlockSpec((2 * PEER_HEADS, N_KEYS, HALF_KEY), lambda i: (0, 0, 0)),
        ],
        out_specs=[tok, tok],
        out_shape=[jax.ShapeDtypeStruct((N, N_SEL), jnp.int32), jax.ShapeDtypeStruct((N, N_SEL), jnp.float32)],
        compiler_params=pltpu.CompilerParams(
            dimension_semantics=("arbitrary",), vmem_limit_bytes=VMEM_LIMIT_BYTES),
        name="query_topk",
    )(h2, w_query, keys)


def _sc_mesh_and_workers():
    info = plsc.get_sparse_core_info()
    assert info.num_lanes == SC_LANES
    mesh = plsc.VectorSubcoreMesh(core_axis_name="core", subcore_axis_name="subcore")
    return mesh, info.num_cores, info.num_cores * info.num_subcores


def _gather_ring(tab_hbm, idx_v, rows_v, sems, consume, n_tokens):
    look = SC_RING - 1

    def copy(t, hd):
        slot = hd % SC_RING
        return pltpu.make_async_copy(
            tab_hbm.at[idx_v.at[t, pl.ds(hd * TOPK, TOPK)]], rows_v.at[slot], sems.at[slot])

    for hd in range(look):
        copy(0, hd).start()

    @pl.loop(0, n_tokens)
    def _(t):
        for hd in range(PEER_HEADS):
            nxt = hd + look
            if nxt < PEER_HEADS:
                copy(t, nxt).start()
            else:
                @pl.when(t + 1 < n_tokens)
                def _():
                    copy(t + 1, nxt - PEER_HEADS).start()
            copy(t, hd).wait()
            consume(t, hd, rows_v.at[hd % SC_RING])


def _for_token_blocks(first, per_worker, block):
    n_full, tail = divmod(per_worker, SC_TOKENS)
    assert tail % 8 == 0, "HBM row offsets of the staged slices must stay 8-aligned"

    @pl.loop(0, n_full)
    def _(blk):
        block(first + blk * SC_TOKENS, SC_TOKENS)

    if tail:
        block(first + n_full * SC_TOKENS, tail)


SC_PAIR_WORDS = D_MODEL // 2
SC_STEPS = SC_PAIR_WORDS // SC_LANES
SC_BF16_SUM = 4


def _bf16_bits(a):
    return lax.bitcast_convert_type(a.astype(jnp.bfloat16), jnp.uint16).astype(jnp.uint32)


def _pack_rows(a):
    bits = _bf16_bits(a)
    return lax.bitcast_convert_type(bits[:, :SC_PAIR_WORDS] | (bits[:, SC_PAIR_WORDS:] << 16), jnp.int32)


def _pack_dup(a):
    bits = _bf16_bits(a)
    return lax.bitcast_convert_type(bits | (bits << 16), jnp.int32)


def _unpack_pair(words):
    lo = lax.bitcast_convert_type(lax.shift_left(words, jnp.int32(16)), jnp.float32)
    hi = lax.bitcast_convert_type(lax.bitwise_and(words, jnp.int32(-65536)), jnp.float32)
    return lo, hi


def _as_bf16(words):
    return plsc.bitcast(words, jnp.bfloat16)


def _widen_sum(products):
    return _unpack_pair(plsc.bitcast(_tree_sum(products), jnp.int32))


def _tree_sum(terms):
    terms = list(terms)
    while len(terms) > 1:
        terms = [terms[i] + terms[i + 1] for i in range(0, len(terms) - 1, 2)] + (
            [terms[-1]] if len(terms) % 2 else [])
    return terms[0]


def _peer_down(h_packed, idx, packed, N):
    mesh, n_cores, n_workers = _sc_mesh_and_workers()
    per_worker = N // n_workers

    def body(h_hbm, idx_hbm, tab_hbm, out_hbm, idx_v, h_v, rows_v, out_v, tr_v, sems):
        wid = lax.axis_index("subcore") * n_cores + lax.axis_index("core")
        lane = lax.iota(jnp.int32, SC_LANES)

        def consume(t, hd, rows):
            def dot_step(g, accs):
                runs = [pl.ds((g * SC_BF16_SUM + i) * SC_LANES, SC_LANES) for i in range(SC_BF16_SUM)]
                hb = [_as_bf16(h_v[t, r]) for r in runs]
                out = []
                for k, acc in enumerate(accs):
                    lo, hi = _widen_sum([_as_bf16(rows[k, r]) * hbi for r, hbi in zip(runs, hb)])
                    out.append(acc + (lo + hi))
                return tuple(out)

            accs = lax.fori_loop(0, SC_STEPS // SC_BF16_SUM, dot_step,
                                 tuple(jnp.zeros((SC_LANES,), jnp.float32) for _ in range(TOPK)))
            for k in range(TOPK):
                tr_v[k, :] = accs[k]
            tot = jnp.zeros((SC_LANES,), jnp.float32)
            for l in range(SC_LANES):
                tot = tot + plsc.load_gather(tr_v, [lane, jnp.full((SC_LANES,), l, jnp.int32)])
            out_v[t, pl.ds(hd * TOPK, TOPK)] = tot

        def block(base, n):
            pltpu.sync_copy(idx_hbm.at[pl.ds(base, n)], idx_v.at[pl.ds(0, n)])
            pltpu.sync_copy(h_hbm.at[pl.ds(base, n)], h_v.at[pl.ds(0, n)])
            _gather_ring(tab_hbm, idx_v, rows_v, sems, consume, n)
            pltpu.sync_copy(out_v.at[pl.ds(0, n)], out_hbm.at[pl.ds(base, n)])

        _for_token_blocks(wid * per_worker, per_worker, block)

    return pl.kernel(
        body,
        out_type=jax.ShapeDtypeStruct((N, N_SEL), jnp.float32),
        mesh=mesh,
        scratch_types=[
            pltpu.VMEM((SC_TOKENS, N_SEL), jnp.int32),
            pltpu.VMEM((SC_TOKENS, SC_PAIR_WORDS), jnp.int32),
            pltpu.VMEM((SC_RING, TOPK, SC_PAIR_WORDS), jnp.int32),
            pltpu.VMEM((SC_TOKENS, N_SEL), jnp.float32),
            pltpu.VMEM((TOPK, SC_LANES), jnp.float32),
            pltpu.SemaphoreType.DMA((SC_RING,)),
        ],
        compiler_params=pltpu.CompilerParams(needs_layout_passes=False),
        name="peer_down",
    )(h_packed, idx, packed)


def _peer_up(w_dup, idx, packed, N):
    D = D_MODEL
    mesh, n_cores, n_workers = _sc_mesh_and_workers()
    per_worker = N // n_workers

    def body(w_hbm, idx_hbm, tab_hbm, out_hbm, idx_v, w_v, rows_v, out_v, sems):
        wid = lax.axis_index("subcore") * n_cores + lax.axis_index("core")

        def consume(t, hd, rows):
            t_vec = jnp.full((SC_LANES,), t, jnp.int32)
            ws = [_as_bf16(plsc.load_gather(w_v, [t_vec, jnp.full((SC_LANES,), hd * TOPK + k, jnp.int32)]))
                  for k in range(TOPK)]

            @plsc.parallel_loop(0, SC_STEPS)
            def _(s):
                run = pl.ds(s * SC_LANES, SC_LANES)
                parts = [_widen_sum([_as_bf16(rows[k, run]) * ws[k] for k in range(g, g + SC_BF16_SUM)])
                         for g in range(0, TOPK, SC_BF16_SUM)]
                for half, c in enumerate((s * SC_LANES, SC_PAIR_WORDS + s * SC_LANES)):
                    sl = pl.ds(c, SC_LANES)
                    terms = [p[half] for p in parts]
                    if hd > 0:
                        terms.append(out_v[t, sl])
                    out_v[t, sl] = _tree_sum(terms)

        def block(base, n):
            pltpu.sync_copy(idx_hbm.at[pl.ds(base, n)], idx_v.at[pl.ds(0, n)])
            pltpu.sync_copy(w_hbm.at[pl.ds(base, n)], w_v.at[pl.ds(0, n)])
            _gather_ring(tab_hbm, idx_v, rows_v, sems, consume, n)
            pltpu.sync_copy(out_v.at[pl.ds(0, n)], out_hbm.at[pl.ds(base, n)])

        _for_token_blocks(wid * per_worker, per_worker, block)

    return pl.kernel(
        body,
        out_type=jax.ShapeDtypeStruct((N, D), jnp.float32),
        mesh=mesh,
        scratch_types=[
            pltpu.VMEM((SC_TOKENS, N_SEL), jnp.int32),
            pltpu.VMEM((SC_TOKENS, N_SEL), jnp.int32),
            pltpu.VMEM((SC_RING, TOPK, SC_PAIR_WORDS), jnp.int32),
            pltpu.VMEM((SC_TOKENS, D), jnp.float32),
            pltpu.SemaphoreType.DMA((SC_RING,)),
        ],
        compiler_params=pltpu.CompilerParams(needs_layout_passes=False),
        name="peer_up",
    )(w_dup, idx, packed)


def _peer_tc_kernel(idx_cur, idx_nxt, h_ref, gate_ref, tab_hbm, o_ref, gbuf, sems):
    i = pl.program_id(0)
    n = pl.num_programs(0)
    groups = N_SEL // 8

    def issue(idx_ref, slot):
        def body(g, carry):
            for j in range(8):
                e = idx_ref[g * 8 + j]
                pltpu.make_async_copy(
                    tab_hbm.at[pl.ds(e, 1), :], gbuf.at[slot, g, pl.ds(j, 1), :], sems.at[slot]).start()
            return carry
        lax.fori_loop(0, TC_TOKENS * groups, body, 0)

    @pl.when(i == 0)
    def _():
        issue(idx_cur, 0)

    @pl.when(i + 1 < n)
    def _():
        issue(idx_nxt, (i + 1) % 2)

    slot = i % 2
    pltpu.make_async_copy(gbuf.at[slot], gbuf.at[slot], sems.at[slot]).wait()

    def rows(t, c0):
        return gbuf[slot, t * groups:(t + 1) * groups, :, c0:c0 + D_MODEL].reshape(N_SEL, D_MODEL)

    lane = lax.broadcasted_iota(jnp.int32, (N_SEL, 128), 1)
    pre_all = jnp.zeros((N_SEL, 128), jnp.float32)
    for t in range(TC_TOKENS):
        prod = rows(t, 0) * h_ref[t:t + 1, :]
        part = prod[:, 0:128]
        for c in range(1, D_MODEL // 128):
            part = part + prod[:, c * 128:(c + 1) * 128]
        pre = jnp.sum(part, axis=1, keepdims=True)
        pre_all = jnp.where(lane == t, pre, pre_all)
    act = 0.5 * pre_all * (1.0 + lax.erf(pre_all * (1.0 / math.sqrt(2.0))))
    gate_t = jnp.concatenate([gate_ref[...]] * (128 // TC_TOKENS), axis=0).T
    wgt = gate_t * act
    for t in range(TC_TOKENS):
        w_col = jnp.sum(jnp.where(lane == t, wgt, 0.0), axis=1, keepdims=True)
        up = rows(t, D_MODEL) * w_col
        o_ref[t:t + 1, :] = jnp.sum(up, axis=0, keepdims=True)


def _peer_tc(h2, idx, gate, table2, row0, n_tokens):
    D = h2.shape[1]
    steps = n_tokens // TC_TOKENS
    blk0 = row0 // TC_TOKENS
    idx_flat = idx.reshape(-1)
    return pl.pallas_call(
        _peer_tc_kernel,
        grid=(steps,),
        in_specs=[
            pl.BlockSpec((TC_TOKENS * N_SEL,), lambda i: (blk0 + i,), memory_space=pltpu.SMEM),
            pl.BlockSpec((TC_TOKENS * N_SEL,), lambda i: (blk0 + jnp.minimum(i + 1, steps - 1),),
                         memory_space=pltpu.SMEM),
            pl.BlockSpec((TC_TOKENS, D), lambda i: (blk0 + i, 0)),
            pl.BlockSpec((TC_TOKENS, N_SEL), lambda i: (blk0 + i, 0)),
            pl.BlockSpec(memory_space=pl.ANY),
        ],
        out_specs=pl.BlockSpec((TC_TOKENS, D), lambda i: (i, 0)),
        out_shape=jax.ShapeDtypeStruct((n_tokens, D), jnp.float32),
        scratch_shapes=[
            pltpu.VMEM((2, TC_TOKENS * N_SEL // 8, 8, 2 * D), jnp.float32),
            pltpu.SemaphoreType.DMA((2,)),
        ],
        compiler_params=pltpu.CompilerParams(
            dimension_semantics=("arbitrary",), vmem_limit_bytes=VMEM_LIMIT_BYTES),
        name="peer_tc",
    )(idx_flat, idx_flat, h2, gate, table2)


def _gate_act_kernel(pre_ref, gate_ref, o_ref):
    p = pre_ref[...]
    act = 0.5 * p * (1.0 + lax.erf(p * (1.0 / math.sqrt(2.0))))
    o_ref[...] = gate_ref[...] * act


def _gate_act(pre, gate):
    N = pre.shape[0]
    spec = pl.BlockSpec((EW_BLOCK, N_SEL), lambda i: (i, 0))
    return pl.pallas_call(
        _gate_act_kernel, grid=(N // EW_BLOCK,), in_specs=[spec, spec], out_specs=spec,
        out_shape=jax.ShapeDtypeStruct((N, N_SEL), jnp.float32), name="gate_act",
    )(pre, gate)


def _final_kernel(x1_ref, mod_ref, g_ref, *refs, normalize, starts):
    peer_refs, o_ref = refs[:-1], refs[-1]
    gate2 = mod_ref[0, 5:6, :]
    peer = peer_refs[0][...]
    for p in range(1, len(peer_refs)):
        peer = jnp.where(pl.program_id(0) >= starts[p], peer_refs[p][...], peer)
    y = x1_ref[0] + gate2 * peer
    o_ref[0] = _rms(y, g_ref[...]) if normalize else y


def _residual(x1, peers, b, mod3, g_final, normalize):
    _, S, D = x1.shape
    blocks = [p.shape[0] // EW_BLOCK for p in peers]
    starts = [sum(blocks[:p]) for p in range(len(peers))]
    tok = pl.BlockSpec((1, EW_BLOCK, D), lambda j: (0, j, 0))

    def piece_spec(start, n):
        return pl.BlockSpec((EW_BLOCK, D), lambda j: (jnp.clip(j - start, 0, n - 1), 0))

    return pl.pallas_call(
        functools.partial(_final_kernel, normalize=normalize, starts=tuple(starts)),
        grid=(S // EW_BLOCK,),
        in_specs=[tok,
                  pl.BlockSpec((1, N_MOD, D), lambda j: (b, 0, 0)),
                  pl.BlockSpec((1, D), lambda j: (0, 0))]
                 + [piece_spec(s, n) for s, n in zip(starts, blocks)],
        out_specs=tok,
        out_shape=jax.ShapeDtypeStruct((1, S, D), jnp.float32),
        compiler_params=pltpu.CompilerParams(vmem_limit_bytes=VMEM_LIMIT_BYTES),
        name="final_norm",
    )(x1, mod3, g_final, *peers)


def _token_split(b, n_seq, S):
    unit = S // 16
    if b == n_seq - 1 and n_seq > 1:
        sc, c = 10, 2
    elif b == 0:
        sc, c = 13, 0
    else:
        sc, c = (11 if b % 2 else 12), 0
    n_sc, n_c = sc * unit, c * unit
    n_a = ((S - n_sc - n_c) // 2) // EW_BLOCK * EW_BLOCK
    return n_sc, n_a, S - n_sc - n_c - n_a, n_c


def kernel(x, c, w_ada, b_ada, g_norm1, w_in, w_pool, pool_scale, sgu_ln_g, sgu_ln_b, w_spatial, b_spatial,
           w_out, g_norm2, w_query, sub_keys, expert_down, expert_up, g_final):
    B, S, D = x.shape
    depth = w_ada.shape[0]
    bf16 = jnp.bfloat16
    band = jnp.asarray(_pool_band(), bf16)
    c_pad = jnp.pad(c, ((0, 8 - B), (0, 0)))
    for l in range(depth):
        mod = _ada_mod(c_pad, w_ada[l].astype(bf16), b_ada[l][None, :])[:B]
        mod3 = mod.reshape(B, N_MOD, D)
        mix_w = (g_norm1[l][None, :], w_in[l].astype(bf16), band, w_pool[l].astype(bf16),
                 pool_scale[l][None, :], sgu_ln_g[l][None, :], sgu_ln_b[l][None, :], w_spatial[l],
                 b_spatial[l].T, w_out[l].astype(bf16), g_norm2[l][None, :])
        keys = sub_keys[l].reshape(2 * PEER_HEADS, N_KEYS, HALF_KEY).astype(bf16)
        wq = w_query[l].astype(bf16)
        table2 = jnp.concatenate([expert_down[l], expert_up[l]], axis=1)
        down_packed = _pack_rows(expert_down[l])
        up_packed = _pack_rows(expert_up[l])
        last = l + 1 == depth
        outs = []
        pending = None
        mod3_b = mod3
        for b in range(B):
            n_sc, n_a, n_b, n_c = _token_split(b, B, S)
            x1, h2 = _mixer(x, b, 1, mod3_b, *mix_w)
            h2f = h2.reshape(S, D)
            idx, gate = _query_topk(h2f, wq, keys)
            tc_a = _peer_tc(h2f, idx, gate, table2, n_sc, n_a)
            pre = _peer_down(_pack_rows(h2f[:n_sc]), idx, down_packed, n_sc)
            gate_b = gate
            if pending is not None:
                p_x1, p_peers, p_seq = pending
                p_peers[0], tc_a = lax.optimization_barrier((p_peers[0], tc_a))
                out_prev = _residual(p_x1, p_peers, p_seq, mod3, g_final[None, :], last)
                outs.append(out_prev)
                gate_b, _ = lax.optimization_barrier((gate, out_prev))
            tc_b = _peer_tc(h2f, idx, gate_b, table2, n_sc + n_a, n_b)
            pre, tc_a, tc_b = lax.optimization_barrier((pre, tc_a, tc_b))
            wgt = _gate_act(pre, gate)
            peers = [_peer_up(_pack_dup(wgt), idx, up_packed, n_sc), tc_a, tc_b]
            if n_c:
                gate_c, _ = lax.optimization_barrier((gate, wgt))
                peers.append(_peer_tc(h2f, idx, gate_c, table2, n_sc + n_a + n_b, n_c))
            mod3_b, _ = lax.optimization_barrier((mod3, wgt))
            pending = (x1, peers, b)
        p_x1, p_peers, p_seq = pending
        outs.append(_residual(p_x1, p_peers, p_seq, mod3, g_final[None, :], last))
        x = jnp.concatenate(outs, axis=0)
    return x
```

```python
import functools
import math

import jax
import jax.numpy as jnp
import numpy as np
from jax import lax
from jax.experimental import pallas as pl
from jax.experimental.pallas import tpu as pltpu
from jax.experimental.pallas import tpu_sc as plsc

D_MODEL = 1024
POOL_WIDTH = 512
SGU_WIDTH = 512
POOL_WINDOWS = (2, 4, 8, 16)
GROUP_DIM = 128
CHUNK = 128
SGU_HEADS = 4
IN_PROJ_WIDTH = POOL_WIDTH + 2 * SGU_WIDTH
PEER_HEADS = 8
N_KEYS = 128
HALF_KEY = 128
TOPK = 16
N_MOD = 6
EPS = 1e-6

MIX_BLOCK = 512
TOPK_BLOCK = 256
EW_BLOCK = 256
VMEM_LIMIT_BYTES = 48 * 1024 * 1024

SC_LANES = 16
SC_TOKENS = 32
SC_RING = 8
N_SEL = PEER_HEADS * TOPK
TC_TOKENS = 8

N_CAND = 16 + 7 * 8 + 8


def _pool_band():
    t = np.arange(CHUNK)[:, None]
    j = np.arange(2 * CHUNK)[None, :]
    bands = []
    for w in POOL_WINDOWS:
        m = (j > CHUNK + t - w) & (j <= CHUNK + t)
        bands.append(np.where(m, 1.0 / w, 0.0))
    return np.stack(bands).astype(np.float32)


def _bf16_dot(a, b):
    return jnp.dot(a.astype(jnp.bfloat16), b.astype(jnp.bfloat16), preferred_element_type=jnp.float32)


def _ada_kernel(c_ref, w_ref, b_ref, o_ref):
    c = c_ref[...]
    c_act = c * jax.nn.sigmoid(c)
    o_ref[...] = _bf16_dot(c_act, w_ref[...]) + b_ref[...]


def _ada_mod(c_pad, w_ada, b_ada):
    rows = c_pad.shape[0]
    return pl.pallas_call(
        _ada_kernel,
        grid=(N_MOD,),
        in_specs=[
            pl.BlockSpec((rows, D_MODEL), lambda i: (0, 0)),
            pl.BlockSpec((D_MODEL, D_MODEL), lambda i: (0, i)),
            pl.BlockSpec((1, D_MODEL), lambda i: (0, i)),
        ],
        out_specs=pl.BlockSpec((rows, D_MODEL), lambda i: (0, i)),
        out_shape=jax.ShapeDtypeStruct((rows, N_MOD * D_MODEL), jnp.float32),
        name="ada_mod",
    )(c_pad, w_ada, b_ada)


def _rms(x, g):
    return x * lax.rsqrt(jnp.mean(x * x, axis=-1, keepdims=True) + EPS) * g


def _mix_kernel(x_ref, mod_ref, g1_ref, win_ref, band_ref, wpool_ref, pscale_ref, lng_ref, lnb_ref,
                wsp_ref, bsp_ref, wout_ref, g2_ref, x1_ref, h2_ref, aext_ref, mixed_ref):
    j = pl.program_id(1)
    x = x_ref[0]
    shift1, scale1, gate1 = mod_ref[0, 0:1, :], mod_ref[0, 1:2, :], mod_ref[0, 2:3, :]
    shift2, scale2 = mod_ref[0, 3:4, :], mod_ref[0, 4:5, :]

    h = _rms(x, g1_ref[...]) * (1.0 + scale1) + shift1
    proj = _bf16_dot(h, win_ref[...])
    a = proj[:, :POOL_WIDTH]
    u = proj[:, POOL_WIDTH:POOL_WIDTH + SGU_WIDTH]
    v = proj[:, POOL_WIDTH + SGU_WIDTH:]

    @pl.when(j == 0)
    def _():
        aext_ref[0:CHUNK, :] = jnp.zeros((CHUNK, POOL_WIDTH), jnp.float32)

    aext_ref[CHUNK:, :] = a

    mu = jnp.mean(v, axis=-1, keepdims=True)
    vc = v - mu
    vn = vc * lax.rsqrt(jnp.mean(vc * vc, axis=-1, keepdims=True) + EPS) * lng_ref[...] + lnb_ref[...]

    row = lax.broadcasted_iota(jnp.int32, (CHUNK, CHUNK), 0)
    col = lax.broadcasted_iota(jnp.int32, (CHUNK, CHUNK), 1)
    tril = col <= row
    t_col = lax.broadcasted_iota(jnp.int32, (CHUNK, 1), 0)

    for c in range(MIX_BLOCK // CHUNK):
        r0 = c * CHUNK
        pos = (j * MIX_BLOCK + r0 + 1 + t_col).astype(jnp.float32)
        for g, w in enumerate(POOL_WINDOWS):
            c0 = g * GROUP_DIM
            seg = aext_ref[r0:r0 + 2 * CHUNK, c0:c0 + GROUP_DIM]
            hi = seg.astype(jnp.bfloat16)
            lo = (seg - hi.astype(jnp.float32)).astype(jnp.bfloat16)
            band = band_ref[g]
            win = (jnp.dot(band, hi, preferred_element_type=jnp.float32)
                   + jnp.dot(band, lo, preferred_element_type=jnp.float32))
            mean = win * (float(w) / jnp.minimum(pos, float(w)))
            pooled = mean - seg[CHUNK:, :]
            y = _bf16_dot(pooled, wpool_ref[g]) * pscale_ref[:, c0:c0 + GROUP_DIM]
            mixed_ref[r0:r0 + CHUNK, c0:c0 + GROUP_DIM] = y.astype(jnp.bfloat16)
        for hh in range(SGU_HEADS):
            c0 = hh * GROUP_DIM
            wm = jnp.where(tril, wsp_ref[hh], 0.0)
            m = _bf16_dot(wm, vn[r0:r0 + CHUNK, c0:c0 + GROUP_DIM]) + bsp_ref[:, hh:hh + 1]
            s = u[r0:r0 + CHUNK, c0:c0 + GROUP_DIM] * m
            mixed_ref[r0:r0 + CHUNK, POOL_WIDTH + c0:POOL_WIDTH + c0 + GROUP_DIM] = s.astype(jnp.bfloat16)

    aext_ref[0:CHUNK, :] = aext_ref[MIX_BLOCK:MIX_BLOCK + CHUNK, :]

    x1 = x + gate1 * jnp.dot(mixed_ref[...], wout_ref[...], preferred_element_type=jnp.float32)
    x1_ref[0] = x1
    h2_ref[0] = _rms(x1, g2_ref[...]) * (1.0 + scale2) + shift2


def _mixer(x, b0, n_seq, mod3, g1, w_in, band, w_pool, pool_scale, ln_g, ln_b, w_sp, b_sp_t, w_out, g2):
    _, S, D = x.shape
    const2 = lambda b, j: (0, 0)
    const3 = lambda b, j: (0, 0, 0)
    tok = pl.BlockSpec((1, MIX_BLOCK, D), lambda b, j: (b, j, 0))
    return pl.pallas_call(
        _mix_kernel,
        grid=(n_seq, S // MIX_BLOCK),
        in_specs=[
            pl.BlockSpec((1, MIX_BLOCK, D), lambda b, j: (b0 + b, j, 0)),
            pl.BlockSpec((1, N_MOD, D), lambda b, j: (b0 + b, 0, 0)),
            pl.BlockSpec((1, D), const2),
            pl.BlockSpec((D, IN_PROJ_WIDTH), const2),
            pl.BlockSpec((len(POOL_WINDOWS), CHUNK, 2 * CHUNK), const3),
            pl.BlockSpec((len(POOL_WINDOWS), GROUP_DIM, GROUP_DIM), const3),
            pl.BlockSpec((1, POOL_WIDTH), const2),
            pl.BlockSpec((1, SGU_WIDTH), const2),
            pl.BlockSpec((1, SGU_WIDTH), const2),
            pl.BlockSpec((SGU_HEADS, CHUNK, CHUNK), const3),
            pl.BlockSpec((CHUNK, SGU_HEADS), const2),
            pl.BlockSpec((D, D), const2),
            pl.BlockSpec((1, D), const2),
        ],
        out_specs=[tok, tok],
        out_shape=[jax.ShapeDtypeStruct((n_seq, S, D), jnp.float32)] * 2,
        scratch_shapes=[
            pltpu.VMEM((MIX_BLOCK + CHUNK, POOL_WIDTH), jnp.float32),
            pltpu.VMEM((MIX_BLOCK, D), jnp.bfloat16),
        ],
        compiler_params=pltpu.CompilerParams(
            dimension_semantics=("arbitrary", "arbitrary"), vmem_limit_bytes=VMEM_LIMIT_BYTES),
        name="mixer",
    )(x, mod3, g1, w_in, band, w_pool, pool_scale, ln_g, ln_b, w_sp, b_sp_t, w_out, g2)


def _extract_top(s, ids, payload, n_out):
    vals, pays = [], []
    for _ in range(n_out):
        m = jnp.max(s, axis=0, keepdims=True)
        pick = jnp.min(jnp.where(s == m, ids, jnp.float32(1e9)), axis=0, keepdims=True)
        sel = ids == pick
        vals.append(m)
        pays.append(pick if payload is None
                    else jnp.max(jnp.where(sel, payload, -1.0), axis=0, keepdims=True))
        s = jnp.where(sel, -jnp.inf, s)
    return jnp.concatenate(vals, axis=0), jnp.concatenate(pays, axis=0)


def _topk_kernel(h2_ref, wq_ref, keys_ref, idx_ref, gate_ref):
    T = TOPK_BLOCK
    q = _bf16_dot(h2_ref[...], wq_ref[...])
    key_ids = lax.broadcasted_iota(jnp.int32, (N_KEYS, T), 0).astype(jnp.float32)
    row = lax.broadcasted_iota(jnp.int32, (N_CAND, T), 0)
    flat = jnp.where(row < 16, row,
                     jnp.where(row < 72, jnp.right_shift(row - 8, 3) * 16 + jnp.bitwise_and(row, 7),
                               (row - 64) * 16)).astype(jnp.float32)
    idx_rows, gate_rows = [], []
    for hd in range(PEER_HEADS):
        tops = []
        for p in range(2):
            hp = hd * 2 + p
            qs = q[:, hp * HALF_KEY:(hp + 1) * HALF_KEY].astype(jnp.bfloat16)
            st = lax.dot_general(keys_ref[hp], qs, (((1,), (1,)), ((), ())),
                                 preferred_element_type=jnp.float32)
            tops.append(_extract_top(st, key_ids, None, TOPK))
        (s0, i0), (s1, i1) = tops
        i0 = i0 * float(N_KEYS)
        cand = jnp.concatenate(
            [s0[0:1] + s1] + [s0[a:a + 1] + s1[0:8] for a in range(1, 8)] + [s0[8:16] + s1[0:1]], axis=0)
        cidx = jnp.concatenate(
            [i0[0:1] + i1] + [i0[a:a + 1] + i1[0:8] for a in range(1, 8)] + [i0[8:16] + i1[0:1]],
            axis=0)
        best, sel_idx = _extract_top(cand, flat, cidx, TOPK)
        e = jnp.exp(best - jnp.max(best, axis=0, keepdims=True))
        gate_rows.append(e / jnp.sum(e, axis=0, keepdims=True))
        idx_rows.append(sel_idx)
    idx_ref[...] = jnp.concatenate(idx_rows, axis=0).T.astype(jnp.int32)
    gate_ref[...] = jnp.concatenate(gate_rows, axis=0).T


def _query_topk(h2, w_query, keys):
    N, D = h2.shape
    tok = pl.BlockSpec((TOPK_BLOCK, N_SEL), lambda i: (i, 0))
    return pl.pallas_call(
        _topk_kernel,
        grid=(N // TOPK_BLOCK,),
        in_specs=[
            pl.BlockSpec((TOPK_BLOCK, D), lambda i: (i, 0)),
            pl.BlockSpec((D, 2 * PEER_HEADS * HALF_KEY), lambda i: (0, 0)),
            pl.BlockSpec((2 * PEER_HEADS, N_KEYS, HALF_KEY), lambda i: (0, 0, 0)),
        ],
        out_specs=[tok, tok],
        out_shape=[jax.ShapeDtypeStruct((N, N_SEL), jnp.int32), jax.ShapeDtypeStruct((N, N_SEL), jnp.float32)],
        compiler_params=pltpu.CompilerParams(
            dimension_semantics=("arbitrary",), vmem_limit_bytes=VMEM_LIMIT_BYTES),
        name="query_topk",
    )(h2, w_query, keys)


def _sc_mesh_and_workers():
    info = plsc.get_sparse_core_info()
    assert info.num_lanes == SC_LANES
    mesh = plsc.VectorSubcoreMesh(core_axis_name="core", subcore_axis_name="subcore")
    return mesh, info.num_cores, info.num_cores * info.num_subcores


def _gather_ring(tab_hbm, idx_v, rows_v, sems, consume, n_tokens):
    look = SC_RING - 1

    def copy(t, hd):
        slot = hd % SC_RING
        return pltpu.make_async_copy(
            tab_hbm.at[idx_v.at[t, pl.ds(hd * TOPK, TOPK)]], rows_v.at[slot], sems.at[slot])

    for hd in range(look):
        copy(0, hd).start()

    @pl.loop(0, n_tokens)
    def _(t):
        for hd in range(PEER_HEADS):
            nxt = hd + look
            if nxt < PEER_HEADS:
                copy(t, nxt).start()
            else:
                @pl.when(t + 1 < n_tokens)
                def _():
                    copy(t + 1, nxt - PEER_HEADS).start()
            copy(t, hd).wait()
            consume(t, hd, rows_v.at[hd % SC_RING])


def _for_token_blocks(first, per_worker, block):
    n_full, tail = divmod(per_worker, SC_TOKENS)
    assert tail % 8 == 0, "HBM row offsets of the staged slices must stay 8-aligned"

    @pl.loop(0, n_full)
    def _(blk):
        block(first + blk * SC_TOKENS, SC_TOKENS)

    if tail:
        block(first + n_full * SC_TOKENS, tail)


SC_PAIR_WORDS = D_MODEL // 2
SC_STEPS = SC_PAIR_WORDS // SC_LANES
SC_BF16_SUM = 4


def _bf16_bits(a):
    return lax.bitcast_convert_type(a.astype(jnp.bfloat16), jnp.uint16).astype(jnp.uint32)


def _pack_rows(a):
    bits = _bf16_bits(a)
    return lax.bitcast_convert_type(bits[:, :SC_PAIR_WORDS] | (bits[:, SC_PAIR_WORDS:] << 16), jnp.int32)


def _pack_dup(a):
    bits = _bf16_bits(a)
    return lax.bitcast_convert_type(bits | (bits << 16), jnp.int32)


def _unpack_pair(words):
    lo = lax.bitcast_convert_type(lax.shift_left(words, jnp.int32(16)), jnp.float32)
    hi = lax.bitcast_convert_type(lax.bitwise_and(words, jnp.int32(-65536)), jnp.float32)
    return lo, hi


def _as_bf16(words):
    return plsc.bitcast(words, jnp.bfloat16)


def _widen_sum(products):
    return _unpack_pair(plsc.bitcast(_tree_sum(products), jnp.int32))


def _tree_sum(terms):
    terms = list(terms)
    while len(terms) > 1:
        terms = [terms[i] + terms[i + 1] for i in range(0, len(terms) - 1, 2)] + (
            [terms[-1]] if len(terms) % 2 else [])
    return terms[0]


def _peer_down(h_packed, idx, packed, N):
    mesh, n_cores, n_workers = _sc_mesh_and_workers()
    per_worker = N // n_workers

    def body(h_hbm, idx_hbm, tab_hbm, out_hbm, idx_v, h_v, rows_v, out_v, tr_v, sems):
        wid = lax.axis_index("subcore") * n_cores + lax.axis_index("core")
        lane = lax.iota(jnp.int32, SC_LANES)

        def consume(t, hd, rows):
            def dot_step(g, accs):
                runs = [pl.ds((g * SC_BF16_SUM + i) * SC_LANES, SC_LANES) for i in range(SC_BF16_SUM)]
                hb = [_as_bf16(h_v[t, r]) for r in runs]
                out = []
                for k, acc in enumerate(accs):
                    lo, hi = _widen_sum([_as_bf16(rows[k, r]) * hbi for r, hbi in zip(runs, hb)])
                    out.append(acc + (lo + hi))
                return tuple(out)

            accs = lax.fori_loop(0, SC_STEPS // SC_BF16_SUM, dot_step,
                                 tuple(jnp.zeros((SC_LANES,), jnp.float32) for _ in range(TOPK)))
            for k in range(TOPK):
                tr_v[k, :] = accs[k]
            tot = jnp.zeros((SC_LANES,), jnp.float32)
            for l in range(SC_LANES):
                tot = tot + plsc.load_gather(tr_v, [lane, jnp.full((SC_LANES,), l, jnp.int32)])
            out_v[t, pl.ds(hd * TOPK, TOPK)] = tot

        def block(base, n):
            pltpu.sync_copy(idx_hbm.at[pl.ds(base, n)], idx_v.at[pl.ds(0, n)])
            pltpu.sync_copy(h_hbm.at[pl.ds(base, n)], h_v.at[pl.ds(0, n)])
            _gather_ring(tab_hbm, idx_v, rows_v, sems, consume, n)
            pltpu.sync_copy(out_v.at[pl.ds(0, n)], out_hbm.at[pl.ds(base, n)])

        _for_token_blocks(wid * per_worker, per_worker, block)

    return pl.kernel(
        body,
        out_type=jax.ShapeDtypeStruct((N, N_SEL), jnp.float32),
        mesh=mesh,
        scratch_types=[
            pltpu.VMEM((SC_TOKENS, N_SEL), jnp.int32),
            pltpu.VMEM((SC_TOKENS, SC_PAIR_WORDS), jnp.int32),
            pltpu.VMEM((SC_RING, TOPK, SC_PAIR_WORDS), jnp.int32),
            pltpu.VMEM((SC_TOKENS, N_SEL), jnp.float32),
            pltpu.VMEM((TOPK, SC_LANES), jnp.float32),
            pltpu.SemaphoreType.DMA((SC_RING,)),
        ],
        compiler_params=pltpu.CompilerParams(needs_layout_passes=False),
        name="peer_down",
    )(h_packed, idx, packed)


def _peer_up(w_dup, idx, packed, N):
    D = D_MODEL
    mesh, n_cores, n_workers = _sc_mesh_and_workers()
    per_worker = N // n_workers

    def body(w_hbm, idx_hbm, tab_hbm, out_hbm, idx_v, w_v, rows_v, out_v, sems):
        wid = lax.axis_index("subcore") * n_cores + lax.axis_index("core")

        def consume(t, hd, rows):
            t_vec = jnp.full((SC_LANES,), t, jnp.int32)
            ws = [_as_bf16(plsc.load_gather(w_v, [t_vec, jnp.full((SC_LANES,), hd * TOPK + k, jnp.int32)]))
                  for k in range(TOPK)]

            @plsc.parallel_loop(0, SC_STEPS)
            def _(s):
                run = pl.ds(s * SC_LANES, SC_LANES)
                parts = [_widen_sum([_as_bf16(rows[k, run]) * ws[k] for k in range(g, g + SC_BF16_SUM)])
                         for g in range(0, TOPK, SC_BF16_SUM)]
                for half, c in enumerate((s * SC_LANES, SC_PAIR_WORDS + s * SC_LANES)):
                    sl = pl.ds(c, SC_LANES)
                    terms = [p[half] for p in parts]
                    if hd > 0:
                        terms.append(out_v[t, sl])
                    out_v[t, sl] = _tree_sum(terms)

        def block(base, n):
            pltpu.sync_copy(idx_hbm.at[pl.ds(base, n)], idx_v.at[pl.ds(0, n)])
            pltpu.sync_copy(w_hbm.at[pl.ds(base, n)], w_v.at[pl.ds(0, n)])
            _gather_ring(tab_hbm, idx_v, rows_v, sems, consume, n)
            pltpu.sync_copy(out_v.at[pl.ds(0, n)], out_hbm.at[pl.ds(base, n)])

        _for_token_blocks(wid * per_worker, per_worker, block)

    return pl.kernel(
        body,
        out_type=jax.ShapeDtypeStruct((N, D), jnp.float32),
        mesh=mesh,
        scratch_types=[
            pltpu.VMEM((SC_TOKENS, N_SEL), jnp.int32),
            pltpu.VMEM((SC_TOKENS, N_SEL), jnp.int32),
            pltpu.VMEM((SC_RING, TOPK, SC_PAIR_WORDS), jnp.int32),
            pltpu.VMEM((SC_TOKENS, D), jnp.float32),
            pltpu.SemaphoreType.DMA((SC_RING,)),
        ],
        compiler_params=pltpu.CompilerParams(needs_layout_passes=False),
        name="peer_up",
    )(w_dup, idx, packed)


def _peer_tc_kernel(idx_cur, idx_nxt, h_ref, gate_ref, tab_hbm, o_ref, gbuf, sems):
    i = pl.program_id(0)
    n = pl.num_programs(0)
    groups = N_SEL // 8

    def issue(idx_ref, slot):
        def body(g, carry):
            for j in range(8):
                e = idx_ref[g * 8 + j]
                pltpu.make_async_copy(
                    tab_hbm.at[pl.ds(e, 1), :], gbuf.at[slot, g, pl.ds(j, 1), :], sems.at[slot]).start()
            return carry
        lax.fori_loop(0, TC_TOKENS * groups, body, 0)

    @pl.when(i == 0)
    def _():
        issue(idx_cur, 0)

    @pl.when(i + 1 < n)
    def _():
        issue(idx_nxt, (i + 1) % 2)

    slot = i % 2
    pltpu.make_async_copy(gbuf.at[slot], gbuf.at[slot], sems.at[slot]).wait()

    def rows(t, c0):
        return gbuf[slot, t * groups:(t + 1) * groups, :, c0:c0 + D_MODEL].reshape(N_SEL, D_MODEL)

    lane = lax.broadcasted_iota(jnp.int32, (N_SEL, 128), 1)
    pre_all = jnp.zeros((N_SEL, 128), jnp.float32)
    for t in range(TC_TOKENS):
        prod = rows(t, 0) * h_ref[t:t + 1, :]
        part = prod[:, 0:128]
        for c in range(1, D_MODEL // 128):
            part = part + prod[:, c * 128:(c + 1) * 128]
        pre = jnp.sum(part, axis=1, keepdims=True)
        pre_all = jnp.where(lane == t, pre, pre_all)
    act = 0.5 * pre_all * (1.0 + lax.erf(pre_all * (1.0 / math.sqrt(2.0))))
    gate_t = jnp.concatenate([gate_ref[...]] * (128 // TC_TOKENS), axis=0).T
    wgt = gate_t * act
    for t in range(TC_TOKENS):
        w_col = jnp.sum(jnp.where(lane == t, wgt, 0.0), axis=1, keepdims=True)
        up = rows(t, D_MODEL) * w_col
        o_ref[t:t + 1, :] = jnp.sum(up, axis=0, keepdims=True)


def _peer_tc(h2, idx, gate, table2, row0, n_tokens):
    D = h2.shape[1]
    steps = n_tokens // TC_TOKENS
    blk0 = row0 // TC_TOKENS
    idx_flat = idx.reshape(-1)
    return pl.pallas_call(
        _peer_tc_kernel,
        grid=(steps,),
        in_specs=[
            pl.BlockSpec((TC_TOKENS * N_SEL,), lambda i: (blk0 + i,), memory_space=pltpu.SMEM),
            pl.BlockSpec((TC_TOKENS * N_SEL,), lambda i: (blk0 + jnp.minimum(i + 1, steps - 1),),
                         memory_space=pltpu.SMEM),
            pl.BlockSpec((TC_TOKENS, D), lambda i: (blk0 + i, 0)),
            pl.BlockSpec((TC_TOKENS, N_SEL), lambda i: (blk0 + i, 0)),
            pl.BlockSpec(memory_space=pl.ANY),
        ],
        out_specs=pl.BlockSpec((TC_TOKENS, D), lambda i: (i, 0)),
        out_shape=jax.ShapeDtypeStruct((n_tokens, D), jnp.float32),
        scratch_shapes=[
            pltpu.VMEM((2, TC_TOKENS * N_SEL // 8, 8, 2 * D), jnp.float32),
            pltpu.SemaphoreType.DMA((2,)),
        ],
        compiler_params=pltpu.CompilerParams(
            dimension_semantics=("arbitrary",), vmem_limit_bytes=VMEM_LIMIT_BYTES),
        name="peer_tc",
    )(idx_flat, idx_flat, h2, gate, table2)


def _gate_act_kernel(pre_ref, gate_ref, o_ref):
    p = pre_ref[...]
    act = 0.5 * p * (1.0 + lax.erf(p * (1.0 / math.sqrt(2.0))))
    o_ref[...] = gate_ref[...] * act


def _gate_act(pre, gate):
    N = pre.shape[0]
    spec = pl.BlockSpec((EW_BLOCK, N_SEL), lambda i: (i, 0))
    return pl.pallas_call(
        _gate_act_kernel, grid=(N // EW_BLOCK,), in_specs=[spec, spec], out_specs=spec,
        out_shape=jax.ShapeDtypeStruct((N, N_SEL), jnp.float32), name="gate_act",
    )(pre, gate)


def _final_kernel(x1_ref, mod_ref, g_ref, *refs, normalize, starts):
    peer_refs, o_ref = refs[:-1], refs[-1]
    gate2 = mod_ref[0, 5:6, :]
    peer = peer_refs[0][...]
    for p in range(1, len(peer_refs)):
        peer = jnp.where(pl.program_id(0) >= starts[p], peer_refs[p][...], peer)
    y = x1_ref[0] + gate2 * peer
    o_ref[0] = _rms(y, g_ref[...]) if normalize else y


def _residual(x1, peers, b, mod3, g_final, normalize):
    _, S, D = x1.shape
    blocks = [p.shape[0] // EW_BLOCK for p in peers]
    starts = [sum(blocks[:p]) for p in range(len(peers))]
    tok = pl.BlockSpec((1, EW_BLOCK, D), lambda j: (0, j, 0))

    def piece_spec(start, n):
        return pl.BlockSpec((EW_BLOCK, D), lambda j: (jnp.clip(j - start, 0, n - 1), 0))

    return pl.pallas_call(
        functools.partial(_final_kernel, normalize=normalize, starts=tuple(starts)),
        grid=(S // EW_BLOCK,),
        in_specs=[tok,
                  pl.BlockSpec((1, N_MOD, D), lambda j: (b, 0, 0)),
                  pl.BlockSpec((1, D), lambda j: (0, 0))]
                 + [piece_spec(s, n) for s, n in zip(starts, blocks)],
        out_specs=tok,
        out_shape=jax.ShapeDtypeStruct((1, S, D), jnp.float32),
        compiler_params=pltpu.CompilerParams(vmem_limit_bytes=VMEM_LIMIT_BYTES),
        name="final_norm",
    )(x1, mod3, g_final, *peers)


def _token_split(b, n_seq, S):
    unit = S // 16
    if b == n_seq - 1 and n_seq > 1:
        sc, c = 11, 2
    elif b == 0:
        sc, c = 14, 0
    else:
        sc, c = 13, 0
    n_sc, n_c = sc * unit, c * unit
    n_a = ((S - n_sc - n_c) // 2) // EW_BLOCK * EW_BLOCK
    return n_sc, n_a, S - n_sc - n_c - n_a, n_c


def kernel(x, c, w_ada, b_ada, g_norm1, w_in, w_pool, pool_scale, sgu_ln_g, sgu_ln_b, w_spatial, b_spatial,
           w_out, g_norm2, w_query, sub_keys, expert_down, expert_up, g_final):
    B, S, D = x.shape
    depth = w_ada.shape[0]
    bf16 = jnp.bfloat16
    band = jnp.asarray(_pool_band(), bf16)
    c_pad = jnp.pad(c, ((0, 8 - B), (0, 0)))
    for l in range(depth):
        mod = _ada_mod(c_pad, w_ada[l].astype(bf16), b_ada[l][None, :])[:B]
        mod3 = mod.reshape(B, N_MOD, D)
        mix_w = (g_norm1[l][None, :], w_in[l].astype(bf16), band, w_pool[l].astype(bf16),
                 pool_scale[l][None, :], sgu_ln_g[l][None, :], sgu_ln_b[l][None, :], w_spatial[l],
                 b_spatial[l].T, w_out[l].astype(bf16), g_norm2[l][None, :])
        keys = sub_keys[l].reshape(2 * PEER_HEADS, N_KEYS, HALF_KEY).astype(bf16)
        wq = w_query[l].astype(bf16)
        table2 = jnp.concatenate([expert_down[l], expert_up[l]], axis=1)
        down_packed = _pack_rows(expert_down[l])
        up_packed = _pack_rows(expert_up[l])
        last = l + 1 == depth
        outs = []
        pending = None
        mod3_b = mod3
        for b in range(B):
            n_sc, n_a, n_b, n_c = _token_split(b, B, S)
            x1, h2 = _mixer(x, b, 1, mod3_b, *mix_w)
            h2f = h2.reshape(S, D)
            idx, gate = _query_topk(h2f, wq, keys)
            tc_a = _peer_tc(h2f, idx, gate, table2, n_sc, n_a)
            pre = _peer_down(_pack_rows(h2f[:n_sc]), idx, down_packed, n_sc)
            gate_b = gate
            if pending is not None:
                p_x1, p_peers, p_seq = pending
                p_peers[0], tc_a = lax.optimization_barrier((p_peers[0], tc_a))
                out_prev = _residual(p_x1, p_peers, p_seq, mod3, g_final[None, :], last)
                outs.append(out_prev)
                gate_b, _ = lax.optimization_barrier((gate, out_prev))
            tc_b = _peer_tc(h2f, idx, gate_b, table2, n_sc + n_a, n_b)
            pre, tc_a, tc_b = lax.optimization_barrier((pre, tc_a, tc_b))
            wgt = _gate_act(pre, gate)
            peers = [_peer_up(_pack_dup(wgt), idx, up_packed, n_sc), tc_a, tc_b]
            if n_c:
                gate_c, _ = lax.optimization_barrier((gate, wgt))
                peers.append(_peer_tc(h2f, idx, gate_c, table2, n_sc + n_a + n_b, n_c))
            mod3_b, _ = lax.optimization_barrier((mod3, wgt))
            pending = (x1, peers, b)
        p_x1, p_peers, p_seq = pending
        outs.append(_residual(p_x1, p_peers, p_seq, mod3, g_final[None, :], last))
        x = jnp.concatenate(outs, axis=0)
    return x
```

```python
import functools
import math

import jax
import jax.numpy as jnp
import numpy as np
from jax import lax
from jax.experimental import pallas as pl
from jax.experimental.pallas import tpu as pltpu
from jax.experimental.pallas import tpu_sc as plsc

D_MODEL = 1024
POOL_WIDTH = 512
SGU_WIDTH = 512
POOL_WINDOWS = (2, 4, 8, 16)
GROUP_DIM = 128
CHUNK = 128
SGU_HEADS = 4
IN_PROJ_WIDTH = POOL_WIDTH + 2 * SGU_WIDTH
PEER_HEADS = 8
N_KEYS = 128
HALF_KEY = 128
TOPK = 16
N_MOD = 6
EPS = 1e-6

MIX_BLOCK = 512
TOPK_BLOCK = 256
EW_BLOCK = 256
VMEM_LIMIT_BYTES = 48 * 1024 * 1024

SC_LANES = 16
SC_TOKENS = 32
SC_RING = 8
N_SEL = PEER_HEADS * TOPK
TC_TOKENS = 8

N_CAND = 16 + 7 * 8 + 8


def _pool_band():
    t = np.arange(CHUNK)[:, None]
    j = np.arange(2 * CHUNK)[None, :]
    bands = []
    for w in POOL_WINDOWS:
        m = (j > CHUNK + t - w) & (j <= CHUNK + t)
        bands.append(np.where(m, 1.0 / w, 0.0))
    return np.stack(bands).astype(np.float32)


def _bf16_dot(a, b):
    return jnp.dot(a.astype(jnp.bfloat16), b.astype(jnp.bfloat16), preferred_element_type=jnp.float32)


def _ada_kernel(c_ref, w_ref, b_ref, o_ref):
    c = c_ref[...]
    c_act = c * jax.nn.sigmoid(c)
    o_ref[...] = _bf16_dot(c_act, w_ref[...]) + b_ref[...]


def _ada_mod(c_pad, w_ada, b_ada):
    rows = c_pad.shape[0]
    return pl.pallas_call(
        _ada_kernel,
        grid=(N_MOD,),
        in_specs=[
            pl.BlockSpec((rows, D_MODEL), lambda i: (0, 0)),
            pl.BlockSpec((D_MODEL, D_MODEL), lambda i: (0, i)),
            pl.BlockSpec((1, D_MODEL), lambda i: (0, i)),
        ],
        out_specs=pl.BlockSpec((rows, D_MODEL), lambda i: (0, i)),
        out_shape=jax.ShapeDtypeStruct((rows, N_MOD * D_MODEL), jnp.float32),
        name="ada_mod",
    )(c_pad, w_ada, b_ada)


def _rms(x, g):
    return x * lax.rsqrt(jnp.mean(x * x, axis=-1, keepdims=True) + EPS) * g


def _mix_kernel(x_ref, mod_ref, g1_ref, win_ref, band_ref, wpool_ref, pscale_ref, lng_ref, lnb_ref,
                wsp_ref, bsp_ref, wout_ref, g2_ref, x1_ref, h2_ref, aext_ref, mixed_ref):
    j = pl.program_id(1)
    x = x_ref[0]
    shift1, scale1, gate1 = mod_ref[0, 0:1, :], mod_ref[0, 1:2, :], mod_ref[0, 2:3, :]
    shift2, scale2 = mod_ref[0, 3:4, :], mod_ref[0, 4:5, :]

    h = _rms(x, g1_ref[...]) * (1.0 + scale1) + shift1
    proj = _bf16_dot(h, win_ref[...])
    a = proj[:, :POOL_WIDTH]
    u = proj[:, POOL_WIDTH:POOL_WIDTH + SGU_WIDTH]
    v = proj[:, POOL_WIDTH + SGU_WIDTH:]

    @pl.when(j == 0)
    def _():
        aext_ref[0:CHUNK, :] = jnp.zeros((CHUNK, POOL_WIDTH), jnp.float32)

    aext_ref[CHUNK:, :] = a

    mu = jnp.mean(v, axis=-1, keepdims=True)
    vc = v - mu
    vn = vc * lax.rsqrt(jnp.mean(vc * vc, axis=-1, keepdims=True) + EPS) * lng_ref[...] + lnb_ref[...]

    row = lax.broadcasted_iota(jnp.int32, (CHUNK, CHUNK), 0)
    col = lax.broadcasted_iota(jnp.int32, (CHUNK, CHUNK), 1)
    tril = col <= row
    t_col = lax.broadcasted_iota(jnp.int32, (CHUNK, 1), 0)

    for c in range(MIX_BLOCK // CHUNK):
        r0 = c * CHUNK
        pos = (j * MIX_BLOCK + r0 + 1 + t_col).astype(jnp.float32)
        for g, w in enumerate(POOL_WINDOWS):
            c0 = g * GROUP_DIM
            seg = aext_ref[r0:r0 + 2 * CHUNK, c0:c0 + GROUP_DIM]
            hi = seg.astype(jnp.bfloat16)
            lo = (seg - hi.astype(jnp.float32)).astype(jnp.bfloat16)
            band = band_ref[g]
            win = (jnp.dot(band, hi, preferred_element_type=jnp.float32)
                   + jnp.dot(band, lo, preferred_element_type=jnp.float32))
            mean = win * (float(w) / jnp.minimum(pos, float(w)))
            pooled = mean - seg[CHUNK:, :]
            y = _bf16_dot(pooled, wpool_ref[g]) * pscale_ref[:, c0:c0 + GROUP_DIM]
            mixed_ref[r0:r0 + CHUNK, c0:c0 + GROUP_DIM] = y.astype(jnp.bfloat16)
        for hh in range(SGU_HEADS):
            c0 = hh * GROUP_DIM
            wm = jnp.where(tril, wsp_ref[hh], 0.0)
            m = _bf16_dot(wm, vn[r0:r0 + CHUNK, c0:c0 + GROUP_DIM]) + bsp_ref[:, hh:hh + 1]
            s = u[r0:r0 + CHUNK, c0:c0 + GROUP_DIM] * m
            mixed_ref[r0:r0 + CHUNK, POOL_WIDTH + c0:POOL_WIDTH + c0 + GROUP_DIM] = s.astype(jnp.bfloat16)

    aext_ref[0:CHUNK, :] = aext_ref[MIX_BLOCK:MIX_BLOCK + CHUNK, :]

    x1 = x + gate1 * jnp.dot(mixed_ref[...], wout_ref[...], preferred_element_type=jnp.float32)
    x1_ref[0] = x1
    h2_ref[0] = _rms(x1, g2_ref[...]) * (1.0 + scale2) + shift2


def _mixer(x, b0, n_seq, mod3, g1, w_in, band, w_pool, pool_scale, ln_g, ln_b, w_sp, b_sp_t, w_out, g2):
    _, S, D = x.shape
    const2 = lambda b, j: (0, 0)
    const3 = lambda b, j: (0, 0, 0)
    tok = pl.BlockSpec((1, MIX_BLOCK, D), lambda b, j: (b, j, 0))
    return pl.pallas_call(
        _mix_kernel,
        grid=(n_seq, S // MIX_BLOCK),
        in_specs=[
            pl.BlockSpec((1, MIX_BLOCK, D), lambda b, j: (b0 + b, j, 0)),
            pl.BlockSpec((1, N_MOD, D), lambda b, j: (b0 + b, 0, 0)),
            pl.BlockSpec((1, D), const2),
            pl.BlockSpec((D, IN_PROJ_WIDTH), const2),
            pl.BlockSpec((len(POOL_WINDOWS), CHUNK, 2 * CHUNK), const3),
            pl.BlockSpec((len(POOL_WINDOWS), GROUP_DIM, GROUP_DIM), const3),
            pl.BlockSpec((1, POOL_WIDTH), const2),
            pl.BlockSpec((1, SGU_WIDTH), const2),
            pl.BlockSpec((1, SGU_WIDTH), const2),
            pl.BlockSpec((SGU_HEADS, CHUNK, CHUNK), const3),
            pl.BlockSpec((CHUNK, SGU_HEADS), const2),
            pl.BlockSpec((D, D), const2),
            pl.BlockSpec((1, D), const2),
        ],
        out_specs=[tok, tok],
        out_shape=[jax.ShapeDtypeStruct((n_seq, S, D), jnp.float32)] * 2,
        scratch_shapes=[
            pltpu.VMEM((MIX_BLOCK + CHUNK, POOL_WIDTH), jnp.float32),
            pltpu.VMEM((MIX_BLOCK, D), jnp.bfloat16),
        ],
        compiler_params=pltpu.CompilerParams(
            dimension_semantics=("arbitrary", "arbitrary"), vmem_limit_bytes=VMEM_LIMIT_BYTES),
        name="mixer",
    )(x, mod3, g1, w_in, band, w_pool, pool_scale, ln_g, ln_b, w_sp, b_sp_t, w_out, g2)


def _extract_top(s, ids, payload, n_out):
    vals, pays = [], []
    for _ in range(n_out):
        m = jnp.max(s, axis=0, keepdims=True)
        pick = jnp.min(jnp.where(s == m, ids, jnp.float32(1e9)), axis=0, keepdims=True)
        sel = ids == pick
        vals.append(m)
        pays.append(pick if payload is None
                    else jnp.max(jnp.where(sel, payload, -1.0), axis=0, keepdims=True))
        s = jnp.where(sel, -jnp.inf, s)
    return jnp.concatenate(vals, axis=0), jnp.concatenate(pays, axis=0)


def _topk_kernel(h2_ref, wq_ref, keys_ref, idx_ref, gate_ref):
    T = TOPK_BLOCK
    q = _bf16_dot(h2_ref[...], wq_ref[...])
    key_ids = lax.broadcasted_iota(jnp.int32, (N_KEYS, T), 0).astype(jnp.float32)
    row = lax.broadcasted_iota(jnp.int32, (N_CAND, T), 0)
    flat = jnp.where(row < 16, row,
                     jnp.where(row < 72, jnp.right_shift(row - 8, 3) * 16 + jnp.bitwise_and(row, 7),
                               (row - 64) * 16)).astype(jnp.float32)
    idx_rows, gate_rows = [], []
    for hd in range(PEER_HEADS):
        tops = []
        for p in range(2):
            hp = hd * 2 + p
            qs = q[:, hp * HALF_KEY:(hp + 1) * HALF_KEY].astype(jnp.bfloat16)
            st = lax.dot_general(keys_ref[hp], qs, (((1,), (1,)), ((), ())),
                                 preferred_element_type=jnp.float32)
            tops.append(_extract_top(st, key_ids, None, TOPK))
        (s0, i0), (s1, i1) = tops
        i0 = i0 * float(N_KEYS)
        cand = jnp.concatenate(
            [s0[0:1] + s1] + [s0[a:a + 1] + s1[0:8] for a in range(1, 8)] + [s0[8:16] + s1[0:1]], axis=0)
        cidx = jnp.concatenate(
            [i0[0:1] + i1] + [i0[a:a + 1] + i1[0:8] for a in range(1, 8)] + [i0[8:16] + i1[0:1]],
            axis=0)
        best, sel_idx = _extract_top(cand, flat, cidx, TOPK)
        e = jnp.exp(best - jnp.max(best, axis=0, keepdims=True))
        gate_rows.append(e / jnp.sum(e, axis=0, keepdims=True))
        idx_rows.append(sel_idx)
    idx_ref[...] = jnp.concatenate(idx_rows, axis=0).T.astype(jnp.int32)
    gate_ref[...] = jnp.concatenate(gate_rows, axis=0).T


def _query_topk(h2, w_query, keys):
    N, D = h2.shape
    tok = pl.BlockSpec((TOPK_BLOCK, N_SEL), lambda i: (i, 0))
    return pl.pallas_call(
        _topk_kernel,
        grid=(N // TOPK_BLOCK,),
        in_specs=[
            pl.BlockSpec((TOPK_BLOCK, D), lambda i: (i, 0)),
            pl.BlockSpec((D, 2 * PEER_HEADS * HALF_KEY), lambda i: (0, 0)),
            pl.BlockSpec((2 * PEER_HEADS, N_KEYS, HALF_KEY), lambda i: (0, 0, 0)),
        ],
        out_specs=[tok, tok],
        out_shape=[jax.ShapeDtypeStruct((N, N_SEL), jnp.int32), jax.ShapeDtypeStruct((N, N_SEL), jnp.float32)],
        compiler_params=pltpu.CompilerParams(
            dimension_semantics=("arbitrary",), vmem_limit_bytes=VMEM_LIMIT_BYTES),
        name="query_topk",
    )(h2, w_query, keys)


def _sc_mesh_and_workers():
    info = plsc.get_sparse_core_info()
    assert info.num_lanes == SC_LANES
    mesh = plsc.VectorSubcoreMesh(core_axis_name="core", subcore_axis_name="subcore")
    return mesh, info.num_cores, info.num_cores * info.num_subcores


def _gather_ring(tab_hbm, idx_v, rows_v, sems, consume, n_tokens):
    look = SC_RING - 1

    def copy(t, hd):
        slot = hd % SC_RING
        return pltpu.make_async_copy(
            tab_hbm.at[idx_v.at[t, pl.ds(hd * TOPK, TOPK)]], rows_v.at[slot], sems.at[slot])

    for hd in range(look):
        copy(0, hd).start()

    @pl.loop(0, n_tokens)
    def _(t):
        for hd in range(PEER_HEADS):
            nxt = hd + look
            if nxt < PEER_HEADS:
                copy(t, nxt).start()
            else:
                @pl.when(t + 1 < n_tokens)
                def _():
                    copy(t + 1, nxt - PEER_HEADS).start()
            copy(t, hd).wait()
            consume(t, hd, rows_v.at[hd % SC_RING])


def _for_token_blocks(first, per_worker, block):
    n_full, tail = divmod(per_worker, SC_TOKENS)
    assert tail % 8 == 0, "HBM row offsets of the staged slices must stay 8-aligned"

    @pl.loop(0, n_full)
    def _(blk):
        block(first + blk * SC_TOKENS, SC_TOKENS)

    if tail:
        block(first + n_full * SC_TOKENS, tail)


SC_PAIR_WORDS = D_MODEL // 2
SC_STEPS = SC_PAIR_WORDS // SC_LANES
SC_BF16_SUM = 4


def _bf16_bits(a):
    return lax.bitcast_convert_type(a.astype(jnp.bfloat16), jnp.uint16).astype(jnp.uint32)


def _pack_rows(a):
    bits = _bf16_bits(a)
    return lax.bitcast_convert_type(bits[:, :SC_PAIR_WORDS] | (bits[:, SC_PAIR_WORDS:] << 16), jnp.int32)


def _pack_dup(a):
    bits = _bf16_bits(a)
    return lax.bitcast_convert_type(bits | (bits << 16), jnp.int32)


def _unpack_pair(words):
    lo = lax.bitcast_convert_type(lax.shift_left(words, jnp.int32(16)), jnp.float32)
    hi = lax.bitcast_convert_type(lax.bitwise_and(words, jnp.int32(-65536)), jnp.float32)
    return lo, hi


def _as_bf16(words):
    return plsc.bitcast(words, jnp.bfloat16)


def _widen_sum(products):
    return _unpack_pair(plsc.bitcast(_tree_sum(products), jnp.int32))


def _tree_sum(terms):
    terms = list(terms)
    while len(terms) > 1:
        terms = [terms[i] + terms[i + 1] for i in range(0, len(terms) - 1, 2)] + (
            [terms[-1]] if len(terms) % 2 else [])
    return terms[0]


def _peer_down(h_packed, idx, packed, N):
    mesh, n_cores, n_workers = _sc_mesh_and_workers()
    per_worker = N // n_workers

    def body(h_hbm, idx_hbm, tab_hbm, out_hbm, idx_v, h_v, rows_v, out_v, tr_v, sems):
        wid = lax.axis_index("subcore") * n_cores + lax.axis_index("core")
        lane = lax.iota(jnp.int32, SC_LANES)

        def consume(t, hd, rows):
            def dot_step(g, accs):
                runs = [pl.ds((g * SC_BF16_SUM + i) * SC_LANES, SC_LANES) for i in range(SC_BF16_SUM)]
                hb = [_as_bf16(h_v[t, r]) for r in runs]
                out = []
                for k, acc in enumerate(accs):
                    lo, hi = _widen_sum([_as_bf16(rows[k, r]) * hbi for r, hbi in zip(runs, hb)])
                    out.append(acc + (lo + hi))
                return tuple(out)

            accs = lax.fori_loop(0, SC_STEPS // SC_BF16_SUM, dot_step,
                                 tuple(jnp.zeros((SC_LANES,), jnp.float32) for _ in range(TOPK)))
            for k in range(TOPK):
                tr_v[k, :] = accs[k]
            tot = jnp.zeros((SC_LANES,), jnp.float32)
            for l in range(SC_LANES):
                tot = tot + plsc.load_gather(tr_v, [lane, jnp.full((SC_LANES,), l, jnp.int32)])
            out_v[t, pl.ds(hd * TOPK, TOPK)] = tot

        def block(base, n):
            pltpu.sync_copy(idx_hbm.at[pl.ds(base, n)], idx_v.at[pl.ds(0, n)])
            pltpu.sync_copy(h_hbm.at[pl.ds(base, n)], h_v.at[pl.ds(0, n)])
            _gather_ring(tab_hbm, idx_v, rows_v, sems, consume, n)
            pltpu.sync_copy(out_v.at[pl.ds(0, n)], out_hbm.at[pl.ds(base, n)])

        _for_token_blocks(wid * per_worker, per_worker, block)

    return pl.kernel(
        body,
        out_type=jax.ShapeDtypeStruct((N, N_SEL), jnp.float32),
        mesh=mesh,
        scratch_types=[
            pltpu.VMEM((SC_TOKENS, N_SEL), jnp.int32),
            pltpu.VMEM((SC_TOKENS, SC_PAIR_WORDS), jnp.int32),
            pltpu.VMEM((SC_RING, TOPK, SC_PAIR_WORDS), jnp.int32),
            pltpu.VMEM((SC_TOKENS, N_SEL), jnp.float32),
            pltpu.VMEM((TOPK, SC_LANES), jnp.float32),
            pltpu.SemaphoreType.DMA((SC_RING,)),
        ],
        compiler_params=pltpu.CompilerParams(needs_layout_passes=False),
        name="peer_down",
    )(h_packed, idx, packed)


def _peer_up(w_dup, idx, packed, N):
    D = D_MODEL
    mesh, n_cores, n_workers = _sc_mesh_and_workers()
    per_worker = N // n_workers

    def body(w_hbm, idx_hbm, tab_hbm, out_hbm, idx_v, w_v, rows_v, out_v, sems):
        wid = lax.axis_index("subcore") * n_cores + lax.axis_index("core")

        def consume(t, hd, rows):
            t_vec = jnp.full((SC_LANES,), t, jnp.int32)
            ws = [_as_bf16(plsc.load_gather(w_v, [t_vec, jnp.full((SC_LANES,), hd * TOPK + k, jnp.int32)]))
                  for k in range(TOPK)]

            @plsc.parallel_loop(0, SC_STEPS)
            def _(s):
                run = pl.ds(s * SC_LANES, SC_LANES)
                parts = [_widen_sum([_as_bf16(rows[k, run]) * ws[k] for k in range(g, g + SC_BF16_SUM)])
                         for g in range(0, TOPK, SC_BF16_SUM)]
                for half, c in enumerate((s * SC_LANES, SC_PAIR_WORDS + s * SC_LANES)):
                    sl = pl.ds(c, SC_LANES)
                    terms = [p[half] for p in parts]
                    if hd > 0:
                        terms.append(out_v[t, sl])
                    out_v[t, sl] = _tree_sum(terms)

        def block(base, n):
            pltpu.sync_copy(idx_hbm.at[pl.ds(base, n)], idx_v.at[pl.ds(0, n)])
            pltpu.sync_copy(w_hbm.at[pl.ds(base, n)], w_v.at[pl.ds(0, n)])
            _gather_ring(tab_hbm, idx_v, rows_v, sems, consume, n)
            pltpu.sync_copy(out_v.at[pl.ds(0, n)], out_hbm.at[pl.ds(base, n)])

        _for_token_blocks(wid * per_worker, per_worker, block)

    return pl.kernel(
        body,
        out_type=jax.ShapeDtypeStruct((N, D), jnp.float32),
        mesh=mesh,
        scratch_types=[
            pltpu.VMEM((SC_TOKENS, N_SEL), jnp.int32),
            pltpu.VMEM((SC_TOKENS, N_SEL), jnp.int32),
            pltpu.VMEM((SC_RING, TOPK, SC_PAIR_WORDS), jnp.int32),
            pltpu.VMEM((SC_TOKENS, D), jnp.float32),
            pltpu.SemaphoreType.DMA((SC_RING,)),
        ],
        compiler_params=pltpu.CompilerParams(needs_layout_passes=False),
        name="peer_up",
    )(w_dup, idx, packed)


def _peer_tc_kernel(idx_cur, idx_nxt, h_ref, gate_ref, tab_hbm, o_ref, gbuf, sems):
    i = pl.program_id(0)
    n = pl.num_programs(0)
    groups = N_SEL // 8

    def issue(idx_ref, slot):
        def body(g, carry):
            for j in range(8):
                e = idx_ref[g * 8 + j]
                pltpu.make_async_copy(
                    tab_hbm.at[pl.ds(e, 1), :], gbuf.at[slot, g, pl.ds(j, 1), :], sems.at[slot]).start()
            return carry
        lax.fori_loop(0, TC_TOKENS * groups, body, 0)

    @pl.when(i == 0)
    def _():
        issue(idx_cur, 0)

    @pl.when(i + 1 < n)
    def _():
        issue(idx_nxt, (i + 1) % 2)

    slot = i % 2
    pltpu.make_async_copy(gbuf.at[slot], gbuf.at[slot], sems.at[slot]).wait()

    half = SC_PAIR_WORDS

    def rows(t, c0):
        w = gbuf[slot, t * groups:(t + 1) * groups, :, c0:c0 + half].reshape(N_SEL, half)
        lo = lax.bitcast_convert_type(lax.shift_left(w, jnp.int32(16)), jnp.float32)
        hi = lax.bitcast_convert_type(lax.bitwise_and(w, jnp.int32(-65536)), jnp.float32)
        return lo, hi

    lane = lax.broadcasted_iota(jnp.int32, (N_SEL, 128), 1)
    pre_all = jnp.zeros((N_SEL, 128), jnp.float32)
    for t in range(TC_TOKENS):
        lo, hi = rows(t, 0)
        prod = lo * h_ref[t:t + 1, 0:half] + hi * h_ref[t:t + 1, half:D_MODEL]
        part = prod[:, 0:128]
        for c in range(1, half // 128):
            part = part + prod[:, c * 128:(c + 1) * 128]
        pre = jnp.sum(part, axis=1, keepdims=True)
        pre_all = jnp.where(lane == t, pre, pre_all)
    act = 0.5 * pre_all * (1.0 + lax.erf(pre_all * (1.0 / math.sqrt(2.0))))
    gate_t = jnp.concatenate([gate_ref[...]] * (128 // TC_TOKENS), axis=0).T
    wgt = gate_t * act
    for t in range(TC_TOKENS):
        w_col = jnp.sum(jnp.where(lane == t, wgt, 0.0), axis=1, keepdims=True)
        lo, hi = rows(t, half)
        o_ref[t:t + 1, 0:half] = jnp.sum(lo * w_col, axis=0, keepdims=True)
        o_ref[t:t + 1, half:D_MODEL] = jnp.sum(hi * w_col, axis=0, keepdims=True)


def _peer_tc(h2, idx, gate, table2, row0, n_tokens):
    D = h2.shape[1]
    steps = n_tokens // TC_TOKENS
    blk0 = row0 // TC_TOKENS
    idx_flat = idx.reshape(-1)
    return pl.pallas_call(
        _peer_tc_kernel,
        grid=(steps,),
        in_specs=[
            pl.BlockSpec((TC_TOKENS * N_SEL,), lambda i: (blk0 + i,), memory_space=pltpu.SMEM),
            pl.BlockSpec((TC_TOKENS * N_SEL,), lambda i: (blk0 + jnp.minimum(i + 1, steps - 1),),
                         memory_space=pltpu.SMEM),
            pl.BlockSpec((TC_TOKENS, D), lambda i: (blk0 + i, 0)),
            pl.BlockSpec((TC_TOKENS, N_SEL), lambda i: (blk0 + i, 0)),
            pl.BlockSpec(memory_space=pl.ANY),
        ],
        out_specs=pl.BlockSpec((TC_TOKENS, D), lambda i: (i, 0)),
        out_shape=jax.ShapeDtypeStruct((n_tokens, D), jnp.float32),
        scratch_shapes=[
            pltpu.VMEM((2, TC_TOKENS * N_SEL // 8, 8, 2 * SC_PAIR_WORDS), jnp.int32),
            pltpu.SemaphoreType.DMA((2,)),
        ],
        compiler_params=pltpu.CompilerParams(
            dimension_semantics=("arbitrary",), vmem_limit_bytes=VMEM_LIMIT_BYTES),
        name="peer_tc",
    )(idx_flat, idx_flat, h2, gate, table2)


def _gate_act_kernel(pre_ref, gate_ref, o_ref):
    p = pre_ref[...]
    act = 0.5 * p * (1.0 + lax.erf(p * (1.0 / math.sqrt(2.0))))
    o_ref[...] = gate_ref[...] * act


def _gate_act(pre, gate):
    N = pre.shape[0]
    spec = pl.BlockSpec((EW_BLOCK, N_SEL), lambda i: (i, 0))
    return pl.pallas_call(
        _gate_act_kernel, grid=(N // EW_BLOCK,), in_specs=[spec, spec], out_specs=spec,
        out_shape=jax.ShapeDtypeStruct((N, N_SEL), jnp.float32), name="gate_act",
    )(pre, gate)


def _final_kernel(x1_ref, mod_ref, g_ref, *refs, normalize, starts):
    peer_refs, o_ref = refs[:-1], refs[-1]
    gate2 = mod_ref[0, 5:6, :]
    peer = peer_refs[0][...]
    for p in range(1, len(peer_refs)):
        peer = jnp.where(pl.program_id(0) >= starts[p], peer_refs[p][...], peer)
    y = x1_ref[0] + gate2 * peer
    o_ref[0] = _rms(y, g_ref[...]) if normalize else y


def _residual(x1, peers, b, mod3, g_final, normalize):
    _, S, D = x1.shape
    blocks = [p.shape[0] // EW_BLOCK for p in peers]
    starts = [sum(blocks[:p]) for p in range(len(peers))]
    tok = pl.BlockSpec((1, EW_BLOCK, D), lambda j: (0, j, 0))

    def piece_spec(start, n):
        return pl.BlockSpec((EW_BLOCK, D), lambda j: (jnp.clip(j - start, 0, n - 1), 0))

    return pl.pallas_call(
        functools.partial(_final_kernel, normalize=normalize, starts=tuple(starts)),
        grid=(S // EW_BLOCK,),
        in_specs=[tok,
                  pl.BlockSpec((1, N_MOD, D), lambda j: (b, 0, 0)),
                  pl.BlockSpec((1, D), lambda j: (0, 0))]
                 + [piece_spec(s, n) for s, n in zip(starts, blocks)],
        out_specs=tok,
        out_shape=jax.ShapeDtypeStruct((1, S, D), jnp.float32),
        compiler_params=pltpu.CompilerParams(vmem_limit_bytes=VMEM_LIMIT_BYTES),
        name="final_norm",
    )(x1, mod3, g_final, *peers)


def _token_split(b, n_seq, S):
    unit = S // 16
    if b == n_seq - 1 and n_seq > 1:
        sc, c = 11, 2
    elif b == 0:
        sc, c = 14, 0
    else:
        sc, c = 13, 0
    n_sc, n_c = sc * unit, c * unit
    n_a = ((S - n_sc - n_c) // 2) // EW_BLOCK * EW_BLOCK
    return n_sc, n_a, S - n_sc - n_c - n_a, n_c


def kernel(x, c, w_ada, b_ada, g_norm1, w_in, w_pool, pool_scale, sgu_ln_g, sgu_ln_b, w_spatial, b_spatial,
           w_out, g_norm2, w_query, sub_keys, expert_down, expert_up, g_final):
    B, S, D = x.shape
    depth = w_ada.shape[0]
    bf16 = jnp.bfloat16
    band = jnp.asarray(_pool_band(), bf16)
    c_pad = jnp.pad(c, ((0, 8 - B), (0, 0)))
    for l in range(depth):
        mod = _ada_mod(c_pad, w_ada[l].astype(bf16), b_ada[l][None, :])[:B]
        mod3 = mod.reshape(B, N_MOD, D)
        mix_w = (g_norm1[l][None, :], w_in[l].astype(bf16), band, w_pool[l].astype(bf16),
                 pool_scale[l][None, :], sgu_ln_g[l][None, :], sgu_ln_b[l][None, :], w_spatial[l],
                 b_spatial[l].T, w_out[l].astype(bf16), g_norm2[l][None, :])
        keys = sub_keys[l].reshape(2 * PEER_HEADS, N_KEYS, HALF_KEY).astype(bf16)
        wq = w_query[l].astype(bf16)
        down_packed = _pack_rows(expert_down[l])
        up_packed = _pack_rows(expert_up[l])
        table2 = jnp.concatenate([down_packed, up_packed], axis=1)
        last = l + 1 == depth
        outs = []
        pending = None
        mod3_b = mod3
        for b in range(B):
            n_sc, n_a, n_b, n_c = _token_split(b, B, S)
            x1, h2 = _mixer(x, b, 1, mod3_b, *mix_w)
            h2f = h2.reshape(S, D)
            idx, gate = _query_topk(h2f, wq, keys)
            tc_a = _peer_tc(h2f, idx, gate, table2, n_sc, n_a)
            pre = _peer_down(_pack_rows(h2f[:n_sc]), idx, down_packed, n_sc)
            gate_b = gate
            if pending is not None:
                p_x1, p_peers, p_seq = pending
                p_peers[0], tc_a = lax.optimization_barrier((p_peers[0], tc_a))
                out_prev = _residual(p_x1, p_peers, p_seq, mod3, g_final[None, :], last)
                outs.append(out_prev)
                gate_b, _ = lax.optimization_barrier((gate, out_prev))
            tc_b = _peer_tc(h2f, idx, gate_b, table2, n_sc + n_a, n_b)
            pre, tc_a, tc_b = lax.optimization_barrier((pre, tc_a, tc_b))
            wgt = _gate_act(pre, gate)
            peers = [_peer_up(_pack_dup(wgt), idx, up_packed, n_sc), tc_a, tc_b]
            if n_c:
                gate_c, _ = lax.optimization_barrier((gate, wgt))
                peers.append(_peer_tc(h2f, idx, gate_c, table2, n_sc + n_a + n_b, n_c))
            mod3_b, _ = lax.optimization_barrier((mod3, wgt))
            pending = (x1, peers, b)
        p_x1, p_peers, p_seq = pending
        outs.append(_residual(p_x1, p_peers, p_seq, mod3, g_final[None, :], last))
        x = jnp.concatenate(outs, axis=0)
    return x
```

```python
import functools
import math

import jax
import jax.numpy as jnp
import numpy as np
from jax import lax
from jax.experimental import pallas as pl
from jax.experimental.pallas import tpu as pltpu
from jax.experimental.pallas import tpu_sc as plsc

D_MODEL = 1024
POOL_WIDTH = 512
SGU_WIDTH = 512
POOL_WINDOWS = (2, 4, 8, 16)
GROUP_DIM = 128
CHUNK = 128
SGU_HEADS = 4
IN_PROJ_WIDTH = POOL_WIDTH + 2 * SGU_WIDTH
PEER_HEADS = 8
N_KEYS = 128
HALF_KEY = 128
TOPK = 16
N_MOD = 6
EPS = 1e-6

MIX_BLOCK = 512
TOPK_BLOCK = 256
EW_BLOCK = 256
VMEM_LIMIT_BYTES = 48 * 1024 * 1024

SC_LANES = 16
SC_TOKENS = 32
SC_RING = 8
N_SEL = PEER_HEADS * TOPK
TC_TOKENS = 8

N_CAND = 16 + 7 * 8 + 8


def _pool_band():
    t = np.arange(CHUNK)[:, None]
    j = np.arange(2 * CHUNK)[None, :]
    bands = []
    for w in POOL_WINDOWS:
        m = (j > CHUNK + t - w) & (j <= CHUNK + t)
        bands.append(np.where(m, 1.0 / w, 0.0))
    return np.stack(bands).astype(np.float32)


def _bf16_dot(a, b):
    return jnp.dot(a.astype(jnp.bfloat16), b.astype(jnp.bfloat16), preferred_element_type=jnp.float32)


def _ada_kernel(c_ref, w_ref, b_ref, o_ref):
    c = c_ref[...]
    c_act = c * jax.nn.sigmoid(c)
    o_ref[...] = _bf16_dot(c_act, w_ref[...]) + b_ref[...]


def _ada_mod(c_pad, w_ada, b_ada):
    rows = c_pad.shape[0]
    return pl.pallas_call(
        _ada_kernel,
        grid=(N_MOD,),
        in_specs=[
            pl.BlockSpec((rows, D_MODEL), lambda i: (0, 0)),
            pl.BlockSpec((D_MODEL, D_MODEL), lambda i: (0, i)),
            pl.BlockSpec((1, D_MODEL), lambda i: (0, i)),
        ],
        out_specs=pl.BlockSpec((rows, D_MODEL), lambda i: (0, i)),
        out_shape=jax.ShapeDtypeStruct((rows, N_MOD * D_MODEL), jnp.float32),
        name="ada_mod",
    )(c_pad, w_ada, b_ada)


def _rms(x, g):
    return x * lax.rsqrt(jnp.mean(x * x, axis=-1, keepdims=True) + EPS) * g


def _mix_kernel(x_ref, mod_ref, g1_ref, win_ref, band_ref, wpool_ref, pscale_ref, lng_ref, lnb_ref,
                wsp_ref, bsp_ref, wout_ref, g2_ref, x1_ref, h2_ref, aext_ref, mixed_ref):
    j = pl.program_id(1)
    x = x_ref[0]
    shift1, scale1, gate1 = mod_ref[0, 0:1, :], mod_ref[0, 1:2, :], mod_ref[0, 2:3, :]
    shift2, scale2 = mod_ref[0, 3:4, :], mod_ref[0, 4:5, :]

    h = _rms(x, g1_ref[...]) * (1.0 + scale1) + shift1
    proj = _bf16_dot(h, win_ref[...])
    a = proj[:, :POOL_WIDTH]
    u = proj[:, POOL_WIDTH:POOL_WIDTH + SGU_WIDTH]
    v = proj[:, POOL_WIDTH + SGU_WIDTH:]

    @pl.when(j == 0)
    def _():
        aext_ref[0:CHUNK, :] = jnp.zeros((CHUNK, POOL_WIDTH), jnp.float32)

    aext_ref[CHUNK:, :] = a

    mu = jnp.mean(v, axis=-1, keepdims=True)
    vc = v - mu
    vn = vc * lax.rsqrt(jnp.mean(vc * vc, axis=-1, keepdims=True) + EPS) * lng_ref[...] + lnb_ref[...]

    row = lax.broadcasted_iota(jnp.int32, (CHUNK, CHUNK), 0)
    col = lax.broadcasted_iota(jnp.int32, (CHUNK, CHUNK), 1)
    tril = col <= row
    t_col = lax.broadcasted_iota(jnp.int32, (CHUNK, 1), 0)

    for c in range(MIX_BLOCK // CHUNK):
        r0 = c * CHUNK
        pos = (j * MIX_BLOCK + r0 + 1 + t_col).astype(jnp.float32)
        for g, w in enumerate(POOL_WINDOWS):
            c0 = g * GROUP_DIM
            seg = aext_ref[r0:r0 + 2 * CHUNK, c0:c0 + GROUP_DIM]
            hi = seg.astype(jnp.bfloat16)
            lo = (seg - hi.astype(jnp.float32)).astype(jnp.bfloat16)
            band = band_ref[g]
            win = (jnp.dot(band, hi, preferred_element_type=jnp.float32)
                   + jnp.dot(band, lo, preferred_element_type=jnp.float32))
            mean = win * (float(w) / jnp.minimum(pos, float(w)))
            pooled = mean - seg[CHUNK:, :]
            y = _bf16_dot(pooled, wpool_ref[g]) * pscale_ref[:, c0:c0 + GROUP_DIM]
            mixed_ref[r0:r0 + CHUNK, c0:c0 + GROUP_DIM] = y.astype(jnp.bfloat16)
        for hh in range(SGU_HEADS):
            c0 = hh * GROUP_DIM
            wm = jnp.where(tril, wsp_ref[hh], 0.0)
            m = _bf16_dot(wm, vn[r0:r0 + CHUNK, c0:c0 + GROUP_DIM]) + bsp_ref[:, hh:hh + 1]
            s = u[r0:r0 + CHUNK, c0:c0 + GROUP_DIM] * m
            mixed_ref[r0:r0 + CHUNK, POOL_WIDTH + c0:POOL_WIDTH + c0 + GROUP_DIM] = s.astype(jnp.bfloat16)

    aext_ref[0:CHUNK, :] = aext_ref[MIX_BLOCK:MIX_BLOCK + CHUNK, :]

    x1 = x + gate1 * jnp.dot(mixed_ref[...], wout_ref[...], preferred_element_type=jnp.float32)
    x1_ref[0] = x1
    h2_ref[0] = _rms(x1, g2_ref[...]) * (1.0 + scale2) + shift2


def _mixer(x, b0, n_seq, mod3, g1, w_in, band, w_pool, pool_scale, ln_g, ln_b, w_sp, b_sp_t, w_out, g2):
    _, S, D = x.shape
    const2 = lambda b, j: (0, 0)
    const3 = lambda b, j: (0, 0, 0)
    tok = pl.BlockSpec((1, MIX_BLOCK, D), lambda b, j: (b, j, 0))
    return pl.pallas_call(
        _mix_kernel,
        grid=(n_seq, S // MIX_BLOCK),
        in_specs=[
            pl.BlockSpec((1, MIX_BLOCK, D), lambda b, j: (b0 + b, j, 0)),
            pl.BlockSpec((1, N_MOD, D), lambda b, j: (b0 + b, 0, 0)),
            pl.BlockSpec((1, D), const2),
            pl.BlockSpec((D, IN_PROJ_WIDTH), const2),
            pl.BlockSpec((len(POOL_WINDOWS), CHUNK, 2 * CHUNK), const3),
            pl.BlockSpec((len(POOL_WINDOWS), GROUP_DIM, GROUP_DIM), const3),
            pl.BlockSpec((1, POOL_WIDTH), const2),
            pl.BlockSpec((1, SGU_WIDTH), const2),
            pl.BlockSpec((1, SGU_WIDTH), const2),
            pl.BlockSpec((SGU_HEADS, CHUNK, CHUNK), const3),
            pl.BlockSpec((CHUNK, SGU_HEADS), const2),
            pl.BlockSpec((D, D), const2),
            pl.BlockSpec((1, D), const2),
        ],
        out_specs=[tok, tok],
        out_shape=[jax.ShapeDtypeStruct((n_seq, S, D), jnp.float32)] * 2,
        scratch_shapes=[
            pltpu.VMEM((MIX_BLOCK + CHUNK, POOL_WIDTH), jnp.float32),
            pltpu.VMEM((MIX_BLOCK, D), jnp.bfloat16),
        ],
        compiler_params=pltpu.CompilerParams(
            dimension_semantics=("arbitrary", "arbitrary"), vmem_limit_bytes=VMEM_LIMIT_BYTES),
        name="mixer",
    )(x, mod3, g1, w_in, band, w_pool, pool_scale, ln_g, ln_b, w_sp, b_sp_t, w_out, g2)


def _extract_top(s, ids, payload, n_out):
    vals, pays = [], []
    for _ in range(n_out):
        m = jnp.max(s, axis=0, keepdims=True)
        pick = jnp.min(jnp.where(s == m, ids, jnp.float32(1e9)), axis=0, keepdims=True)
        sel = ids == pick
        vals.append(m)
        pays.append(pick if payload is None
                    else jnp.max(jnp.where(sel, payload, -1.0), axis=0, keepdims=True))
        s = jnp.where(sel, -jnp.inf, s)
    return jnp.concatenate(vals, axis=0), jnp.concatenate(pays, axis=0)


def _topk_kernel(h2_ref, wq_ref, keys_ref, idx_ref, gate_ref):
    T = TOPK_BLOCK
    q = _bf16_dot(h2_ref[...], wq_ref[...])
    key_ids = lax.broadcasted_iota(jnp.int32, (N_KEYS, T), 0).astype(jnp.float32)
    row = lax.broadcasted_iota(jnp.int32, (N_CAND, T), 0)
    flat = jnp.where(row < 16, row,
                     jnp.where(row < 72, jnp.right_shift(row - 8, 3) * 16 + jnp.bitwise_and(row, 7),
                               (row - 64) * 16)).astype(jnp.float32)
    idx_rows, gate_rows = [], []
    for hd in range(PEER_HEADS):
        tops = []
        for p in range(2):
            hp = hd * 2 + p
            qs = q[:, hp * HALF_KEY:(hp + 1) * HALF_KEY].astype(jnp.bfloat16)
            st = lax.dot_general(keys_ref[hp], qs, (((1,), (1,)), ((), ())),
                                 preferred_element_type=jnp.float32)
            tops.append(_extract_top(st, key_ids, None, TOPK))
        (s0, i0), (s1, i1) = tops
        i0 = i0 * float(N_KEYS)
        cand = jnp.concatenate(
            [s0[0:1] + s1] + [s0[a:a + 1] + s1[0:8] for a in range(1, 8)] + [s0[8:16] + s1[0:1]], axis=0)
        cidx = jnp.concatenate(
            [i0[0:1] + i1] + [i0[a:a + 1] + i1[0:8] for a in range(1, 8)] + [i0[8:16] + i1[0:1]],
            axis=0)
        best, sel_idx = _extract_top(cand, flat, cidx, TOPK)
        e = jnp.exp(best - jnp.max(best, axis=0, keepdims=True))
        gate_rows.append(e / jnp.sum(e, axis=0, keepdims=True))
        idx_rows.append(sel_idx)
    idx_ref[...] = jnp.concatenate(idx_rows, axis=0).T.astype(jnp.int32)
    gate_ref[...] = jnp.concatenate(gate_rows, axis=0).T


def _query_topk(h2, w_query, keys, row0, N):
    D = h2.shape[1]
    blk0 = row0 // TOPK_BLOCK
    tok = pl.BlockSpec((TOPK_BLOCK, N_SEL), lambda i: (i, 0))
    return pl.pallas_call(
        _topk_kernel,
        grid=(N // TOPK_BLOCK,),
        in_specs=[
            pl.BlockSpec((TOPK_BLOCK, D), lambda i: (blk0 + i, 0)),
            pl.BlockSpec((D, 2 * PEER_HEADS * HALF_KEY), lambda i: (0, 0)),
            pl.BlockSpec((2 * PEER_HEADS, N_KEYS, HALF_KEY), lambda i: (0, 0, 0)),
        ],
        out_specs=[tok, tok],
        out_shape=[jax.ShapeDtypeStruct((N, N_SEL), jnp.int32), jax.ShapeDtypeStruct((N, N_SEL), jnp.float32)],
        compiler_params=pltpu.CompilerParams(
            dimension_semantics=("arbitrary",), vmem_limit_bytes=VMEM_LIMIT_BYTES),
        name="query_topk",
    )(h2, w_query, keys)


def _sc_mesh_and_workers():
    info = plsc.get_sparse_core_info()
    assert info.num_lanes == SC_LANES
    mesh = plsc.VectorSubcoreMesh(core_axis_name="core", subcore_axis_name="subcore")
    return mesh, info.num_cores, info.num_cores * info.num_subcores


def _gather_ring(tab_hbm, idx_v, rows_v, sems, consume, n_tokens):
    look = SC_RING - 1

    def copy(t, hd):
        slot = hd % SC_RING
        return pltpu.make_async_copy(
            tab_hbm.at[idx_v.at[t, pl.ds(hd * TOPK, TOPK)]], rows_v.at[slot], sems.at[slot])

    for hd in range(look):
        copy(0, hd).start()

    @pl.loop(0, n_tokens)
    def _(t):
        for hd in range(PEER_HEADS):
            nxt = hd + look
            if nxt < PEER_HEADS:
                copy(t, nxt).start()
            else:
                @pl.when(t + 1 < n_tokens)
                def _():
                    copy(t + 1, nxt - PEER_HEADS).start()
            copy(t, hd).wait()
            consume(t, hd, rows_v.at[hd % SC_RING])


def _for_token_blocks(first, per_worker, block):
    n_full, tail = divmod(per_worker, SC_TOKENS)
    assert tail % 8 == 0, "HBM row offsets of the staged slices must stay 8-aligned"

    @pl.loop(0, n_full)
    def _(blk):
        block(first + blk * SC_TOKENS, SC_TOKENS)

    if tail:
        block(first + n_full * SC_TOKENS, tail)


SC_PAIR_WORDS = D_MODEL // 2
SC_STEPS = SC_PAIR_WORDS // SC_LANES
SC_BF16_SUM = 4


def _bf16_bits(a):
    return lax.bitcast_convert_type(a.astype(jnp.bfloat16), jnp.uint16).astype(jnp.uint32)


def _pack_rows(a):
    bits = _bf16_bits(a)
    return lax.bitcast_convert_type(bits[:, :SC_PAIR_WORDS] | (bits[:, SC_PAIR_WORDS:] << 16), jnp.int32)


def _pack_dup(a):
    bits = _bf16_bits(a)
    return lax.bitcast_convert_type(bits | (bits << 16), jnp.int32)


def _unpack_pair(words):
    lo = lax.bitcast_convert_type(lax.shift_left(words, jnp.int32(16)), jnp.float32)
    hi = lax.bitcast_convert_type(lax.bitwise_and(words, jnp.int32(-65536)), jnp.float32)
    return lo, hi


def _as_bf16(words):
    return plsc.bitcast(words, jnp.bfloat16)


def _widen_sum(products):
    return _unpack_pair(plsc.bitcast(_tree_sum(products), jnp.int32))


def _tree_sum(terms):
    terms = list(terms)
    while len(terms) > 1:
        terms = [terms[i] + terms[i + 1] for i in range(0, len(terms) - 1, 2)] + (
            [terms[-1]] if len(terms) % 2 else [])
    return terms[0]


def _peer_down(h_packed, idx, packed, N):
    mesh, n_cores, n_workers = _sc_mesh_and_workers()
    per_worker = N // n_workers

    def body(h_hbm, idx_hbm, tab_hbm, out_hbm, idx_v, h_v, rows_v, out_v, tr_v, sems):
        wid = lax.axis_index("subcore") * n_cores + lax.axis_index("core")
        lane = lax.iota(jnp.int32, SC_LANES)

        def consume(t, hd, rows):
            def dot_step(g, accs):
                runs = [pl.ds((g * SC_BF16_SUM + i) * SC_LANES, SC_LANES) for i in range(SC_BF16_SUM)]
                hb = [_as_bf16(h_v[t, r]) for r in runs]
                out = []
                for k, acc in enumerate(accs):
                    lo, hi = _widen_sum([_as_bf16(rows[k, r]) * hbi for r, hbi in zip(runs, hb)])
                    out.append(acc + (lo + hi))
                return tuple(out)

            accs = lax.fori_loop(0, SC_STEPS // SC_BF16_SUM, dot_step,
                                 tuple(jnp.zeros((SC_LANES,), jnp.float32) for _ in range(TOPK)))
            for k in range(TOPK):
                tr_v[k, :] = accs[k]
            tot = jnp.zeros((SC_LANES,), jnp.float32)
            for l in range(SC_LANES):
                tot = tot + plsc.load_gather(tr_v, [lane, jnp.full((SC_LANES,), l, jnp.int32)])
            out_v[t, pl.ds(hd * TOPK, TOPK)] = tot

        def block(base, n):
            pltpu.sync_copy(idx_hbm.at[pl.ds(base, n)], idx_v.at[pl.ds(0, n)])
            pltpu.sync_copy(h_hbm.at[pl.ds(base, n)], h_v.at[pl.ds(0, n)])
            _gather_ring(tab_hbm, idx_v, rows_v, sems, consume, n)
            pltpu.sync_copy(out_v.at[pl.ds(0, n)], out_hbm.at[pl.ds(base, n)])

        _for_token_blocks(wid * per_worker, per_worker, block)

    return pl.kernel(
        body,
        out_type=jax.ShapeDtypeStruct((N, N_SEL), jnp.float32),
        mesh=mesh,
        scratch_types=[
            pltpu.VMEM((SC_TOKENS, N_SEL), jnp.int32),
            pltpu.VMEM((SC_TOKENS, SC_PAIR_WORDS), jnp.int32),
            pltpu.VMEM((SC_RING, TOPK, SC_PAIR_WORDS), jnp.int32),
            pltpu.VMEM((SC_TOKENS, N_SEL), jnp.float32),
            pltpu.VMEM((TOPK, SC_LANES), jnp.float32),
            pltpu.SemaphoreType.DMA((SC_RING,)),
        ],
        compiler_params=pltpu.CompilerParams(needs_layout_passes=False),
        name="peer_down",
    )(h_packed, idx, packed)


def _peer_up(w_dup, idx, packed, N):
    D = D_MODEL
    mesh, n_cores, n_workers = _sc_mesh_and_workers()
    per_worker = N // n_workers

    def body(w_hbm, idx_hbm, tab_hbm, out_hbm, idx_v, w_v, rows_v, out_v, sems):
        wid = lax.axis_index("subcore") * n_cores + lax.axis_index("core")

        def consume(t, hd, rows):
            t_vec = jnp.full((SC_LANES,), t, jnp.int32)
            ws = [_as_bf16(plsc.load_gather(w_v, [t_vec, jnp.full((SC_LANES,), hd * TOPK + k, jnp.int32)]))
                  for k in range(TOPK)]

            @plsc.parallel_loop(0, SC_STEPS)
            def _(s):
                run = pl.ds(s * SC_LANES, SC_LANES)
                parts = [_widen_sum([_as_bf16(rows[k, run]) * ws[k] for k in range(g, g + SC_BF16_SUM)])
                         for g in range(0, TOPK, SC_BF16_SUM)]
                for half, c in enumerate((s * SC_LANES, SC_PAIR_WORDS + s * SC_LANES)):
                    sl = pl.ds(c, SC_LANES)
                    terms = [p[half] for p in parts]
                    if hd > 0:
                        terms.append(out_v[t, sl])
                    out_v[t, sl] = _tree_sum(terms)

        def block(base, n):
            pltpu.sync_copy(idx_hbm.at[pl.ds(base, n)], idx_v.at[pl.ds(0, n)])
            pltpu.sync_copy(w_hbm.at[pl.ds(base, n)], w_v.at[pl.ds(0, n)])
            _gather_ring(tab_hbm, idx_v, rows_v, sems, consume, n)
            pltpu.sync_copy(out_v.at[pl.ds(0, n)], out_hbm.at[pl.ds(base, n)])

        _for_token_blocks(wid * per_worker, per_worker, block)

    return pl.kernel(
        body,
        out_type=jax.ShapeDtypeStruct((N, D), jnp.float32),
        mesh=mesh,
        scratch_types=[
            pltpu.VMEM((SC_TOKENS, N_SEL), jnp.int32),
            pltpu.VMEM((SC_TOKENS, N_SEL), jnp.int32),
            pltpu.VMEM((SC_RING, TOPK, SC_PAIR_WORDS), jnp.int32),
            pltpu.VMEM((SC_TOKENS, D), jnp.float32),
            pltpu.SemaphoreType.DMA((SC_RING,)),
        ],
        compiler_params=pltpu.CompilerParams(needs_layout_passes=False),
        name="peer_up",
    )(w_dup, idx, packed)


def _peer_tc_kernel(idx_cur, idx_nxt, h_ref, gate_ref, tab_hbm, o_ref, gbuf, sems):
    i = pl.program_id(0)
    n = pl.num_programs(0)
    groups = N_SEL // 8

    def issue(idx_ref, slot):
        def body(g, carry):
            for j in range(8):
                e = idx_ref[g * 8 + j]
                pltpu.make_async_copy(
                    tab_hbm.at[pl.ds(e, 1), :], gbuf.at[slot, g, pl.ds(j, 1), :], sems.at[slot]).start()
            return carry
        lax.fori_loop(0, TC_TOKENS * groups, body, 0)

    @pl.when(i == 0)
    def _():
        issue(idx_cur, 0)

    @pl.when(i + 1 < n)
    def _():
        issue(idx_nxt, (i + 1) % 2)

    slot = i % 2
    pltpu.make_async_copy(gbuf.at[slot], gbuf.at[slot], sems.at[slot]).wait()

    half = SC_PAIR_WORDS

    def rows(t, c0):
        w = gbuf[slot, t * groups:(t + 1) * groups, :, c0:c0 + half].reshape(N_SEL, half)
        lo = lax.bitcast_convert_type(lax.shift_left(w, jnp.int32(16)), jnp.float32)
        hi = lax.bitcast_convert_type(lax.bitwise_and(w, jnp.int32(-65536)), jnp.float32)
        return lo, hi

    lane = lax.broadcasted_iota(jnp.int32, (N_SEL, 128), 1)
    pre_all = jnp.zeros((N_SEL, 128), jnp.float32)
    for t in range(TC_TOKENS):
        lo, hi = rows(t, 0)
        prod = lo * h_ref[t:t + 1, 0:half] + hi * h_ref[t:t + 1, half:D_MODEL]
        part = prod[:, 0:128]
        for c in range(1, half // 128):
            part = part + prod[:, c * 128:(c + 1) * 128]
        pre = jnp.sum(part, axis=1, keepdims=True)
        pre_all = jnp.where(lane == t, pre, pre_all)
    act = 0.5 * pre_all * (1.0 + lax.erf(pre_all * (1.0 / math.sqrt(2.0))))
    gate_t = jnp.concatenate([gate_ref[...]] * (128 // TC_TOKENS), axis=0).T
    wgt = gate_t * act
    for t in range(TC_TOKENS):
        w_col = jnp.sum(jnp.where(lane == t, wgt, 0.0), axis=1, keepdims=True)
        lo, hi = rows(t, half)
        o_ref[t:t + 1, 0:half] = jnp.sum(lo * w_col, axis=0, keepdims=True)
        o_ref[t:t + 1, half:D_MODEL] = jnp.sum(hi * w_col, axis=0, keepdims=True)


def _peer_tc(h2, h_row0, idx, gate, table2, row0, n_tokens):
    D = h2.shape[1]
    steps = n_tokens // TC_TOKENS
    blk0 = row0 // TC_TOKENS
    h_blk0 = (h_row0 + row0) // TC_TOKENS
    idx_flat = idx.reshape(-1)
    return pl.pallas_call(
        _peer_tc_kernel,
        grid=(steps,),
        in_specs=[
            pl.BlockSpec((TC_TOKENS * N_SEL,), lambda i: (blk0 + i,), memory_space=pltpu.SMEM),
            pl.BlockSpec((TC_TOKENS * N_SEL,), lambda i: (blk0 + jnp.minimum(i + 1, steps - 1),),
                         memory_space=pltpu.SMEM),
            pl.BlockSpec((TC_TOKENS, D), lambda i: (h_blk0 + i, 0)),
            pl.BlockSpec((TC_TOKENS, N_SEL), lambda i: (blk0 + i, 0)),
            pl.BlockSpec(memory_space=pl.ANY),
        ],
        out_specs=pl.BlockSpec((TC_TOKENS, D), lambda i: (i, 0)),
        out_shape=jax.ShapeDtypeStruct((n_tokens, D), jnp.float32),
        scratch_shapes=[
            pltpu.VMEM((2, TC_TOKENS * N_SEL // 8, 8, 2 * SC_PAIR_WORDS), jnp.int32),
            pltpu.SemaphoreType.DMA((2,)),
        ],
        compiler_params=pltpu.CompilerParams(
            dimension_semantics=("arbitrary",), vmem_limit_bytes=VMEM_LIMIT_BYTES),
        name="peer_tc",
    )(idx_flat, idx_flat, h2, gate, table2)


def _gate_act_kernel(pre_ref, gate_ref, o_ref):
    p = pre_ref[...]
    act = 0.5 * p * (1.0 + lax.erf(p * (1.0 / math.sqrt(2.0))))
    o_ref[...] = gate_ref[...] * act


def _gate_act(pre, gate):
    N = pre.shape[0]
    spec = pl.BlockSpec((EW_BLOCK, N_SEL), lambda i: (i, 0))
    return pl.pallas_call(
        _gate_act_kernel, grid=(N // EW_BLOCK,), in_specs=[spec, spec], out_specs=spec,
        out_shape=jax.ShapeDtypeStruct((N, N_SEL), jnp.float32), name="gate_act",
    )(pre, gate)


def _final_kernel(x1_ref, mod_ref, g_ref, *refs, normalize, starts):
    peer_refs, o_ref = refs[:-1], refs[-1]
    gate2 = mod_ref[0, 5:6, :]
    peer = peer_refs[0][...]
    for p in range(1, len(peer_refs)):
        peer = jnp.where(pl.program_id(0) >= starts[p], peer_refs[p][...], peer)
    y = x1_ref[0] + gate2 * peer
    o_ref[...] = _rms(y, g_ref[...]) if normalize else y


def _residual(x1, row0, peers, b, mod3, g_final, normalize):
    D = x1.shape[2]
    blocks = [p.shape[0] // EW_BLOCK for p in peers]
    starts = [sum(blocks[:p]) for p in range(len(peers))]
    blk0 = row0 // EW_BLOCK

    def piece_spec(start, n):
        return pl.BlockSpec((EW_BLOCK, D), lambda j: (jnp.clip(j - start, 0, n - 1), 0))

    return pl.pallas_call(
        functools.partial(_final_kernel, normalize=normalize, starts=tuple(starts)),
        grid=(sum(blocks),),
        in_specs=[pl.BlockSpec((1, EW_BLOCK, D), lambda j: (0, blk0 + j, 0)),
                  pl.BlockSpec((1, N_MOD, D), lambda j: (b, 0, 0)),
                  pl.BlockSpec((1, D), lambda j: (0, 0))]
                 + [piece_spec(s, n) for s, n in zip(starts, blocks)],
        out_specs=pl.BlockSpec((EW_BLOCK, D), lambda j: (j, 0)),
        out_shape=jax.ShapeDtypeStruct((sum(blocks) * EW_BLOCK, D), jnp.float32),
        compiler_params=pltpu.CompilerParams(vmem_limit_bytes=VMEM_LIMIT_BYTES),
        name="final_norm",
    )(x1, mod3, g_final, *peers)


def _token_split(u, n_units, n):
    unit = n // 16
    if u == n_units - 1 and n_units > 1:
        sc, c = 11, 2
    elif u == 0:
        sc, c = 14, 0
    else:
        sc, c = 13, 0
    n_sc, n_c = sc * unit, c * unit
    n_a = ((n - n_sc - n_c) // 2) // EW_BLOCK * EW_BLOCK
    return n_sc, n_a, n - n_sc - n_c - n_a, n_c


def _token_ranges(n_seq, S):
    ranges = [(0, 0, S // 2), (0, S // 2, S - S // 2)] if S % (2 * 16 * EW_BLOCK) == 0 else [(0, 0, S)]
    return ranges + [(b, 0, S) for b in range(1, n_seq)]


def kernel(x, c, w_ada, b_ada, g_norm1, w_in, w_pool, pool_scale, sgu_ln_g, sgu_ln_b, w_spatial, b_spatial,
           w_out, g_norm2, w_query, sub_keys, expert_down, expert_up, g_final):
    B, S, D = x.shape
    depth = w_ada.shape[0]
    bf16 = jnp.bfloat16
    band = jnp.asarray(_pool_band(), bf16)
    c_pad = jnp.pad(c, ((0, 8 - B), (0, 0)))
    for l in range(depth):
        mod = _ada_mod(c_pad, w_ada[l].astype(bf16), b_ada[l][None, :])[:B]
        mod3 = mod.reshape(B, N_MOD, D)
        mix_w = (g_norm1[l][None, :], w_in[l].astype(bf16), band, w_pool[l].astype(bf16),
                 pool_scale[l][None, :], sgu_ln_g[l][None, :], sgu_ln_b[l][None, :], w_spatial[l],
                 b_spatial[l].T, w_out[l].astype(bf16), g_norm2[l][None, :])
        keys = sub_keys[l].reshape(2 * PEER_HEADS, N_KEYS, HALF_KEY).astype(bf16)
        wq = w_query[l].astype(bf16)
        down_packed = _pack_rows(expert_down[l])
        up_packed = _pack_rows(expert_up[l])
        table2 = jnp.concatenate([down_packed, up_packed], axis=1)
        last = l + 1 == depth
        outs = []
        pending = None
        mod3_b = mod3
        seq_done = -1
        ranges = _token_ranges(B, S)
        for u, (b, r0, n) in enumerate(ranges):
            n_sc, n_a, n_b, n_c = _token_split(u, len(ranges), n)
            if b != seq_done:
                x1, h2 = _mixer(x, b, 1, mod3_b, *mix_w)
                h2f = h2.reshape(S, D)
                seq_done = b
            idx, gate = _query_topk(h2f, wq, keys, r0, n)
            tc_a = _peer_tc(h2f, r0, idx, gate, table2, n_sc, n_a)
            pre = _peer_down(_pack_rows(h2f[r0:r0 + n_sc]), idx, down_packed, n_sc)
            gate_b = gate
            if pending is not None:
                p_x1, p_r0, p_peers, p_seq = pending
                p_peers[0], tc_a = lax.optimization_barrier((p_peers[0], tc_a))
                out_prev = _residual(p_x1, p_r0, p_peers, p_seq, mod3, g_final[None, :], last)
                outs.append(out_prev)
                gate_b, _ = lax.optimization_barrier((gate, out_prev))
            tc_b = _peer_tc(h2f, r0, idx, gate_b, table2, n_sc + n_a, n_b)
            pre, tc_a, tc_b = lax.optimization_barrier((pre, tc_a, tc_b))
            wgt = _gate_act(pre, gate)
            peers = [_peer_up(_pack_dup(wgt), idx, up_packed, n_sc), tc_a, tc_b]
            if n_c:
                gate_c, _ = lax.optimization_barrier((gate, wgt))
                peers.append(_peer_tc(h2f, r0, idx, gate_c, table2, n_sc + n_a + n_b, n_c))
            mod3_b, _ = lax.optimization_barrier((mod3, wgt))
            pending = (x1, r0, peers, b)
        p_x1, p_r0, p_peers, p_seq = pending
        outs.append(_residual(p_x1, p_r0, p_peers, p_seq, mod3, g_final[None, :], last))
        x = jnp.concatenate(outs, axis=0).reshape(B, S, D)
    return x
```

```python
import functools
import math

import jax
import jax.numpy as jnp
import numpy as np
from jax import lax
from jax.experimental import pallas as pl
from jax.experimental.pallas import tpu as pltpu
from jax.experimental.pallas import tpu_sc as plsc

D_MODEL = 1024
POOL_WIDTH = 512
SGU_WIDTH = 512
POOL_WINDOWS = (2, 4, 8, 16)
GROUP_DIM = 128
CHUNK = 128
SGU_HEADS = 4
IN_PROJ_WIDTH = POOL_WIDTH + 2 * SGU_WIDTH
PEER_HEADS = 8
N_KEYS = 128
HALF_KEY = 128
TOPK = 16
N_MOD = 6
EPS = 1e-6

MIX_BLOCK = 512
TOPK_BLOCK = 256
EW_BLOCK = 256
VMEM_LIMIT_BYTES = 48 * 1024 * 1024

SC_LANES = 16
SC_TOKENS = 32
SC_RING = 8
N_SEL = PEER_HEADS * TOPK
TC_TOKENS = 8

N_CAND = 16 + 7 * 8 + 8


def _pool_band():
    t = np.arange(CHUNK)[:, None]
    j = np.arange(2 * CHUNK)[None, :]
    bands = []
    for w in POOL_WINDOWS:
        m = (j > CHUNK + t - w) & (j <= CHUNK + t)
        bands.append(np.where(m, 1.0 / w, 0.0))
    return np.stack(bands).astype(np.float32)


def _bf16_dot(a, b):
    return jnp.dot(a.astype(jnp.bfloat16), b.astype(jnp.bfloat16), preferred_element_type=jnp.float32)


def _ada_kernel(c_ref, w_ref, b_ref, o_ref):
    c = c_ref[...]
    c_act = c * jax.nn.sigmoid(c)
    o_ref[...] = _bf16_dot(c_act, w_ref[...]) + b_ref[...]


def _ada_mod(c_pad, w_ada, b_ada):
    rows = c_pad.shape[0]
    return pl.pallas_call(
        _ada_kernel,
        grid=(N_MOD,),
        in_specs=[
            pl.BlockSpec((rows, D_MODEL), lambda i: (0, 0)),
            pl.BlockSpec((D_MODEL, D_MODEL), lambda i: (0, i)),
            pl.BlockSpec((1, D_MODEL), lambda i: (0, i)),
        ],
        out_specs=pl.BlockSpec((rows, D_MODEL), lambda i: (0, i)),
        out_shape=jax.ShapeDtypeStruct((rows, N_MOD * D_MODEL), jnp.float32),
        name="ada_mod",
    )(c_pad, w_ada, b_ada)


def _rms(x, g):
    return x * lax.rsqrt(jnp.mean(x * x, axis=-1, keepdims=True) + EPS) * g


def _mix_kernel(x_ref, mod_ref, g1_ref, win_ref, band_ref, wpool_ref, pscale_ref, lng_ref, lnb_ref,
                wsp_ref, bsp_ref, wout_ref, g2_ref, x1_ref, h2_ref, aext_ref, mixed_ref):
    j = pl.program_id(1)
    x = x_ref[0]
    shift1, scale1, gate1 = mod_ref[0, 0:1, :], mod_ref[0, 1:2, :], mod_ref[0, 2:3, :]
    shift2, scale2 = mod_ref[0, 3:4, :], mod_ref[0, 4:5, :]

    h = _rms(x, g1_ref[...]) * (1.0 + scale1) + shift1
    proj = _bf16_dot(h, win_ref[...])
    a = proj[:, :POOL_WIDTH]
    u = proj[:, POOL_WIDTH:POOL_WIDTH + SGU_WIDTH]
    v = proj[:, POOL_WIDTH + SGU_WIDTH:]

    @pl.when(j == 0)
    def _():
        aext_ref[0:CHUNK, :] = jnp.zeros((CHUNK, POOL_WIDTH), jnp.float32)

    aext_ref[CHUNK:, :] = a

    mu = jnp.mean(v, axis=-1, keepdims=True)
    vc = v - mu
    vn = vc * lax.rsqrt(jnp.mean(vc * vc, axis=-1, keepdims=True) + EPS) * lng_ref[...] + lnb_ref[...]

    row = lax.broadcasted_iota(jnp.int32, (CHUNK, CHUNK), 0)
    col = lax.broadcasted_iota(jnp.int32, (CHUNK, CHUNK), 1)
    tril = col <= row
    t_col = lax.broadcasted_iota(jnp.int32, (CHUNK, 1), 0)

    for c in range(MIX_BLOCK // CHUNK):
        r0 = c * CHUNK
        pos = (j * MIX_BLOCK + r0 + 1 + t_col).astype(jnp.float32)
        for g, w in enumerate(POOL_WINDOWS):
            c0 = g * GROUP_DIM
            seg = aext_ref[r0:r0 + 2 * CHUNK, c0:c0 + GROUP_DIM]
            hi = seg.astype(jnp.bfloat16)
            lo = (seg - hi.astype(jnp.float32)).astype(jnp.bfloat16)
            band = band_ref[g]
            win = (jnp.dot(band, hi, preferred_element_type=jnp.float32)
                   + jnp.dot(band, lo, preferred_element_type=jnp.float32))
            mean = win * (float(w) / jnp.minimum(pos, float(w)))
            pooled = mean - seg[CHUNK:, :]
            y = _bf16_dot(pooled, wpool_ref[g]) * pscale_ref[:, c0:c0 + GROUP_DIM]
            mixed_ref[r0:r0 + CHUNK, c0:c0 + GROUP_DIM] = y.astype(jnp.bfloat16)
        for hh in range(SGU_HEADS):
            c0 = hh * GROUP_DIM
            wm = jnp.where(tril, wsp_ref[hh], 0.0)
            m = _bf16_dot(wm, vn[r0:r0 + CHUNK, c0:c0 + GROUP_DIM]) + bsp_ref[:, hh:hh + 1]
            s = u[r0:r0 + CHUNK, c0:c0 + GROUP_DIM] * m
            mixed_ref[r0:r0 + CHUNK, POOL_WIDTH + c0:POOL_WIDTH + c0 + GROUP_DIM] = s.astype(jnp.bfloat16)

    aext_ref[0:CHUNK, :] = aext_ref[MIX_BLOCK:MIX_BLOCK + CHUNK, :]

    x1 = x + gate1 * jnp.dot(mixed_ref[...], wout_ref[...], preferred_element_type=jnp.float32)
    x1_ref[0] = x1
    h2_ref[0] = _rms(x1, g2_ref[...]) * (1.0 + scale2) + shift2


def _mixer(x, b0, n_seq, mod3, g1, w_in, band, w_pool, pool_scale, ln_g, ln_b, w_sp, b_sp_t, w_out, g2):
    _, S, D = x.shape
    const2 = lambda b, j: (0, 0)
    const3 = lambda b, j: (0, 0, 0)
    tok = pl.BlockSpec((1, MIX_BLOCK, D), lambda b, j: (b, j, 0))
    return pl.pallas_call(
        _mix_kernel,
        grid=(n_seq, S // MIX_BLOCK),
        in_specs=[
            pl.BlockSpec((1, MIX_BLOCK, D), lambda b, j: (b0 + b, j, 0)),
            pl.BlockSpec((1, N_MOD, D), lambda b, j: (b0 + b, 0, 0)),
            pl.BlockSpec((1, D), const2),
            pl.BlockSpec((D, IN_PROJ_WIDTH), const2),
            pl.BlockSpec((len(POOL_WINDOWS), CHUNK, 2 * CHUNK), const3),
            pl.BlockSpec((len(POOL_WINDOWS), GROUP_DIM, GROUP_DIM), const3),
            pl.BlockSpec((1, POOL_WIDTH), const2),
            pl.BlockSpec((1, SGU_WIDTH), const2),
            pl.BlockSpec((1, SGU_WIDTH), const2),
            pl.BlockSpec((SGU_HEADS, CHUNK, CHUNK), const3),
            pl.BlockSpec((CHUNK, SGU_HEADS), const2),
            pl.BlockSpec((D, D), const2),
            pl.BlockSpec((1, D), const2),
        ],
        out_specs=[tok, tok],
        out_shape=[jax.ShapeDtypeStruct((n_seq, S, D), jnp.float32)] * 2,
        scratch_shapes=[
            pltpu.VMEM((MIX_BLOCK + CHUNK, POOL_WIDTH), jnp.float32),
            pltpu.VMEM((MIX_BLOCK, D), jnp.bfloat16),
        ],
        compiler_params=pltpu.CompilerParams(
            dimension_semantics=("arbitrary", "arbitrary"), vmem_limit_bytes=VMEM_LIMIT_BYTES),
        name="mixer",
    )(x, mod3, g1, w_in, band, w_pool, pool_scale, ln_g, ln_b, w_sp, b_sp_t, w_out, g2)


def _extract_top(s, ids, payload, n_out):
    vals, pays = [], []
    for _ in range(n_out):
        m = jnp.max(s, axis=0, keepdims=True)
        pick = jnp.min(jnp.where(s == m, ids, jnp.float32(1e9)), axis=0, keepdims=True)
        sel = ids == pick
        vals.append(m)
        pays.append(pick if payload is None
                    else jnp.max(jnp.where(sel, payload, -1.0), axis=0, keepdims=True))
        s = jnp.where(sel, -jnp.inf, s)
    return jnp.concatenate(vals, axis=0), jnp.concatenate(pays, axis=0)


def _topk_kernel(h2_ref, wq_ref, keys_ref, idx_ref, gate_ref):
    T = TOPK_BLOCK
    q = _bf16_dot(h2_ref[...], wq_ref[...])
    key_ids = lax.broadcasted_iota(jnp.int32, (N_KEYS, T), 0).astype(jnp.float32)
    row = lax.broadcasted_iota(jnp.int32, (N_CAND, T), 0)
    flat = jnp.where(row < 16, row,
                     jnp.where(row < 72, jnp.right_shift(row - 8, 3) * 16 + jnp.bitwise_and(row, 7),
                               (row - 64) * 16)).astype(jnp.float32)
    idx_rows, gate_rows = [], []
    for hd in range(PEER_HEADS):
        tops = []
        for p in range(2):
            hp = hd * 2 + p
            qs = q[:, hp * HALF_KEY:(hp + 1) * HALF_KEY].astype(jnp.bfloat16)
            st = lax.dot_general(keys_ref[hp], qs, (((1,), (1,)), ((), ())),
                                 preferred_element_type=jnp.float32)
            tops.append(_extract_top(st, key_ids, None, TOPK))
        (s0, i0), (s1, i1) = tops
        i0 = i0 * float(N_KEYS)
        cand = jnp.concatenate(
            [s0[0:1] + s1] + [s0[a:a + 1] + s1[0:8] for a in range(1, 8)] + [s0[8:16] + s1[0:1]], axis=0)
        cidx = jnp.concatenate(
            [i0[0:1] + i1] + [i0[a:a + 1] + i1[0:8] for a in range(1, 8)] + [i0[8:16] + i1[0:1]],
            axis=0)
        best, sel_idx = _extract_top(cand, flat, cidx, TOPK)
        e = jnp.exp(best - jnp.max(best, axis=0, keepdims=True))
        gate_rows.append(e / jnp.sum(e, axis=0, keepdims=True))
        idx_rows.append(sel_idx)
    idx_ref[...] = jnp.concatenate(idx_rows, axis=0).T.astype(jnp.int32)
    gate_ref[...] = jnp.concatenate(gate_rows, axis=0).T


def _query_topk(h2, w_query, keys, row0, N):
    D = h2.shape[1]
    blk0 = row0 // TOPK_BLOCK
    tok = pl.BlockSpec((TOPK_BLOCK, N_SEL), lambda i: (i, 0))
    return pl.pallas_call(
        _topk_kernel,
        grid=(N // TOPK_BLOCK,),
        in_specs=[
            pl.BlockSpec((TOPK_BLOCK, D), lambda i: (blk0 + i, 0)),
            pl.BlockSpec((D, 2 * PEER_HEADS * HALF_KEY), lambda i: (0, 0)),
            pl.BlockSpec((2 * PEER_HEADS, N_KEYS, HALF_KEY), lambda i: (0, 0, 0)),
        ],
        out_specs=[tok, tok],
        out_shape=[jax.ShapeDtypeStruct((N, N_SEL), jnp.int32), jax.ShapeDtypeStruct((N, N_SEL), jnp.float32)],
        compiler_params=pltpu.CompilerParams(
            dimension_semantics=("arbitrary",), vmem_limit_bytes=VMEM_LIMIT_BYTES),
        name="query_topk",
    )(h2, w_query, keys)


def _sc_mesh_and_workers():
    info = plsc.get_sparse_core_info()
    assert info.num_lanes == SC_LANES
    mesh = plsc.VectorSubcoreMesh(core_axis_name="core", subcore_axis_name="subcore")
    return mesh, info.num_cores, info.num_cores * info.num_subcores


def _gather_ring(tab_hbm, idx_v, rows_v, sems, consume, n_tokens):
    look = SC_RING - 1

    def copy(t, hd):
        slot = hd % SC_RING
        return pltpu.make_async_copy(
            tab_hbm.at[idx_v.at[t, pl.ds(hd * TOPK, TOPK)]], rows_v.at[slot], sems.at[slot])

    for hd in range(look):
        copy(0, hd).start()

    @pl.loop(0, n_tokens)
    def _(t):
        for hd in range(PEER_HEADS):
            nxt = hd + look
            if nxt < PEER_HEADS:
                copy(t, nxt).start()
            else:
                @pl.when(t + 1 < n_tokens)
                def _():
                    copy(t + 1, nxt - PEER_HEADS).start()
            copy(t, hd).wait()
            consume(t, hd, rows_v.at[hd % SC_RING])


def _for_token_blocks(first, per_worker, block):
    n_full, tail = divmod(per_worker, SC_TOKENS)
    assert tail % 8 == 0, "HBM row offsets of the staged slices must stay 8-aligned"

    @pl.loop(0, n_full)
    def _(blk):
        block(first + blk * SC_TOKENS, SC_TOKENS)

    if tail:
        block(first + n_full * SC_TOKENS, tail)


SC_PAIR_WORDS = D_MODEL // 2
SC_STEPS = SC_PAIR_WORDS // SC_LANES
SC_BF16_SUM = 4


def _bf16_bits(a):
    return lax.bitcast_convert_type(a.astype(jnp.bfloat16), jnp.uint16).astype(jnp.uint32)


def _pack_rows(a):
    bits = _bf16_bits(a)
    return lax.bitcast_convert_type(bits[:, :SC_PAIR_WORDS] | (bits[:, SC_PAIR_WORDS:] << 16), jnp.int32)


def _pack_dup(a):
    bits = _bf16_bits(a)
    return lax.bitcast_convert_type(bits | (bits << 16), jnp.int32)


def _unpack_pair(words):
    lo = lax.bitcast_convert_type(lax.shift_left(words, jnp.int32(16)), jnp.float32)
    hi = lax.bitcast_convert_type(lax.bitwise_and(words, jnp.int32(-65536)), jnp.float32)
    return lo, hi


SC_ROW_TILE = (SC_PAIR_WORDS // 128, 128)


def _sc_table(packed):
    return packed.reshape(packed.shape[0], *SC_ROW_TILE)


def _row_run(rows, k, s):
    shift = (128 // SC_LANES).bit_length() - 1
    return rows[k, jnp.right_shift(s, shift), pl.ds(jnp.bitwise_and(s, 128 // SC_LANES - 1) * SC_LANES, SC_LANES)]


def _as_bf16(words):
    return plsc.bitcast(words, jnp.bfloat16)


def _widen_sum(products):
    return _unpack_pair(plsc.bitcast(_tree_sum(products), jnp.int32))


def _tree_sum(terms):
    terms = list(terms)
    while len(terms) > 1:
        terms = [terms[i] + terms[i + 1] for i in range(0, len(terms) - 1, 2)] + (
            [terms[-1]] if len(terms) % 2 else [])
    return terms[0]


def _peer_down(h_packed, idx, packed, N):
    mesh, n_cores, n_workers = _sc_mesh_and_workers()
    per_worker = N // n_workers

    def body(h_hbm, idx_hbm, tab_hbm, out_hbm, idx_v, h_v, rows_v, out_v, tr_v, sems):
        wid = lax.axis_index("subcore") * n_cores + lax.axis_index("core")
        lane = lax.iota(jnp.int32, SC_LANES)

        def consume(t, hd, rows):
            def dot_step(g, accs):
                steps = [g * SC_BF16_SUM + i for i in range(SC_BF16_SUM)]
                hb = [_as_bf16(h_v[t, pl.ds(s * SC_LANES, SC_LANES)]) for s in steps]
                out = []
                for k, acc in enumerate(accs):
                    lo, hi = _widen_sum([_as_bf16(_row_run(rows, k, s)) * hbi for s, hbi in zip(steps, hb)])
                    out.append(acc + (lo + hi))
                return tuple(out)

            accs = lax.fori_loop(0, SC_STEPS // SC_BF16_SUM, dot_step,
                                 tuple(jnp.zeros((SC_LANES,), jnp.float32) for _ in range(TOPK)))
            for k in range(TOPK):
                tr_v[k, :] = accs[k]
            tot = jnp.zeros((SC_LANES,), jnp.float32)
            for l in range(SC_LANES):
                tot = tot + plsc.load_gather(tr_v, [lane, jnp.full((SC_LANES,), l, jnp.int32)])
            out_v[t, pl.ds(hd * TOPK, TOPK)] = tot

        def block(base, n):
            pltpu.sync_copy(idx_hbm.at[pl.ds(base, n)], idx_v.at[pl.ds(0, n)])
            pltpu.sync_copy(h_hbm.at[pl.ds(base, n)], h_v.at[pl.ds(0, n)])
            _gather_ring(tab_hbm, idx_v, rows_v, sems, consume, n)
            pltpu.sync_copy(out_v.at[pl.ds(0, n)], out_hbm.at[pl.ds(base, n)])

        _for_token_blocks(wid * per_worker, per_worker, block)

    return pl.kernel(
        body,
        out_type=jax.ShapeDtypeStruct((N, N_SEL), jnp.float32),
        mesh=mesh,
        scratch_types=[
            pltpu.VMEM((SC_TOKENS, N_SEL), jnp.int32),
            pltpu.VMEM((SC_TOKENS, SC_PAIR_WORDS), jnp.int32),
            pltpu.VMEM((SC_RING, TOPK) + SC_ROW_TILE, jnp.int32),
            pltpu.VMEM((SC_TOKENS, N_SEL), jnp.float32),
            pltpu.VMEM((TOPK, SC_LANES), jnp.float32),
            pltpu.SemaphoreType.DMA((SC_RING,)),
        ],
        compiler_params=pltpu.CompilerParams(needs_layout_passes=False),
        name="peer_down",
    )(h_packed, idx, packed)


def _peer_up(w_dup, idx, packed, N):
    D = D_MODEL
    mesh, n_cores, n_workers = _sc_mesh_and_workers()
    per_worker = N // n_workers

    def body(w_hbm, idx_hbm, tab_hbm, out_hbm, idx_v, w_v, rows_v, out_v, sems):
        wid = lax.axis_index("subcore") * n_cores + lax.axis_index("core")

        def consume(t, hd, rows):
            t_vec = jnp.full((SC_LANES,), t, jnp.int32)
            ws = [_as_bf16(plsc.load_gather(w_v, [t_vec, jnp.full((SC_LANES,), hd * TOPK + k, jnp.int32)]))
                  for k in range(TOPK)]

            @plsc.parallel_loop(0, SC_STEPS)
            def _(s):
                parts = [_widen_sum([_as_bf16(_row_run(rows, k, s)) * ws[k] for k in range(g, g + SC_BF16_SUM)])
                         for g in range(0, TOPK, SC_BF16_SUM)]
                for half, c in enumerate((s * SC_LANES, SC_PAIR_WORDS + s * SC_LANES)):
                    sl = pl.ds(c, SC_LANES)
                    terms = [p[half] for p in parts]
                    if hd > 0:
                        terms.append(out_v[t, sl])
                    out_v[t, sl] = _tree_sum(terms)

        def block(base, n):
            pltpu.sync_copy(idx_hbm.at[pl.ds(base, n)], idx_v.at[pl.ds(0, n)])
            pltpu.sync_copy(w_hbm.at[pl.ds(base, n)], w_v.at[pl.ds(0, n)])
            _gather_ring(tab_hbm, idx_v, rows_v, sems, consume, n)
            pltpu.sync_copy(out_v.at[pl.ds(0, n)], out_hbm.at[pl.ds(base, n)])

        _for_token_blocks(wid * per_worker, per_worker, block)

    return pl.kernel(
        body,
        out_type=jax.ShapeDtypeStruct((N, D), jnp.float32),
        mesh=mesh,
        scratch_types=[
            pltpu.VMEM((SC_TOKENS, N_SEL), jnp.int32),
            pltpu.VMEM((SC_TOKENS, N_SEL), jnp.int32),
            pltpu.VMEM((SC_RING, TOPK) + SC_ROW_TILE, jnp.int32),
            pltpu.VMEM((SC_TOKENS, D), jnp.float32),
            pltpu.SemaphoreType.DMA((SC_RING,)),
        ],
        compiler_params=pltpu.CompilerParams(needs_layout_passes=False),
        name="peer_up",
    )(w_dup, idx, packed)


def _peer_tc_kernel(idx_cur, idx_nxt, h_ref, gate_ref, tab_hbm, o_ref, gbuf, sems):
    i = pl.program_id(0)
    n = pl.num_programs(0)
    groups = N_SEL // 8

    def issue(idx_ref, slot):
        def body(g, carry):
            for j in range(8):
                e = idx_ref[g * 8 + j]
                pltpu.make_async_copy(
                    tab_hbm.at[pl.ds(e, 1), :], gbuf.at[slot, g, pl.ds(j, 1), :], sems.at[slot]).start()
            return carry
        lax.fori_loop(0, TC_TOKENS * groups, body, 0)

    @pl.when(i == 0)
    def _():
        issue(idx_cur, 0)

    @pl.when(i + 1 < n)
    def _():
        issue(idx_nxt, (i + 1) % 2)

    slot = i % 2
    pltpu.make_async_copy(gbuf.at[slot], gbuf.at[slot], sems.at[slot]).wait()

    half = SC_PAIR_WORDS

    def rows(t, c0):
        w = gbuf[slot, t * groups:(t + 1) * groups, :, c0:c0 + half].reshape(N_SEL, half)
        lo = lax.bitcast_convert_type(lax.shift_left(w, jnp.int32(16)), jnp.float32)
        hi = lax.bitcast_convert_type(lax.bitwise_and(w, jnp.int32(-65536)), jnp.float32)
        return lo, hi

    lane = lax.broadcasted_iota(jnp.int32, (N_SEL, 128), 1)
    pre_all = jnp.zeros((N_SEL, 128), jnp.float32)
    for t in range(TC_TOKENS):
        lo, hi = rows(t, 0)
        prod = lo * h_ref[t:t + 1, 0:half] + hi * h_ref[t:t + 1, half:D_MODEL]
        part = prod[:, 0:128]
        for c in range(1, half // 128):
            part = part + prod[:, c * 128:(c + 1) * 128]
        pre = jnp.sum(part, axis=1, keepdims=True)
        pre_all = jnp.where(lane == t, pre, pre_all)
    act = 0.5 * pre_all * (1.0 + lax.erf(pre_all * (1.0 / math.sqrt(2.0))))
    gate_t = jnp.concatenate([gate_ref[...]] * (128 // TC_TOKENS), axis=0).T
    wgt = gate_t * act
    for t in range(TC_TOKENS):
        w_col = jnp.sum(jnp.where(lane == t, wgt, 0.0), axis=1, keepdims=True)
        lo, hi = rows(t, half)
        o_ref[t:t + 1, 0:half] = jnp.sum(lo * w_col, axis=0, keepdims=True)
        o_ref[t:t + 1, half:D_MODEL] = jnp.sum(hi * w_col, axis=0, keepdims=True)


def _peer_tc(h2, h_row0, idx, gate, table2, row0, n_tokens):
    D = h2.shape[1]
    steps = n_tokens // TC_TOKENS
    blk0 = row0 // TC_TOKENS
    h_blk0 = (h_row0 + row0) // TC_TOKENS
    idx_flat = idx.reshape(-1)
    return pl.pallas_call(
        _peer_tc_kernel,
        grid=(steps,),
        in_specs=[
            pl.BlockSpec((TC_TOKENS * N_SEL,), lambda i: (blk0 + i,), memory_space=pltpu.SMEM),
            pl.BlockSpec((TC_TOKENS * N_SEL,), lambda i: (blk0 + jnp.minimum(i + 1, steps - 1),),
                         memory_space=pltpu.SMEM),
            pl.BlockSpec((TC_TOKENS, D), lambda i: (h_blk0 + i, 0)),
            pl.BlockSpec((TC_TOKENS, N_SEL), lambda i: (blk0 + i, 0)),
            pl.BlockSpec(memory_space=pl.ANY),
        ],
        out_specs=pl.BlockSpec((TC_TOKENS, D), lambda i: (i, 0)),
        out_shape=jax.ShapeDtypeStruct((n_tokens, D), jnp.float32),
        scratch_shapes=[
            pltpu.VMEM((2, TC_TOKENS * N_SEL // 8, 8, 2 * SC_PAIR_WORDS), jnp.int32),
            pltpu.SemaphoreType.DMA((2,)),
        ],
        compiler_params=pltpu.CompilerParams(
            dimension_semantics=("arbitrary",), vmem_limit_bytes=VMEM_LIMIT_BYTES),
        name="peer_tc",
    )(idx_flat, idx_flat, h2, gate, table2)


def _gate_act_kernel(pre_ref, gate_ref, o_ref):
    p = pre_ref[...]
    act = 0.5 * p * (1.0 + lax.erf(p * (1.0 / math.sqrt(2.0))))
    o_ref[...] = gate_ref[...] * act


def _gate_act(pre, gate):
    N = pre.shape[0]
    spec = pl.BlockSpec((EW_BLOCK, N_SEL), lambda i: (i, 0))
    return pl.pallas_call(
        _gate_act_kernel, grid=(N // EW_BLOCK,), in_specs=[spec, spec], out_specs=spec,
        out_shape=jax.ShapeDtypeStruct((N, N_SEL), jnp.float32), name="gate_act",
    )(pre, gate)


def _final_kernel(x1_ref, mod_ref, g_ref, *refs, normalize, starts):
    peer_refs, o_ref = refs[:-1], refs[-1]
    gate2 = mod_ref[0, 5:6, :]
    peer = peer_refs[0][...]
    for p in range(1, len(peer_refs)):
        peer = jnp.where(pl.program_id(0) >= starts[p], peer_refs[p][...], peer)
    y = x1_ref[0] + gate2 * peer
    o_ref[...] = _rms(y, g_ref[...]) if normalize else y


def _residual(x1, row0, peers, b, mod3, g_final, normalize):
    D = x1.shape[2]
    blocks = [p.shape[0] // EW_BLOCK for p in peers]
    starts = [sum(blocks[:p]) for p in range(len(peers))]
    blk0 = row0 // EW_BLOCK

    def piece_spec(start, n):
        return pl.BlockSpec((EW_BLOCK, D), lambda j: (jnp.clip(j - start, 0, n - 1), 0))

    return pl.pallas_call(
        functools.partial(_final_kernel, normalize=normalize, starts=tuple(starts)),
        grid=(sum(blocks),),
        in_specs=[pl.BlockSpec((1, EW_BLOCK, D), lambda j: (0, blk0 + j, 0)),
                  pl.BlockSpec((1, N_MOD, D), lambda j: (b, 0, 0)),
                  pl.BlockSpec((1, D), lambda j: (0, 0))]
                 + [piece_spec(s, n) for s, n in zip(starts, blocks)],
        out_specs=pl.BlockSpec((EW_BLOCK, D), lambda j: (j, 0)),
        out_shape=jax.ShapeDtypeStruct((sum(blocks) * EW_BLOCK, D), jnp.float32),
        compiler_params=pltpu.CompilerParams(vmem_limit_bytes=VMEM_LIMIT_BYTES),
        name="final_norm",
    )(x1, mod3, g_final, *peers)


def _token_split(u, n_units, n):
    unit = n // 16
    if u == n_units - 1 and n_units > 1:
        sc, c = 11, 2
    elif u == 0:
        sc, c = 14, 0
    else:
        sc, c = 13, 0
    n_sc, n_c = sc * unit, c * unit
    n_a = ((n - n_sc - n_c) // 2) // EW_BLOCK * EW_BLOCK
    return n_sc, n_a, n - n_sc - n_c - n_a, n_c


def _token_ranges(n_seq, S):
    return [(b, 0, S) for b in range(n_seq)]


def kernel(x, c, w_ada, b_ada, g_norm1, w_in, w_pool, pool_scale, sgu_ln_g, sgu_ln_b, w_spatial, b_spatial,
           w_out, g_norm2, w_query, sub_keys, expert_down, expert_up, g_final):
    B, S, D = x.shape
    depth = w_ada.shape[0]
    bf16 = jnp.bfloat16
    band = jnp.asarray(_pool_band(), bf16)
    c_pad = jnp.pad(c, ((0, 8 - B), (0, 0)))
    for l in range(depth):
        mod = _ada_mod(c_pad, w_ada[l].astype(bf16), b_ada[l][None, :])[:B]
        mod3 = mod.reshape(B, N_MOD, D)
        mix_w = (g_norm1[l][None, :], w_in[l].astype(bf16), band, w_pool[l].astype(bf16),
                 pool_scale[l][None, :], sgu_ln_g[l][None, :], sgu_ln_b[l][None, :], w_spatial[l],
                 b_spatial[l].T, w_out[l].astype(bf16), g_norm2[l][None, :])
        keys = sub_keys[l].reshape(2 * PEER_HEADS, N_KEYS, HALF_KEY).astype(bf16)
        wq = w_query[l].astype(bf16)
        down_packed = _pack_rows(expert_down[l])
        up_packed = _pack_rows(expert_up[l])
        table2 = jnp.concatenate([down_packed, up_packed], axis=1)
        down_sc, up_sc = _sc_table(down_packed), _sc_table(up_packed)
        last = l + 1 == depth
        outs = []
        pending = None
        mod3_b = mod3
        seq_done = -1
        ranges = _token_ranges(B, S)
        for u, (b, r0, n) in enumerate(ranges):
            n_sc, n_a, n_b, n_c = _token_split(u, len(ranges), n)
            if b != seq_done:
                x1, h2 = _mixer(x, b, 1, mod3_b, *mix_w)
                h2f = h2.reshape(S, D)
                seq_done = b
            idx, gate = _query_topk(h2f, wq, keys, r0, n)
            tc_a = _peer_tc(h2f, r0, idx, gate, table2, n_sc, n_a)
            pre = _peer_down(_pack_rows(h2f[r0:r0 + n_sc]), idx, down_sc, n_sc)
            gate_b = gate
            if pending is not None:
                p_x1, p_r0, p_peers, p_seq = pending
                p_peers[0], tc_a = lax.optimization_barrier((p_peers[0], tc_a))
                out_prev = _residual(p_x1, p_r0, p_peers, p_seq, mod3, g_final[None, :], last)
                outs.append(out_prev)
                gate_b, _ = lax.optimization_barrier((gate, out_prev))
            tc_b = _peer_tc(h2f, r0, idx, gate_b, table2, n_sc + n_a, n_b)
            pre, tc_a, tc_b = lax.optimization_barrier((pre, tc_a, tc_b))
            wgt = _gate_act(pre, gate)
            peers = [_peer_up(_pack_dup(wgt), idx, up_sc, n_sc), tc_a, tc_b]
            if n_c:
                gate_c, _ = lax.optimization_barrier((gate, wgt))
                peers.append(_peer_tc(h2f, r0, idx, gate_c, table2, n_sc + n_a + n_b, n_c))
            mod3_b, _ = lax.optimization_barrier((mod3, wgt))
            pending = (x1, r0, peers, b)
        p_x1, p_r0, p_peers, p_seq = pending
        outs.append(_residual(p_x1, p_r0, p_peers, p_seq, mod3, g_final[None, :], last))
        x = jnp.concatenate(outs, axis=0).reshape(B, S, D)
    return x
```

```python
import functools
import math

import jax
import jax.numpy as jnp
import numpy as np
from jax import lax
from jax.experimental import pallas as pl
from jax.experimental.pallas import tpu as pltpu
from jax.experimental.pallas import tpu_sc as plsc

D_MODEL = 1024
POOL_WIDTH = 512
SGU_WIDTH = 512
POOL_WINDOWS = (2, 4, 8, 16)
GROUP_DIM = 128
CHUNK = 128
SGU_HEADS = 4
IN_PROJ_WIDTH = POOL_WIDTH + 2 * SGU_WIDTH
PEER_HEADS = 8
N_KEYS = 128
HALF_KEY = 128
TOPK = 16
N_MOD = 6
EPS = 1e-6

MIX_BLOCK = 512
TOPK_BLOCK = 256
EW_BLOCK = 256
VMEM_LIMIT_BYTES = 48 * 1024 * 1024

SC_LANES = 16
SC_TOKENS = 32
SC_RING = 8
N_SEL = PEER_HEADS * TOPK
TC_TOKENS = 8

N_CAND = 16 + 7 * 8 + 8


def _pool_band():
    t = np.arange(CHUNK)[:, None]
    j = np.arange(2 * CHUNK)[None, :]
    bands = []
    for w in POOL_WINDOWS:
        m = (j > CHUNK + t - w) & (j <= CHUNK + t)
        bands.append(np.where(m, 1.0 / w, 0.0))
    return np.stack(bands).astype(np.float32)


def _bf16_dot(a, b):
    return jnp.dot(a.astype(jnp.bfloat16), b.astype(jnp.bfloat16), preferred_element_type=jnp.float32)


def _ada_kernel(c_ref, w_ref, b_ref, o_ref):
    c = c_ref[...]
    c_act = c * jax.nn.sigmoid(c)
    o_ref[...] = _bf16_dot(c_act, w_ref[...]) + b_ref[...]


def _ada_mod(c_pad, w_ada, b_ada):
    rows = c_pad.shape[0]
    return pl.pallas_call(
        _ada_kernel,
        grid=(N_MOD,),
        in_specs=[
            pl.BlockSpec((rows, D_MODEL), lambda i: (0, 0)),
            pl.BlockSpec((D_MODEL, D_MODEL), lambda i: (0, i)),
            pl.BlockSpec((1, D_MODEL), lambda i: (0, i)),
        ],
        out_specs=pl.BlockSpec((rows, D_MODEL), lambda i: (0, i)),
        out_shape=jax.ShapeDtypeStruct((rows, N_MOD * D_MODEL), jnp.float32),
        name="ada_mod",
    )(c_pad, w_ada, b_ada)


def _rms(x, g):
    return x * lax.rsqrt(jnp.mean(x * x, axis=-1, keepdims=True) + EPS) * g


def _mix_kernel(x_ref, mod_ref, g1_ref, win_ref, band_ref, wpool_ref, pscale_ref, lng_ref, lnb_ref,
                wsp_ref, bsp_ref, wout_ref, g2_ref, x1_ref, h2_ref, aext_ref, mixed_ref):
    j = pl.program_id(1)
    x = x_ref[0]
    shift1, scale1, gate1 = mod_ref[0, 0:1, :], mod_ref[0, 1:2, :], mod_ref[0, 2:3, :]
    shift2, scale2 = mod_ref[0, 3:4, :], mod_ref[0, 4:5, :]

    h = _rms(x, g1_ref[...]) * (1.0 + scale1) + shift1
    proj = _bf16_dot(h, win_ref[...])
    a = proj[:, :POOL_WIDTH]
    u = proj[:, POOL_WIDTH:POOL_WIDTH + SGU_WIDTH]
    v = proj[:, POOL_WIDTH + SGU_WIDTH:]

    @pl.when(j == 0)
    def _():
        aext_ref[0:CHUNK, :] = jnp.zeros((CHUNK, POOL_WIDTH), jnp.float32)

    aext_ref[CHUNK:, :] = a

    mu = jnp.mean(v, axis=-1, keepdims=True)
    vc = v - mu
    vn = vc * lax.rsqrt(jnp.mean(vc * vc, axis=-1, keepdims=True) + EPS) * lng_ref[...] + lnb_ref[...]

    row = lax.broadcasted_iota(jnp.int32, (CHUNK, CHUNK), 0)
    col = lax.broadcasted_iota(jnp.int32, (CHUNK, CHUNK), 1)
    tril = col <= row
    t_col = lax.broadcasted_iota(jnp.int32, (CHUNK, 1), 0)

    for c in range(MIX_BLOCK // CHUNK):
        r0 = c * CHUNK
        pos = (j * MIX_BLOCK + r0 + 1 + t_col).astype(jnp.float32)
        for g, w in enumerate(POOL_WINDOWS):
            c0 = g * GROUP_DIM
            seg = aext_ref[r0:r0 + 2 * CHUNK, c0:c0 + GROUP_DIM]
            hi = seg.astype(jnp.bfloat16)
            lo = (seg - hi.astype(jnp.float32)).astype(jnp.bfloat16)
            band = band_ref[g]
            win = (jnp.dot(band, hi, preferred_element_type=jnp.float32)
                   + jnp.dot(band, lo, preferred_element_type=jnp.float32))
            mean = win * (float(w) / jnp.minimum(pos, float(w)))
            pooled = mean - seg[CHUNK:, :]
            y = _bf16_dot(pooled, wpool_ref[g]) * pscale_ref[:, c0:c0 + GROUP_DIM]
            mixed_ref[r0:r0 + CHUNK, c0:c0 + GROUP_DIM] = y.astype(jnp.bfloat16)
        for hh in range(SGU_HEADS):
            c0 = hh * GROUP_DIM
            wm = jnp.where(tril, wsp_ref[hh], 0.0)
            m = _bf16_dot(wm, vn[r0:r0 + CHUNK, c0:c0 + GROUP_DIM]) + bsp_ref[:, hh:hh + 1]
            s = u[r0:r0 + CHUNK, c0:c0 + GROUP_DIM] * m
            mixed_ref[r0:r0 + CHUNK, POOL_WIDTH + c0:POOL_WIDTH + c0 + GROUP_DIM] = s.astype(jnp.bfloat16)

    aext_ref[0:CHUNK, :] = aext_ref[MIX_BLOCK:MIX_BLOCK + CHUNK, :]

    x1 = x + gate1 * jnp.dot(mixed_ref[...], wout_ref[...], preferred_element_type=jnp.float32)
    x1_ref[0] = x1
    h2_ref[0] = _rms(x1, g2_ref[...]) * (1.0 + scale2) + shift2


def _mixer(x, b0, n_seq, mod3, g1, w_in, band, w_pool, pool_scale, ln_g, ln_b, w_sp, b_sp_t, w_out, g2):
    _, S, D = x.shape
    const2 = lambda b, j: (0, 0)
    const3 = lambda b, j: (0, 0, 0)
    tok = pl.BlockSpec((1, MIX_BLOCK, D), lambda b, j: (b, j, 0))
    return pl.pallas_call(
        _mix_kernel,
        grid=(n_seq, S // MIX_BLOCK),
        in_specs=[
            pl.BlockSpec((1, MIX_BLOCK, D), lambda b, j: (b0 + b, j, 0)),
            pl.BlockSpec((1, N_MOD, D), lambda b, j: (b0 + b, 0, 0)),
            pl.BlockSpec((1, D), const2),
            pl.BlockSpec((D, IN_PROJ_WIDTH), const2),
            pl.BlockSpec((len(POOL_WINDOWS), CHUNK, 2 * CHUNK), const3),
            pl.BlockSpec((len(POOL_WINDOWS), GROUP_DIM, GROUP_DIM), const3),
            pl.BlockSpec((1, POOL_WIDTH), const2),
            pl.BlockSpec((1, SGU_WIDTH), const2),
            pl.BlockSpec((1, SGU_WIDTH), const2),
            pl.BlockSpec((SGU_HEADS, CHUNK, CHUNK), const3),
            pl.BlockSpec((CHUNK, SGU_HEADS), const2),
            pl.BlockSpec((D, D), const2),
            pl.BlockSpec((1, D), const2),
        ],
        out_specs=[tok, tok],
        out_shape=[jax.ShapeDtypeStruct((n_seq, S, D), jnp.float32)] * 2,
        scratch_shapes=[
            pltpu.VMEM((MIX_BLOCK + CHUNK, POOL_WIDTH), jnp.float32),
            pltpu.VMEM((MIX_BLOCK, D), jnp.bfloat16),
        ],
        compiler_params=pltpu.CompilerParams(
            dimension_semantics=("arbitrary", "arbitrary"), vmem_limit_bytes=VMEM_LIMIT_BYTES),
        name="mixer",
    )(x, mod3, g1, w_in, band, w_pool, pool_scale, ln_g, ln_b, w_sp, b_sp_t, w_out, g2)


def _extract_top(s, ids, payload, n_out):
    vals, pays = [], []
    for _ in range(n_out):
        m = jnp.max(s, axis=0, keepdims=True)
        pick = jnp.min(jnp.where(s == m, ids, jnp.float32(1e9)), axis=0, keepdims=True)
        sel = ids == pick
        vals.append(m)
        pays.append(pick if payload is None
                    else jnp.max(jnp.where(sel, payload, -1.0), axis=0, keepdims=True))
        s = jnp.where(sel, -jnp.inf, s)
    return jnp.concatenate(vals, axis=0), jnp.concatenate(pays, axis=0)


def _topk_kernel(h2_ref, wq_ref, keys_ref, idx_ref, gate_ref):
    T = TOPK_BLOCK
    q = _bf16_dot(h2_ref[...], wq_ref[...])
    key_ids = lax.broadcasted_iota(jnp.int32, (N_KEYS, T), 0).astype(jnp.float32)
    row = lax.broadcasted_iota(jnp.int32, (N_CAND, T), 0)
    flat = jnp.where(row < 16, row,
                     jnp.where(row < 72, jnp.right_shift(row - 8, 3) * 16 + jnp.bitwise_and(row, 7),
                               (row - 64) * 16)).astype(jnp.float32)
    idx_rows, gate_rows = [], []
    for hd in range(PEER_HEADS):
        tops = []
        for p in range(2):
            hp = hd * 2 + p
            qs = q[:, hp * HALF_KEY:(hp + 1) * HALF_KEY].astype(jnp.bfloat16)
            st = lax.dot_general(keys_ref[hp], qs, (((1,), (1,)), ((), ())),
                                 preferred_element_type=jnp.float32)
            tops.append(_extract_top(st, key_ids, None, TOPK))
        (s0, i0), (s1, i1) = tops
        i0 = i0 * float(N_KEYS)
        cand = jnp.concatenate(
            [s0[0:1] + s1] + [s0[a:a + 1] + s1[0:8] for a in range(1, 8)] + [s0[8:16] + s1[0:1]], axis=0)
        cidx = jnp.concatenate(
            [i0[0:1] + i1] + [i0[a:a + 1] + i1[0:8] for a in range(1, 8)] + [i0[8:16] + i1[0:1]],
            axis=0)
        best, sel_idx = _extract_top(cand, flat, cidx, TOPK)
        e = jnp.exp(best - jnp.max(best, axis=0, keepdims=True))
        gate_rows.append(e / jnp.sum(e, axis=0, keepdims=True))
        idx_rows.append(sel_idx)
    idx_ref[...] = jnp.concatenate(idx_rows, axis=0).T.astype(jnp.int32)
    gate_ref[...] = jnp.concatenate(gate_rows, axis=0).T


def _query_topk(h2, w_query, keys, row0, N):
    D = h2.shape[1]
    blk0 = row0 // TOPK_BLOCK
    tok = pl.BlockSpec((TOPK_BLOCK, N_SEL), lambda i: (i, 0))
    return pl.pallas_call(
        _topk_kernel,
        grid=(N // TOPK_BLOCK,),
        in_specs=[
            pl.BlockSpec((TOPK_BLOCK, D), lambda i: (blk0 + i, 0)),
            pl.BlockSpec((D, 2 * PEER_HEADS * HALF_KEY), lambda i: (0, 0)),
            pl.BlockSpec((2 * PEER_HEADS, N_KEYS, HALF_KEY), lambda i: (0, 0, 0)),
        ],
        out_specs=[tok, tok],
        out_shape=[jax.ShapeDtypeStruct((N, N_SEL), jnp.int32), jax.ShapeDtypeStruct((N, N_SEL), jnp.float32)],
        compiler_params=pltpu.CompilerParams(
            dimension_semantics=("arbitrary",), vmem_limit_bytes=VMEM_LIMIT_BYTES),
        name="query_topk",
    )(h2, w_query, keys)


def _sc_mesh_and_workers():
    info = plsc.get_sparse_core_info()
    assert info.num_lanes == SC_LANES
    mesh = plsc.VectorSubcoreMesh(core_axis_name="core", subcore_axis_name="subcore")
    return mesh, info.num_cores, info.num_cores * info.num_subcores


def _gather_ring(tab_hbm, idx_v, rows_v, sems, consume, n_tokens):
    look = SC_RING - 1

    def copy(t, hd):
        slot = hd % SC_RING
        return pltpu.make_async_copy(
            tab_hbm.at[idx_v.at[t, pl.ds(hd * TOPK, TOPK)]], rows_v.at[slot], sems.at[slot])

    for hd in range(look):
        copy(0, hd).start()

    @pl.loop(0, n_tokens)
    def _(t):
        for hd in range(PEER_HEADS):
            nxt = hd + look
            if nxt < PEER_HEADS:
                copy(t, nxt).start()
            else:
                @pl.when(t + 1 < n_tokens)
                def _():
                    copy(t + 1, nxt - PEER_HEADS).start()
            copy(t, hd).wait()
            consume(t, hd, rows_v.at[hd % SC_RING])


def _for_token_blocks(first, per_worker, block):
    n_full, tail = divmod(per_worker, SC_TOKENS)
    assert tail % 8 == 0, "HBM row offsets of the staged slices must stay 8-aligned"

    @pl.loop(0, n_full)
    def _(blk):
        block(first + blk * SC_TOKENS, SC_TOKENS)

    if tail:
        block(first + n_full * SC_TOKENS, tail)


SC_PAIR_WORDS = D_MODEL // 2
SC_STEPS = SC_PAIR_WORDS // SC_LANES
SC_BF16_SUM = 4


def _bf16_bits(a):
    return lax.bitcast_convert_type(a.astype(jnp.bfloat16), jnp.uint16).astype(jnp.uint32)


def _pack_rows(a):
    bits = _bf16_bits(a)
    return lax.bitcast_convert_type(bits[:, :SC_PAIR_WORDS] | (bits[:, SC_PAIR_WORDS:] << 16), jnp.int32)


def _pack_dup(a):
    bits = _bf16_bits(a)
    return lax.bitcast_convert_type(bits | (bits << 16), jnp.int32)


def _unpack_pair(words):
    lo = lax.bitcast_convert_type(lax.shift_left(words, jnp.int32(16)), jnp.float32)
    hi = lax.bitcast_convert_type(lax.bitwise_and(words, jnp.int32(-65536)), jnp.float32)
    return lo, hi


SC_ROW_TILE = (SC_PAIR_WORDS // 128, 128)


def _sc_table(packed):
    return packed.reshape(packed.shape[0], *SC_ROW_TILE)


def _row_run(rows, k, s):
    shift = (128 // SC_LANES).bit_length() - 1
    return rows[k, jnp.right_shift(s, shift), pl.ds(jnp.bitwise_and(s, 128 // SC_LANES - 1) * SC_LANES, SC_LANES)]


def _as_bf16(words):
    return plsc.bitcast(words, jnp.bfloat16)


def _widen_sum(products):
    return _unpack_pair(plsc.bitcast(_tree_sum(products), jnp.int32))


def _tree_sum(terms):
    terms = list(terms)
    while len(terms) > 1:
        terms = [terms[i] + terms[i + 1] for i in range(0, len(terms) - 1, 2)] + (
            [terms[-1]] if len(terms) % 2 else [])
    return terms[0]


def _peer_down(h_packed, idx, packed, N):
    mesh, n_cores, n_workers = _sc_mesh_and_workers()
    per_worker = N // n_workers

    def body(h_hbm, idx_hbm, tab_hbm, out_hbm, idx_v, h_v, rows_v, out_v, tr_v, sems):
        wid = lax.axis_index("subcore") * n_cores + lax.axis_index("core")
        lane = lax.iota(jnp.int32, SC_LANES)

        def consume(t, hd, rows):
            def dot_step(g, accs):
                steps = [g * SC_BF16_SUM + i for i in range(SC_BF16_SUM)]
                hb = [_as_bf16(h_v[t, pl.ds(s * SC_LANES, SC_LANES)]) for s in steps]
                out = []
                for k, acc in enumerate(accs):
                    lo, hi = _widen_sum([_as_bf16(_row_run(rows, k, s)) * hbi for s, hbi in zip(steps, hb)])
                    out.append(acc + (lo + hi))
                return tuple(out)

            accs = lax.fori_loop(0, SC_STEPS // SC_BF16_SUM, dot_step,
                                 tuple(jnp.zeros((SC_LANES,), jnp.float32) for _ in range(TOPK)))
            tot = jnp.zeros((SC_LANES,), jnp.float32)
            for k in range(TOPK):
                tot = jnp.where(lane == k, jnp.sum(accs[k]), tot)
            out_v[t, pl.ds(hd * TOPK, TOPK)] = tot

        def block(base, n):
            pltpu.sync_copy(idx_hbm.at[pl.ds(base, n)], idx_v.at[pl.ds(0, n)])
            pltpu.sync_copy(h_hbm.at[pl.ds(base, n)], h_v.at[pl.ds(0, n)])
            _gather_ring(tab_hbm, idx_v, rows_v, sems, consume, n)
            pltpu.sync_copy(out_v.at[pl.ds(0, n)], out_hbm.at[pl.ds(base, n)])

        _for_token_blocks(wid * per_worker, per_worker, block)

    return pl.kernel(
        body,
        out_type=jax.ShapeDtypeStruct((N, N_SEL), jnp.float32),
        mesh=mesh,
        scratch_types=[
            pltpu.VMEM((SC_TOKENS, N_SEL), jnp.int32),
            pltpu.VMEM((SC_TOKENS, SC_PAIR_WORDS), jnp.int32),
            pltpu.VMEM((SC_RING, TOPK) + SC_ROW_TILE, jnp.int32),
            pltpu.VMEM((SC_TOKENS, N_SEL), jnp.float32),
            pltpu.VMEM((TOPK, SC_LANES), jnp.float32),
            pltpu.SemaphoreType.DMA((SC_RING,)),
        ],
        compiler_params=pltpu.CompilerParams(needs_layout_passes=False),
        name="peer_down",
    )(h_packed, idx, packed)


def _peer_up(w_dup, idx, packed, N):
    D = D_MODEL
    mesh, n_cores, n_workers = _sc_mesh_and_workers()
    per_worker = N // n_workers

    def body(w_hbm, idx_hbm, tab_hbm, out_hbm, idx_v, w_v, rows_v, out_v, sems):
        wid = lax.axis_index("subcore") * n_cores + lax.axis_index("core")

        def consume(t, hd, rows):
            t_vec = jnp.full((SC_LANES,), t, jnp.int32)
            ws = [_as_bf16(plsc.load_gather(w_v, [t_vec, jnp.full((SC_LANES,), hd * TOPK + k, jnp.int32)]))
                  for k in range(TOPK)]

            @plsc.parallel_loop(0, SC_STEPS)
            def _(s):
                parts = [_widen_sum([_as_bf16(_row_run(rows, k, s)) * ws[k] for k in range(g, g + SC_BF16_SUM)])
                         for g in range(0, TOPK, SC_BF16_SUM)]
                for half, c in enumerate((s * SC_LANES, SC_PAIR_WORDS + s * SC_LANES)):
                    sl = pl.ds(c, SC_LANES)
                    terms = [p[half] for p in parts]
                    if hd > 0:
                        terms.append(out_v[t, sl])
                    out_v[t, sl] = _tree_sum(terms)

        def block(base, n):
            pltpu.sync_copy(idx_hbm.at[pl.ds(base, n)], idx_v.at[pl.ds(0, n)])
            pltpu.sync_copy(w_hbm.at[pl.ds(base, n)], w_v.at[pl.ds(0, n)])
            _gather_ring(tab_hbm, idx_v, rows_v, sems, consume, n)
            pltpu.sync_copy(out_v.at[pl.ds(0, n)], out_hbm.at[pl.ds(base, n)])

        _for_token_blocks(wid * per_worker, per_worker, block)

    return pl.kernel(
        body,
        out_type=jax.ShapeDtypeStruct((N, D), jnp.float32),
        mesh=mesh,
        scratch_types=[
            pltpu.VMEM((SC_TOKENS, N_SEL), jnp.int32),
            pltpu.VMEM((SC_TOKENS, N_SEL), jnp.int32),
            pltpu.VMEM((SC_RING, TOPK) + SC_ROW_TILE, jnp.int32),
            pltpu.VMEM((SC_TOKENS, D), jnp.float32),
            pltpu.SemaphoreType.DMA((SC_RING,)),
        ],
        compiler_params=pltpu.CompilerParams(needs_layout_passes=False),
        name="peer_up",
    )(w_dup, idx, packed)


def _peer_tc_kernel(idx_cur, idx_nxt, h_ref, gate_ref, tab_hbm, o_ref, gbuf, sems):
    i = pl.program_id(0)
    n = pl.num_programs(0)
    groups = N_SEL // 8

    def issue(idx_ref, slot):
        def body(g, carry):
            for j in range(8):
                e = idx_ref[g * 8 + j]
                pltpu.make_async_copy(
                    tab_hbm.at[pl.ds(e, 1), :], gbuf.at[slot, g, pl.ds(j, 1), :], sems.at[slot]).start()
            return carry
        lax.fori_loop(0, TC_TOKENS * groups, body, 0)

    @pl.when(i == 0)
    def _():
        issue(idx_cur, 0)

    @pl.when(i + 1 < n)
    def _():
        issue(idx_nxt, (i + 1) % 2)

    slot = i % 2
    pltpu.make_async_copy(gbuf.at[slot], gbuf.at[slot], sems.at[slot]).wait()

    half = SC_PAIR_WORDS

    def rows(t, c0):
        w = gbuf[slot, t * groups:(t + 1) * groups, :, c0:c0 + half].reshape(N_SEL, half)
        lo = lax.bitcast_convert_type(lax.shift_left(w, jnp.int32(16)), jnp.float32)
        hi = lax.bitcast_convert_type(lax.bitwise_and(w, jnp.int32(-65536)), jnp.float32)
        return lo, hi

    lane = lax.broadcasted_iota(jnp.int32, (N_SEL, 128), 1)
    pre_all = jnp.zeros((N_SEL, 128), jnp.float32)
    for t in range(TC_TOKENS):
        lo, hi = rows(t, 0)
        prod = lo * h_ref[t:t + 1, 0:half] + hi * h_ref[t:t + 1, half:D_MODEL]
        part = prod[:, 0:128]
        for c in range(1, half // 128):
            part = part + prod[:, c * 128:(c + 1) * 128]
        pre = jnp.sum(part, axis=1, keepdims=True)
        pre_all = jnp.where(lane == t, pre, pre_all)
    act = 0.5 * pre_all * (1.0 + lax.erf(pre_all * (1.0 / math.sqrt(2.0))))
    gate_t = jnp.concatenate([gate_ref[...]] * (128 // TC_TOKENS), axis=0).T
    wgt = gate_t * act
    for t in range(TC_TOKENS):
        w_col = jnp.sum(jnp.where(lane == t, wgt, 0.0), axis=1, keepdims=True)
        lo, hi = rows(t, half)
        o_ref[t:t + 1, 0:half] = jnp.sum(lo * w_col, axis=0, keepdims=True)
        o_ref[t:t + 1, half:D_MODEL] = jnp.sum(hi * w_col, axis=0, keepdims=True)


def _peer_tc(h2, h_row0, idx, gate, table2, row0, n_tokens):
    D = h2.shape[1]
    steps = n_tokens // TC_TOKENS
    blk0 = row0 // TC_TOKENS
    h_blk0 = (h_row0 + row0) // TC_TOKENS
    idx_flat = idx.reshape(-1)
    return pl.pallas_call(
        _peer_tc_kernel,
        grid=(steps,),
        in_specs=[
            pl.BlockSpec((TC_TOKENS * N_SEL,), lambda i: (blk0 + i,), memory_space=pltpu.SMEM),
            pl.BlockSpec((TC_TOKENS * N_SEL,), lambda i: (blk0 + jnp.minimum(i + 1, steps - 1),),
                         memory_space=pltpu.SMEM),
            pl.BlockSpec((TC_TOKENS, D), lambda i: (h_blk0 + i, 0)),
            pl.BlockSpec((TC_TOKENS, N_SEL), lambda i: (blk0 + i, 0)),
            pl.BlockSpec(memory_space=pl.ANY),
        ],
        out_specs=pl.BlockSpec((TC_TOKENS, D), lambda i: (i, 0)),
        out_shape=jax.ShapeDtypeStruct((n_tokens, D), jnp.float32),
        scratch_shapes=[
            pltpu.VMEM((2, TC_TOKENS * N_SEL // 8, 8, 2 * SC_PAIR_WORDS), jnp.int32),
            pltpu.SemaphoreType.DMA((2,)),
        ],
        compiler_params=pltpu.CompilerParams(
            dimension_semantics=("arbitrary",), vmem_limit_bytes=VMEM_LIMIT_BYTES),
        name="peer_tc",
    )(idx_flat, idx_flat, h2, gate, table2)


def _gate_act_kernel(pre_ref, gate_ref, o_ref):
    p = pre_ref[...]
    act = 0.5 * p * (1.0 + lax.erf(p * (1.0 / math.sqrt(2.0))))
    o_ref[...] = gate_ref[...] * act


def _gate_act(pre, gate):
    N = pre.shape[0]
    spec = pl.BlockSpec((EW_BLOCK, N_SEL), lambda i: (i, 0))
    return pl.pallas_call(
        _gate_act_kernel, grid=(N // EW_BLOCK,), in_specs=[spec, spec], out_specs=spec,
        out_shape=jax.ShapeDtypeStruct((N, N_SEL), jnp.float32), name="gate_act",
    )(pre, gate)


def _final_kernel(x1_ref, mod_ref, g_ref, *refs, normalize, starts):
    peer_refs, o_ref = refs[:-1], refs[-1]
    gate2 = mod_ref[0, 5:6, :]
    peer = peer_refs[0][...]
    for p in range(1, len(peer_refs)):
        peer = jnp.where(pl.program_id(0) >= starts[p], peer_refs[p][...], peer)
    y = x1_ref[0] + gate2 * peer
    o_ref[...] = _rms(y, g_ref[...]) if normalize else y


def _residual(x1, row0, peers, b, mod3, g_final, normalize):
    D = x1.shape[2]
    blocks = [p.shape[0] // EW_BLOCK for p in peers]
    starts = [sum(blocks[:p]) for p in range(len(peers))]
    blk0 = row0 // EW_BLOCK

    def piece_spec(start, n):
        return pl.BlockSpec((EW_BLOCK, D), lambda j: (jnp.clip(j - start, 0, n - 1), 0))

    return pl.pallas_call(
        functools.partial(_final_kernel, normalize=normalize, starts=tuple(starts)),
        grid=(sum(blocks),),
        in_specs=[pl.BlockSpec((1, EW_BLOCK, D), lambda j: (0, blk0 + j, 0)),
                  pl.BlockSpec((1, N_MOD, D), lambda j: (b, 0, 0)),
                  pl.BlockSpec((1, D), lambda j: (0, 0))]
                 + [piece_spec(s, n) for s, n in zip(starts, blocks)],
        out_specs=pl.BlockSpec((EW_BLOCK, D), lambda j: (j, 0)),
        out_shape=jax.ShapeDtypeStruct((sum(blocks) * EW_BLOCK, D), jnp.float32),
        compiler_params=pltpu.CompilerParams(vmem_limit_bytes=VMEM_LIMIT_BYTES),
        name="final_norm",
    )(x1, mod3, g_final, *peers)


def _token_split(u, n_units, n):
    unit = n // 16
    if u == n_units - 1 and n_units > 1:
        sc, c = 11, 2
    elif u == 0:
        sc, c = 14, 0
    else:
        sc, c = 13, 0
    n_sc, n_c = sc * unit, c * unit
    n_a = ((n - n_sc - n_c) // 2) // EW_BLOCK * EW_BLOCK
    return n_sc, n_a, n - n_sc - n_c - n_a, n_c


def _token_ranges(n_seq, S):
    return [(b, 0, S) for b in range(n_seq)]


def kernel(x, c, w_ada, b_ada, g_norm1, w_in, w_pool, pool_scale, sgu_ln_g, sgu_ln_b, w_spatial, b_spatial,
           w_out, g_norm2, w_query, sub_keys, expert_down, expert_up, g_final):
    B, S, D = x.shape
    depth = w_ada.shape[0]
    bf16 = jnp.bfloat16
    band = jnp.asarray(_pool_band(), bf16)
    c_pad = jnp.pad(c, ((0, 8 - B), (0, 0)))
    for l in range(depth):
        mod = _ada_mod(c_pad, w_ada[l].astype(bf16), b_ada[l][None, :])[:B]
        mod3 = mod.reshape(B, N_MOD, D)
        mix_w = (g_norm1[l][None, :], w_in[l].astype(bf16), band, w_pool[l].astype(bf16),
                 pool_scale[l][None, :], sgu_ln_g[l][None, :], sgu_ln_b[l][None, :], w_spatial[l],
                 b_spatial[l].T, w_out[l].astype(bf16), g_norm2[l][None, :])
        keys = sub_keys[l].reshape(2 * PEER_HEADS, N_KEYS, HALF_KEY).astype(bf16)
        wq = w_query[l].astype(bf16)
        down_packed = _pack_rows(expert_down[l])
        up_packed = _pack_rows(expert_up[l])
        table2 = jnp.concatenate([down_packed, up_packed], axis=1)
        down_sc, up_sc = _sc_table(down_packed), _sc_table(up_packed)
        last = l + 1 == depth
        outs = []
        pending = None
        mod3_b = mod3
        seq_done = -1
        ranges = _token_ranges(B, S)
        for u, (b, r0, n) in enumerate(ranges):
            n_sc, n_a, n_b, n_c = _token_split(u, len(ranges), n)
            if b != seq_done:
                x1, h2 = _mixer(x, b, 1, mod3_b, *mix_w)
                h2f = h2.reshape(S, D)
                seq_done = b
            idx, gate = _query_topk(h2f, wq, keys, r0, n)
            tc_a = _peer_tc(h2f, r0, idx, gate, table2, n_sc, n_a)
            pre = _peer_down(_pack_rows(h2f[r0:r0 + n_sc]), idx, down_sc, n_sc)
            gate_b = gate
            if pending is not None:
                p_x1, p_r0, p_peers, p_seq = pending
                p_peers[0], tc_a = lax.optimization_barrier((p_peers[0], tc_a))
                out_prev = _residual(p_x1, p_r0, p_peers, p_seq, mod3, g_final[None, :], last)
                outs.append(out_prev)
                gate_b, _ = lax.optimization_barrier((gate, out_prev))
            tc_b = _peer_tc(h2f, r0, idx, gate_b, table2, n_sc + n_a, n_b)
            pre, tc_a, tc_b = lax.optimization_barrier((pre, tc_a, tc_b))
            wgt = _gate_act(pre, gate)
            peers = [_peer_up(_pack_dup(wgt), idx, up_sc, n_sc), tc_a, tc_b]
            if n_c:
                gate_c, _ = lax.optimization_barrier((gate, wgt))
                peers.append(_peer_tc(h2f, r0, idx, gate_c, table2, n_sc + n_a + n_b, n_c))
            mod3_b, _ = lax.optimization_barrier((mod3, wgt))
            pending = (x1, r0, peers, b)
        p_x1, p_r0, p_peers, p_seq = pending
        outs.append(_residual(p_x1, p_r0, p_peers, p_seq, mod3, g_final[None, :], last))
        x = jnp.concatenate(outs, axis=0).reshape(B, S, D)
    return x
```

```python
import functools
import math

import jax
import jax.numpy as jnp
import numpy as np
from jax import lax
from jax.experimental import pallas as pl
from jax.experimental.pallas import tpu as pltpu
from jax.experimental.pallas import tpu_sc as plsc

D_MODEL = 1024
POOL_WIDTH = 512
SGU_WIDTH = 512
POOL_WINDOWS = (2, 4, 8, 16)
GROUP_DIM = 128
CHUNK = 128
SGU_HEADS = 4
IN_PROJ_WIDTH = POOL_WIDTH + 2 * SGU_WIDTH
PEER_HEADS = 8
N_KEYS = 128
HALF_KEY = 128
TOPK = 16
N_MOD = 6
EPS = 1e-6

MIX_BLOCK = 512
TOPK_BLOCK = 256
EW_BLOCK = 256
VMEM_LIMIT_BYTES = 48 * 1024 * 1024

SC_LANES = 16
SC_TOKENS = 32
SC_RING = 8
N_SEL = PEER_HEADS * TOPK
TC_TOKENS = 8

N_CAND = 16 + 7 * 8 + 8


def _pool_band():
    t = np.arange(CHUNK)[:, None]
    j = np.arange(2 * CHUNK)[None, :]
    bands = []
    for w in POOL_WINDOWS:
        m = (j > CHUNK + t - w) & (j <= CHUNK + t)
        bands.append(np.where(m, 1.0 / w, 0.0))
    return np.stack(bands).astype(np.float32)


def _bf16_dot(a, b):
    return jnp.dot(a.astype(jnp.bfloat16), b.astype(jnp.bfloat16), preferred_element_type=jnp.float32)


def _ada_kernel(c_ref, w_ref, b_ref, o_ref):
    c = c_ref[...]
    c_act = c * jax.nn.sigmoid(c)
    o_ref[...] = _bf16_dot(c_act, w_ref[...]) + b_ref[...]


def _ada_mod(c_pad, w_ada, b_ada):
    rows = c_pad.shape[0]
    return pl.pallas_call(
        _ada_kernel,
        grid=(N_MOD,),
        in_specs=[
            pl.BlockSpec((rows, D_MODEL), lambda i: (0, 0)),
            pl.BlockSpec((D_MODEL, D_MODEL), lambda i: (0, i)),
            pl.BlockSpec((1, D_MODEL), lambda i: (0, i)),
        ],
        out_specs=pl.BlockSpec((rows, D_MODEL), lambda i: (0, i)),
        out_shape=jax.ShapeDtypeStruct((rows, N_MOD * D_MODEL), jnp.float32),
        name="ada_mod",
    )(c_pad, w_ada, b_ada)


def _rms(x, g):
    return x * lax.rsqrt(jnp.mean(x * x, axis=-1, keepdims=True) + EPS) * g


def _mix_kernel(x_ref, mod_ref, g1_ref, win_ref, band_ref, wpool_ref, pscale_ref, lng_ref, lnb_ref,
                wsp_ref, bsp_ref, wout_ref, g2_ref, x1_ref, h2_ref, aext_ref, mixed_ref):
    j = pl.program_id(1)
    x = x_ref[0]
    shift1, scale1, gate1 = mod_ref[0, 0:1, :], mod_ref[0, 1:2, :], mod_ref[0, 2:3, :]
    shift2, scale2 = mod_ref[0, 3:4, :], mod_ref[0, 4:5, :]

    h = _rms(x, g1_ref[...]) * (1.0 + scale1) + shift1
    proj = _bf16_dot(h, win_ref[...])
    a = proj[:, :POOL_WIDTH]
    u = proj[:, POOL_WIDTH:POOL_WIDTH + SGU_WIDTH]
    v = proj[:, POOL_WIDTH + SGU_WIDTH:]

    @pl.when(j == 0)
    def _():
        aext_ref[0:CHUNK, :] = jnp.zeros((CHUNK, POOL_WIDTH), jnp.float32)

    aext_ref[CHUNK:, :] = a

    mu = jnp.mean(v, axis=-1, keepdims=True)
    vc = v - mu
    vn = vc * lax.rsqrt(jnp.mean(vc * vc, axis=-1, keepdims=True) + EPS) * lng_ref[...] + lnb_ref[...]

    row = lax.broadcasted_iota(jnp.int32, (CHUNK, CHUNK), 0)
    col = lax.broadcasted_iota(jnp.int32, (CHUNK, CHUNK), 1)
    tril = col <= row
    t_col = lax.broadcasted_iota(jnp.int32, (CHUNK, 1), 0)

    for c in range(MIX_BLOCK // CHUNK):
        r0 = c * CHUNK
        pos = (j * MIX_BLOCK + r0 + 1 + t_col).astype(jnp.float32)
        for g, w in enumerate(POOL_WINDOWS):
            c0 = g * GROUP_DIM
            seg = aext_ref[r0:r0 + 2 * CHUNK, c0:c0 + GROUP_DIM]
            hi = seg.astype(jnp.bfloat16)
            lo = (seg - hi.astype(jnp.float32)).astype(jnp.bfloat16)
            band = band_ref[g]
            win = (jnp.dot(band, hi, preferred_element_type=jnp.float32)
                   + jnp.dot(band, lo, preferred_element_type=jnp.float32))
            mean = win * (float(w) / jnp.minimum(pos, float(w)))
            pooled = mean - seg[CHUNK:, :]
            y = _bf16_dot(pooled, wpool_ref[g]) * pscale_ref[:, c0:c0 + GROUP_DIM]
            mixed_ref[r0:r0 + CHUNK, c0:c0 + GROUP_DIM] = y.astype(jnp.bfloat16)
        for hh in range(SGU_HEADS):
            c0 = hh * GROUP_DIM
            wm = jnp.where(tril, wsp_ref[hh], 0.0)
            m = _bf16_dot(wm, vn[r0:r0 + CHUNK, c0:c0 + GROUP_DIM]) + bsp_ref[:, hh:hh + 1]
            s = u[r0:r0 + CHUNK, c0:c0 + GROUP_DIM] * m
            mixed_ref[r0:r0 + CHUNK, POOL_WIDTH + c0:POOL_WIDTH + c0 + GROUP_DIM] = s.astype(jnp.bfloat16)

    aext_ref[0:CHUNK, :] = aext_ref[MIX_BLOCK:MIX_BLOCK + CHUNK, :]

    x1 = x + gate1 * jnp.dot(mixed_ref[...], wout_ref[...], preferred_element_type=jnp.float32)
    x1_ref[0] = x1
    h2_ref[0] = _rms(x1, g2_ref[...]) * (1.0 + scale2) + shift2


def _mixer(x, b0, n_seq, mod3, g1, w_in, band, w_pool, pool_scale, ln_g, ln_b, w_sp, b_sp_t, w_out, g2):
    _, S, D = x.shape
    const2 = lambda b, j: (0, 0)
    const3 = lambda b, j: (0, 0, 0)
    tok = pl.BlockSpec((1, MIX_BLOCK, D), lambda b, j: (b, j, 0))
    return pl.pallas_call(
        _mix_kernel,
        grid=(n_seq, S // MIX_BLOCK),
        in_specs=[
            pl.BlockSpec((1, MIX_BLOCK, D), lambda b, j: (b0 + b, j, 0)),
            pl.BlockSpec((1, N_MOD, D), lambda b, j: (b0 + b, 0, 0)),
            pl.BlockSpec((1, D), const2),
            pl.BlockSpec((D, IN_PROJ_WIDTH), const2),
            pl.BlockSpec((len(POOL_WINDOWS), CHUNK, 2 * CHUNK), const3),
            pl.BlockSpec((len(POOL_WINDOWS), GROUP_DIM, GROUP_DIM), const3),
            pl.BlockSpec((1, POOL_WIDTH), const2),
            pl.BlockSpec((1, SGU_WIDTH), const2),
            pl.BlockSpec((1, SGU_WIDTH), const2),
            pl.BlockSpec((SGU_HEADS, CHUNK, CHUNK), const3),
            pl.BlockSpec((CHUNK, SGU_HEADS), const2),
            pl.BlockSpec((D, D), const2),
            pl.BlockSpec((1, D), const2),
        ],
        out_specs=[tok, tok],
        out_shape=[jax.ShapeDtypeStruct((n_seq, S, D), jnp.float32)] * 2,
        scratch_shapes=[
            pltpu.VMEM((MIX_BLOCK + CHUNK, POOL_WIDTH), jnp.float32),
            pltpu.VMEM((MIX_BLOCK, D), jnp.bfloat16),
        ],
        compiler_params=pltpu.CompilerParams(
            dimension_semantics=("arbitrary", "arbitrary"), vmem_limit_bytes=VMEM_LIMIT_BYTES),
        name="mixer",
    )(x, mod3, g1, w_in, band, w_pool, pool_scale, ln_g, ln_b, w_sp, b_sp_t, w_out, g2)


def _extract_top(s, ids, payload, n_out):
    vals, pays = [], []
    for _ in range(n_out):
        m = jnp.max(s, axis=0, keepdims=True)
        pick = jnp.min(jnp.where(s == m, ids, jnp.float32(1e9)), axis=0, keepdims=True)
        sel = ids == pick
        vals.append(m)
        pays.append(pick if payload is None
                    else jnp.max(jnp.where(sel, payload, -1.0), axis=0, keepdims=True))
        s = jnp.where(sel, -jnp.inf, s)
    return jnp.concatenate(vals, axis=0), jnp.concatenate(pays, axis=0)


def _topk_kernel(h2_ref, wq_ref, keys_ref, idx_ref, gate_ref):
    T = TOPK_BLOCK
    q = _bf16_dot(h2_ref[...], wq_ref[...])
    key_ids = lax.broadcasted_iota(jnp.int32, (N_KEYS, T), 0).astype(jnp.float32)
    row = lax.broadcasted_iota(jnp.int32, (N_CAND, T), 0)
    flat = jnp.where(row < 16, row,
                     jnp.where(row < 72, jnp.right_shift(row - 8, 3) * 16 + jnp.bitwise_and(row, 7),
                               (row - 64) * 16)).astype(jnp.float32)
    idx_rows, gate_rows = [], []
    for hd in range(PEER_HEADS):
        tops = []
        for p in range(2):
            hp = hd * 2 + p
            qs = q[:, hp * HALF_KEY:(hp + 1) * HALF_KEY].astype(jnp.bfloat16)
            st = lax.dot_general(keys_ref[hp], qs, (((1,), (1,)), ((), ())),
                                 preferred_element_type=jnp.float32)
            tops.append(_extract_top(st, key_ids, None, TOPK))
        (s0, i0), (s1, i1) = tops
        i0 = i0 * float(N_KEYS)
        cand = jnp.concatenate(
            [s0[0:1] + s1] + [s0[a:a + 1] + s1[0:8] for a in range(1, 8)] + [s0[8:16] + s1[0:1]], axis=0)
        cidx = jnp.concatenate(
            [i0[0:1] + i1] + [i0[a:a + 1] + i1[0:8] for a in range(1, 8)] + [i0[8:16] + i1[0:1]],
            axis=0)
        best, sel_idx = _extract_top(cand, flat, cidx, TOPK)
        e = jnp.exp(best - jnp.max(best, axis=0, keepdims=True))
        gate_rows.append(e / jnp.sum(e, axis=0, keepdims=True))
        idx_rows.append(sel_idx)
    idx_ref[...] = jnp.concatenate(idx_rows, axis=0).T.astype(jnp.int32)
    gate_ref[...] = jnp.concatenate(gate_rows, axis=0).T


def _query_topk(h2, w_query, keys, row0, N):
    D = h2.shape[1]
    blk0 = row0 // TOPK_BLOCK
    tok = pl.BlockSpec((TOPK_BLOCK, N_SEL), lambda i: (i, 0))
    return pl.pallas_call(
        _topk_kernel,
        grid=(N // TOPK_BLOCK,),
        in_specs=[
            pl.BlockSpec((TOPK_BLOCK, D), lambda i: (blk0 + i, 0)),
            pl.BlockSpec((D, 2 * PEER_HEADS * HALF_KEY), lambda i: (0, 0)),
            pl.BlockSpec((2 * PEER_HEADS, N_KEYS, HALF_KEY), lambda i: (0, 0, 0)),
        ],
        out_specs=[tok, tok],
        out_shape=[jax.ShapeDtypeStruct((N, N_SEL), jnp.int32), jax.ShapeDtypeStruct((N, N_SEL), jnp.float32)],
        compiler_params=pltpu.CompilerParams(
            dimension_semantics=("arbitrary",), vmem_limit_bytes=VMEM_LIMIT_BYTES),
        name="query_topk",
    )(h2, w_query, keys)


def _sc_mesh_and_workers():
    info = plsc.get_sparse_core_info()
    assert info.num_lanes == SC_LANES
    mesh = plsc.VectorSubcoreMesh(core_axis_name="core", subcore_axis_name="subcore")
    return mesh, info.num_cores, info.num_cores * info.num_subcores


def _gather_ring(tab_hbm, idx_v, rows_v, sems, consume, n_tokens):
    look = SC_RING - 1

    def copy(t, hd):
        slot = hd % SC_RING
        return pltpu.make_async_copy(
            tab_hbm.at[idx_v.at[t, pl.ds(hd * TOPK, TOPK)]], rows_v.at[slot], sems.at[slot])

    for hd in range(look):
        copy(0, hd).start()

    @pl.loop(0, n_tokens)
    def _(t):
        for hd in range(PEER_HEADS):
            nxt = hd + look
            if nxt < PEER_HEADS:
                copy(t, nxt).start()
            else:
                @pl.when(t + 1 < n_tokens)
                def _():
                    copy(t + 1, nxt - PEER_HEADS).start()
            copy(t, hd).wait()
            consume(t, hd, rows_v.at[hd % SC_RING])


def _for_token_blocks(first, per_worker, block):
    n_full, tail = divmod(per_worker, SC_TOKENS)
    assert tail % 8 == 0, "HBM row offsets of the staged slices must stay 8-aligned"

    @pl.loop(0, n_full)
    def _(blk):
        block(first + blk * SC_TOKENS, SC_TOKENS)

    if tail:
        block(first + n_full * SC_TOKENS, tail)


SC_PAIR_WORDS = D_MODEL // 2
SC_STEPS = SC_PAIR_WORDS // SC_LANES
SC_BF16_SUM = 4


def _bf16_bits(a):
    return lax.bitcast_convert_type(a.astype(jnp.bfloat16), jnp.uint16).astype(jnp.uint32)


def _pack_rows(a):
    bits = _bf16_bits(a)
    return lax.bitcast_convert_type(bits[:, :SC_PAIR_WORDS] | (bits[:, SC_PAIR_WORDS:] << 16), jnp.int32)


def _pack_dup(a):
    bits = _bf16_bits(a)
    return lax.bitcast_convert_type(bits | (bits << 16), jnp.int32)


def _unpack_pair(words):
    lo = lax.bitcast_convert_type(lax.shift_left(words, jnp.int32(16)), jnp.float32)
    hi = lax.bitcast_convert_type(lax.bitwise_and(words, jnp.int32(-65536)), jnp.float32)
    return lo, hi


SC_ROW_TILE = (SC_PAIR_WORDS // 128, 128)


def _sc_table(packed):
    return packed.reshape(packed.shape[0], *SC_ROW_TILE)


def _row_run(rows, k, s):
    shift = (128 // SC_LANES).bit_length() - 1
    return rows[k, jnp.right_shift(s, shift), pl.ds(jnp.bitwise_and(s, 128 // SC_LANES - 1) * SC_LANES, SC_LANES)]


def _as_bf16(words):
    return plsc.bitcast(words, jnp.bfloat16)


def _widen_sum(products):
    return _unpack_pair(plsc.bitcast(_tree_sum(products), jnp.int32))


def _tree_sum(terms):
    terms = list(terms)
    while len(terms) > 1:
        terms = [terms[i] + terms[i + 1] for i in range(0, len(terms) - 1, 2)] + (
            [terms[-1]] if len(terms) % 2 else [])
    return terms[0]


def _peer_down(h_packed, idx, packed, N):
    mesh, n_cores, n_workers = _sc_mesh_and_workers()
    per_worker = N // n_workers

    def body(h_hbm, idx_hbm, tab_hbm, out_hbm, idx_v, h_v, rows_v, out_v, sems):
        wid = lax.axis_index("subcore") * n_cores + lax.axis_index("core")
        lane = lax.iota(jnp.int32, SC_LANES)

        def consume(t, hd, rows):
            def dot_step(g, accs):
                steps = [g * SC_BF16_SUM + i for i in range(SC_BF16_SUM)]
                hb = [_as_bf16(h_v[t, pl.ds(s * SC_LANES, SC_LANES)]) for s in steps]
                out = []
                for k, acc in enumerate(accs):
                    lo, hi = _widen_sum([_as_bf16(_row_run(rows, k, s)) * hbi for s, hbi in zip(steps, hb)])
                    out.append(acc + (lo + hi))
                return tuple(out)

            accs = lax.fori_loop(0, SC_STEPS // SC_BF16_SUM, dot_step,
                                 tuple(jnp.zeros((SC_LANES,), jnp.float32) for _ in range(TOPK)))
            tot = jnp.zeros((SC_LANES,), jnp.float32)
            for k in range(TOPK):
                tot = jnp.where(lane == k, jnp.sum(accs[k]), tot)
            out_v[t, pl.ds(hd * TOPK, TOPK)] = tot

        def block(base, n):
            pltpu.sync_copy(idx_hbm.at[pl.ds(base, n)], idx_v.at[pl.ds(0, n)])
            pltpu.sync_copy(h_hbm.at[pl.ds(base, n)], h_v.at[pl.ds(0, n)])
            _gather_ring(tab_hbm, idx_v, rows_v, sems, consume, n)
            pltpu.sync_copy(out_v.at[pl.ds(0, n)], out_hbm.at[pl.ds(base, n)])

        _for_token_blocks(wid * per_worker, per_worker, block)

    return pl.kernel(
        body,
        out_type=jax.ShapeDtypeStruct((N, N_SEL), jnp.float32),
        mesh=mesh,
        scratch_types=[
            pltpu.VMEM((SC_TOKENS, N_SEL), jnp.int32),
            pltpu.VMEM((SC_TOKENS, SC_PAIR_WORDS), jnp.int32),
            pltpu.VMEM((SC_RING, TOPK) + SC_ROW_TILE, jnp.int32),
            pltpu.VMEM((SC_TOKENS, N_SEL), jnp.float32),
            pltpu.SemaphoreType.DMA((SC_RING,)),
        ],
        compiler_params=pltpu.CompilerParams(needs_layout_passes=False),
        name="peer_down",
    )(h_packed, idx, packed)


def _peer_up(w_dup, idx, packed, N):
    D = D_MODEL
    mesh, n_cores, n_workers = _sc_mesh_and_workers()
    per_worker = N // n_workers

    def body(w_hbm, idx_hbm, tab_hbm, out_hbm, idx_v, w_v, rows_v, out_v, sems):
        wid = lax.axis_index("subcore") * n_cores + lax.axis_index("core")

        def consume(t, hd, rows):
            t_vec = jnp.full((SC_LANES,), t, jnp.int32)
            ws = [_as_bf16(plsc.load_gather(w_v, [t_vec, jnp.full((SC_LANES,), hd * TOPK + k, jnp.int32)]))
                  for k in range(TOPK)]

            @plsc.parallel_loop(0, SC_STEPS)
            def _(s):
                parts = [_widen_sum([_as_bf16(_row_run(rows, k, s)) * ws[k] for k in range(g, g + SC_BF16_SUM)])
                         for g in range(0, TOPK, SC_BF16_SUM)]
                for half, c in enumerate((s * SC_LANES, SC_PAIR_WORDS + s * SC_LANES)):
                    sl = pl.ds(c, SC_LANES)
                    terms = [p[half] for p in parts]
                    if hd > 0:
                        terms.append(out_v[t, sl])
                    out_v[t, sl] = _tree_sum(terms)

        def block(base, n):
            pltpu.sync_copy(idx_hbm.at[pl.ds(base, n)], idx_v.at[pl.ds(0, n)])
            pltpu.sync_copy(w_hbm.at[pl.ds(base, n)], w_v.at[pl.ds(0, n)])
            _gather_ring(tab_hbm, idx_v, rows_v, sems, consume, n)
            pltpu.sync_copy(out_v.at[pl.ds(0, n)], out_hbm.at[pl.ds(base, n)])

        _for_token_blocks(wid * per_worker, per_worker, block)

    return pl.kernel(
        body,
        out_type=jax.ShapeDtypeStruct((N, D), jnp.float32),
        mesh=mesh,
        scratch_types=[
            pltpu.VMEM((SC_TOKENS, N_SEL), jnp.int32),
            pltpu.VMEM((SC_TOKENS, N_SEL), jnp.int32),
            pltpu.VMEM((SC_RING, TOPK) + SC_ROW_TILE, jnp.int32),
            pltpu.VMEM((SC_TOKENS, D), jnp.float32),
            pltpu.SemaphoreType.DMA((SC_RING,)),
        ],
        compiler_params=pltpu.CompilerParams(needs_layout_passes=False),
        name="peer_up",
    )(w_dup, idx, packed)


def _peer_tc_kernel(idx_cur, idx_nxt, h_ref, gate_ref, tab_hbm, o_ref, gbuf, sems):
    i = pl.program_id(0)
    n = pl.num_programs(0)
    groups = N_SEL // 8

    def issue(idx_ref, slot):
        def body(g, carry):
            for j in range(8):
                e = idx_ref[g * 8 + j]
                pltpu.make_async_copy(
                    tab_hbm.at[pl.ds(e, 1), :], gbuf.at[slot, g, pl.ds(j, 1), :], sems.at[slot]).start()
            return carry
        lax.fori_loop(0, TC_TOKENS * groups, body, 0)

    @pl.when(i == 0)
    def _():
        issue(idx_cur, 0)

    @pl.when(i + 1 < n)
    def _():
        issue(idx_nxt, (i + 1) % 2)

    slot = i % 2
    pltpu.make_async_copy(gbuf.at[slot], gbuf.at[slot], sems.at[slot]).wait()

    half = SC_PAIR_WORDS

    def rows(t, c0):
        w = gbuf[slot, t * groups:(t + 1) * groups, :, c0:c0 + half].reshape(N_SEL, half)
        lo = lax.bitcast_convert_type(lax.shift_left(w, jnp.int32(16)), jnp.float32)
        hi = lax.bitcast_convert_type(lax.bitwise_and(w, jnp.int32(-65536)), jnp.float32)
        return lo, hi

    lane = lax.broadcasted_iota(jnp.int32, (N_SEL, 128), 1)
    pre_all = jnp.zeros((N_SEL, 128), jnp.float32)
    for t in range(TC_TOKENS):
        lo, hi = rows(t, 0)
        prod = lo * h_ref[t:t + 1, 0:half] + hi * h_ref[t:t + 1, half:D_MODEL]
        part = prod[:, 0:128]
        for c in range(1, half // 128):
            part = part + prod[:, c * 128:(c + 1) * 128]
        pre = jnp.sum(part, axis=1, keepdims=True)
        pre_all = jnp.where(lane == t, pre, pre_all)
    act = 0.5 * pre_all * (1.0 + lax.erf(pre_all * (1.0 / math.sqrt(2.0))))
    gate_t = jnp.concatenate([gate_ref[...]] * (128 // TC_TOKENS), axis=0).T
    wgt = gate_t * act
    for t in range(TC_TOKENS):
        w_col = jnp.sum(jnp.where(lane == t, wgt, 0.0), axis=1, keepdims=True)
        lo, hi = rows(t, half)
        o_ref[t:t + 1, 0:half] = jnp.sum(lo * w_col, axis=0, keepdims=True)
        o_ref[t:t + 1, half:D_MODEL] = jnp.sum(hi * w_col, axis=0, keepdims=True)


def _peer_tc(h2, h_row0, idx, gate, table2, row0, n_tokens):
    D = h2.shape[1]
    steps = n_tokens // TC_TOKENS
    blk0 = row0 // TC_TOKENS
    h_blk0 = (h_row0 + row0) // TC_TOKENS
    idx_flat = idx.reshape(-1)
    return pl.pallas_call(
        _peer_tc_kernel,
        grid=(steps,),
        in_specs=[
            pl.BlockSpec((TC_TOKENS * N_SEL,), lambda i: (blk0 + i,), memory_space=pltpu.SMEM),
            pl.BlockSpec((TC_TOKENS * N_SEL,), lambda i: (blk0 + jnp.minimum(i + 1, steps - 1),),
                         memory_space=pltpu.SMEM),
            pl.BlockSpec((TC_TOKENS, D), lambda i: (h_blk0 + i, 0)),
            pl.BlockSpec((TC_TOKENS, N_SEL), lambda i: (blk0 + i, 0)),
            pl.BlockSpec(memory_space=pl.ANY),
        ],
        out_specs=pl.BlockSpec((TC_TOKENS, D), lambda i: (i, 0)),
        out_shape=jax.ShapeDtypeStruct((n_tokens, D), jnp.float32),
        scratch_shapes=[
            pltpu.VMEM((2, TC_TOKENS * N_SEL // 8, 8, 2 * SC_PAIR_WORDS), jnp.int32),
            pltpu.SemaphoreType.DMA((2,)),
        ],
        compiler_params=pltpu.CompilerParams(
            dimension_semantics=("arbitrary",), vmem_limit_bytes=VMEM_LIMIT_BYTES),
        name="peer_tc",
    )(idx_flat, idx_flat, h2, gate, table2)


def _gate_act_kernel(pre_ref, gate_ref, o_ref):
    p = pre_ref[...]
    act = 0.5 * p * (1.0 + lax.erf(p * (1.0 / math.sqrt(2.0))))
    o_ref[...] = gate_ref[...] * act


def _gate_act(pre, gate):
    N = pre.shape[0]
    spec = pl.BlockSpec((EW_BLOCK, N_SEL), lambda i: (i, 0))
    return pl.pallas_call(
        _gate_act_kernel, grid=(N // EW_BLOCK,), in_specs=[spec, spec], out_specs=spec,
        out_shape=jax.ShapeDtypeStruct((N, N_SEL), jnp.float32), name="gate_act",
    )(pre, gate)


def _final_kernel(x1_ref, mod_ref, g_ref, *refs, normalize, starts):
    peer_refs, o_ref = refs[:-1], refs[-1]
    gate2 = mod_ref[0, 5:6, :]
    peer = peer_refs[0][...]
    for p in range(1, len(peer_refs)):
        peer = jnp.where(pl.program_id(0) >= starts[p], peer_refs[p][...], peer)
    y = x1_ref[0] + gate2 * peer
    o_ref[...] = _rms(y, g_ref[...]) if normalize else y


def _residual(x1, row0, peers, b, mod3, g_final, normalize):
    D = x1.shape[2]
    blocks = [p.shape[0] // EW_BLOCK for p in peers]
    starts = [sum(blocks[:p]) for p in range(len(peers))]
    blk0 = row0 // EW_BLOCK

    def piece_spec(start, n):
        return pl.BlockSpec((EW_BLOCK, D), lambda j: (jnp.clip(j - start, 0, n - 1), 0))

    return pl.pallas_call(
        functools.partial(_final_kernel, normalize=normalize, starts=tuple(starts)),
        grid=(sum(blocks),),
        in_specs=[pl.BlockSpec((1, EW_BLOCK, D), lambda j: (0, blk0 + j, 0)),
                  pl.BlockSpec((1, N_MOD, D), lambda j: (b, 0, 0)),
                  pl.BlockSpec((1, D), lambda j: (0, 0))]
                 + [piece_spec(s, n) for s, n in zip(starts, blocks)],
        out_specs=pl.BlockSpec((EW_BLOCK, D), lambda j: (j, 0)),
        out_shape=jax.ShapeDtypeStruct((sum(blocks) * EW_BLOCK, D), jnp.float32),
        compiler_params=pltpu.CompilerParams(vmem_limit_bytes=VMEM_LIMIT_BYTES),
        name="final_norm",
    )(x1, mod3, g_final, *peers)


def _token_split(u, n_units, n):
    unit = n // 16
    if u == n_units - 1 and n_units > 1:
        sc, c = 12, 2
    elif u == 0:
        sc, c = 14, 0
    else:
        sc, c = 13, 0
    n_sc, n_c = sc * unit, c * unit
    n_a = ((n - n_sc - n_c) // 2) // EW_BLOCK * EW_BLOCK
    return n_sc, n_a, n - n_sc - n_c - n_a, n_c


def _token_ranges(n_seq, S):
    return [(b, 0, S) for b in range(n_seq)]


def kernel(x, c, w_ada, b_ada, g_norm1, w_in, w_pool, pool_scale, sgu_ln_g, sgu_ln_b, w_spatial, b_spatial,
           w_out, g_norm2, w_query, sub_keys, expert_down, expert_up, g_final):
    B, S, D = x.shape
    depth = w_ada.shape[0]
    bf16 = jnp.bfloat16
    band = jnp.asarray(_pool_band(), bf16)
    c_pad = jnp.pad(c, ((0, 8 - B), (0, 0)))
    for l in range(depth):
        mod = _ada_mod(c_pad, w_ada[l].astype(bf16), b_ada[l][None, :])[:B]
        mod3 = mod.reshape(B, N_MOD, D)
        mix_w = (g_norm1[l][None, :], w_in[l].astype(bf16), band, w_pool[l].astype(bf16),
                 pool_scale[l][None, :], sgu_ln_g[l][None, :], sgu_ln_b[l][None, :], w_spatial[l],
                 b_spatial[l].T, w_out[l].astype(bf16), g_norm2[l][None, :])
        keys = sub_keys[l].reshape(2 * PEER_HEADS, N_KEYS, HALF_KEY).astype(bf16)
        wq = w_query[l].astype(bf16)
        down_packed = _pack_rows(expert_down[l])
        up_packed = _pack_rows(expert_up[l])
        table2 = jnp.concatenate([down_packed, up_packed], axis=1)
        down_sc, up_sc = _sc_table(down_packed), _sc_table(up_packed)
        last = l + 1 == depth
        outs = []
        pending = None
        mod3_b = mod3
        seq_done = -1
        ranges = _token_ranges(B, S)
        for u, (b, r0, n) in enumerate(ranges):
            n_sc, n_a, n_b, n_c = _token_split(u, len(ranges), n)
            if b != seq_done:
                x1, h2 = _mixer(x, b, 1, mod3_b, *mix_w)
                h2f = h2.reshape(S, D)
                seq_done = b
            idx, gate = _query_topk(h2f, wq, keys, r0, n)
            tc_a = _peer_tc(h2f, r0, idx, gate, table2, n_sc, n_a)
            pre = _peer_down(_pack_rows(h2f[r0:r0 + n_sc]), idx, down_sc, n_sc)
            gate_b = gate
            if pending is not None:
                p_x1, p_r0, p_peers, p_seq = pending
                p_peers[0], tc_a = lax.optimization_barrier((p_peers[0], tc_a))
                out_prev = _residual(p_x1, p_r0, p_peers, p_seq, mod3, g_final[None, :], last)
                outs.append(out_prev)
                gate_b, _ = lax.optimization_barrier((gate, out_prev))
            tc_b = _peer_tc(h2f, r0, idx, gate_b, table2, n_sc + n_a, n_b)
            pre, tc_a, tc_b = lax.optimization_barrier((pre, tc_a, tc_b))
            wgt = _gate_act(pre, gate)
            peers = [_peer_up(_pack_dup(wgt), idx, up_sc, n_sc), tc_a, tc_b]
            if n_c:
                gate_c, _ = lax.optimization_barrier((gate, wgt))
                peers.append(_peer_tc(h2f, r0, idx, gate_c, table2, n_sc + n_a + n_b, n_c))
            mod3_b, _ = lax.optimization_barrier((mod3, wgt))
            pending = (x1, r0, peers, b)
        p_x1, p_r0, p_peers, p_seq = pending
        outs.append(_residual(p_x1, p_r0, p_peers, p_seq, mod3, g_final[None, :], last))
        x = jnp.concatenate(outs, axis=0).reshape(B, S, D)
    return x
```

```python
import functools
import math

import jax
import jax.numpy as jnp
import numpy as np
from jax import lax
from jax.experimental import pallas as pl
from jax.experimental.pallas import tpu as pltpu
from jax.experimental.pallas import tpu_sc as plsc

D_MODEL = 1024
POOL_WIDTH = 512
SGU_WIDTH = 512
POOL_WINDOWS = (2, 4, 8, 16)
GROUP_DIM = 128
CHUNK = 128
SGU_HEADS = 4
IN_PROJ_WIDTH = POOL_WIDTH + 2 * SGU_WIDTH
PEER_HEADS = 8
N_KEYS = 128
HALF_KEY = 128
TOPK = 16
N_MOD = 6
EPS = 1e-6

MIX_BLOCK = 512
TOPK_BLOCK = 256
EW_BLOCK = 256
VMEM_LIMIT_BYTES = 48 * 1024 * 1024

SC_LANES = 16
SC_TOKENS = 32
SC_RING = 8
N_SEL = PEER_HEADS * TOPK
TC_TOKENS = 8

N_CAND = 16 + 7 * 8 + 8


def _pool_band():
    t = np.arange(CHUNK)[:, None]
    j = np.arange(2 * CHUNK)[None, :]
    bands = []
    for w in POOL_WINDOWS:
        m = (j > CHUNK + t - w) & (j <= CHUNK + t)
        bands.append(np.where(m, 1.0 / w, 0.0))
    return np.stack(bands).astype(np.float32)


def _bf16_dot(a, b):
    return jnp.dot(a.astype(jnp.bfloat16), b.astype(jnp.bfloat16), preferred_element_type=jnp.float32)


def _ada_kernel(c_ref, w_ref, b_ref, o_ref):
    c = c_ref[...]
    c_act = c * jax.nn.sigmoid(c)
    o_ref[...] = _bf16_dot(c_act, w_ref[...]) + b_ref[...]


def _ada_mod(c_pad, w_ada, b_ada):
    rows = c_pad.shape[0]
    return pl.pallas_call(
        _ada_kernel,
        grid=(N_MOD,),
        in_specs=[
            pl.BlockSpec((rows, D_MODEL), lambda i: (0, 0)),
            pl.BlockSpec((D_MODEL, D_MODEL), lambda i: (0, i)),
            pl.BlockSpec((1, D_MODEL), lambda i: (0, i)),
        ],
        out_specs=pl.BlockSpec((rows, D_MODEL), lambda i: (0, i)),
        out_shape=jax.ShapeDtypeStruct((rows, N_MOD * D_MODEL), jnp.float32),
        name="ada_mod",
    )(c_pad, w_ada, b_ada)


def _rms(x, g):
    return x * lax.rsqrt(jnp.mean(x * x, axis=-1, keepdims=True) + EPS) * g


def _mix_kernel(x_ref, mod_ref, g1_ref, win_ref, band_ref, wpool_ref, pscale_ref, lng_ref, lnb_ref,
                wsp_ref, bsp_ref, wout_ref, g2_ref, x1_ref, h2_ref, h2p_ref, aext_ref, mixed_ref):
    j = pl.program_id(1)
    x = x_ref[0]
    shift1, scale1, gate1 = mod_ref[0, 0:1, :], mod_ref[0, 1:2, :], mod_ref[0, 2:3, :]
    shift2, scale2 = mod_ref[0, 3:4, :], mod_ref[0, 4:5, :]

    h = _rms(x, g1_ref[...]) * (1.0 + scale1) + shift1
    proj = _bf16_dot(h, win_ref[...])
    a = proj[:, :POOL_WIDTH]
    u = proj[:, POOL_WIDTH:POOL_WIDTH + SGU_WIDTH]
    v = proj[:, POOL_WIDTH + SGU_WIDTH:]

    @pl.when(j == 0)
    def _():
        aext_ref[0:CHUNK, :] = jnp.zeros((CHUNK, POOL_WIDTH), jnp.float32)

    aext_ref[CHUNK:, :] = a

    mu = jnp.mean(v, axis=-1, keepdims=True)
    vc = v - mu
    vn = vc * lax.rsqrt(jnp.mean(vc * vc, axis=-1, keepdims=True) + EPS) * lng_ref[...] + lnb_ref[...]

    row = lax.broadcasted_iota(jnp.int32, (CHUNK, CHUNK), 0)
    col = lax.broadcasted_iota(jnp.int32, (CHUNK, CHUNK), 1)
    tril = col <= row
    t_col = lax.broadcasted_iota(jnp.int32, (CHUNK, 1), 0)

    for c in range(MIX_BLOCK // CHUNK):
        r0 = c * CHUNK
        pos = (j * MIX_BLOCK + r0 + 1 + t_col).astype(jnp.float32)
        for g, w in enumerate(POOL_WINDOWS):
            c0 = g * GROUP_DIM
            seg = aext_ref[r0:r0 + 2 * CHUNK, c0:c0 + GROUP_DIM]
            hi = seg.astype(jnp.bfloat16)
            lo = (seg - hi.astype(jnp.float32)).astype(jnp.bfloat16)
            band = band_ref[g]
            win = (jnp.dot(band, hi, preferred_element_type=jnp.float32)
                   + jnp.dot(band, lo, preferred_element_type=jnp.float32))
            mean = win * (float(w) / jnp.minimum(pos, float(w)))
            pooled = mean - seg[CHUNK:, :]
            y = _bf16_dot(pooled, wpool_ref[g]) * pscale_ref[:, c0:c0 + GROUP_DIM]
            mixed_ref[r0:r0 + CHUNK, c0:c0 + GROUP_DIM] = y.astype(jnp.bfloat16)
        for hh in range(SGU_HEADS):
            c0 = hh * GROUP_DIM
            wm = jnp.where(tril, wsp_ref[hh], 0.0)
            m = _bf16_dot(wm, vn[r0:r0 + CHUNK, c0:c0 + GROUP_DIM]) + bsp_ref[:, hh:hh + 1]
            s = u[r0:r0 + CHUNK, c0:c0 + GROUP_DIM] * m
            mixed_ref[r0:r0 + CHUNK, POOL_WIDTH + c0:POOL_WIDTH + c0 + GROUP_DIM] = s.astype(jnp.bfloat16)

    aext_ref[0:CHUNK, :] = aext_ref[MIX_BLOCK:MIX_BLOCK + CHUNK, :]

    x1 = x + gate1 * jnp.dot(mixed_ref[...], wout_ref[...], preferred_element_type=jnp.float32)
    x1_ref[0] = x1
    h2 = _rms(x1, g2_ref[...]) * (1.0 + scale2) + shift2
    h2_ref[0] = h2
    bits = lax.bitcast_convert_type(h2.astype(jnp.bfloat16).astype(jnp.float32), jnp.int32)
    half = D_MODEL // 2
    h2p_ref[0] = lax.shift_right_logical(bits[:, :half], jnp.int32(16)) | (bits[:, half:] & jnp.int32(-65536))


def _mixer(x, b0, n_seq, mod3, g1, w_in, band, w_pool, pool_scale, ln_g, ln_b, w_sp, b_sp_t, w_out, g2):
    _, S, D = x.shape
    const2 = lambda b, j: (0, 0)
    const3 = lambda b, j: (0, 0, 0)
    tok = pl.BlockSpec((1, MIX_BLOCK, D), lambda b, j: (b, j, 0))
    return pl.pallas_call(
        _mix_kernel,
        grid=(n_seq, S // MIX_BLOCK),
        in_specs=[
            pl.BlockSpec((1, MIX_BLOCK, D), lambda b, j: (b0 + b, j, 0)),
            pl.BlockSpec((1, N_MOD, D), lambda b, j: (b0 + b, 0, 0)),
            pl.BlockSpec((1, D), const2),
            pl.BlockSpec((D, IN_PROJ_WIDTH), const2),
            pl.BlockSpec((len(POOL_WINDOWS), CHUNK, 2 * CHUNK), const3),
            pl.BlockSpec((len(POOL_WINDOWS), GROUP_DIM, GROUP_DIM), const3),
            pl.BlockSpec((1, POOL_WIDTH), const2),
            pl.BlockSpec((1, SGU_WIDTH), const2),
            pl.BlockSpec((1, SGU_WIDTH), const2),
            pl.BlockSpec((SGU_HEADS, CHUNK, CHUNK), const3),
            pl.BlockSpec((CHUNK, SGU_HEADS), const2),
            pl.BlockSpec((D, D), const2),
            pl.BlockSpec((1, D), const2),
        ],
        out_specs=[tok, tok, pl.BlockSpec((1, MIX_BLOCK, D // 2), lambda b, j: (b, j, 0))],
        out_shape=[jax.ShapeDtypeStruct((n_seq, S, D), jnp.float32)] * 2
                  + [jax.ShapeDtypeStruct((n_seq, S, D // 2), jnp.int32)],
        scratch_shapes=[
            pltpu.VMEM((MIX_BLOCK + CHUNK, POOL_WIDTH), jnp.float32),
            pltpu.VMEM((MIX_BLOCK, D), jnp.bfloat16),
        ],
        compiler_params=pltpu.CompilerParams(
            dimension_semantics=("arbitrary", "arbitrary"), vmem_limit_bytes=VMEM_LIMIT_BYTES),
        name="mixer",
    )(x, mod3, g1, w_in, band, w_pool, pool_scale, ln_g, ln_b, w_sp, b_sp_t, w_out, g2)


def _extract_top(s, ids, payload, n_out):
    vals, pays = [], []
    for _ in range(n_out):
        m = jnp.max(s, axis=0, keepdims=True)
        pick = jnp.min(jnp.where(s == m, ids, jnp.float32(1e9)), axis=0, keepdims=True)
        sel = ids == pick
        vals.append(m)
        pays.append(pick if payload is None
                    else jnp.max(jnp.where(sel, payload, -1.0), axis=0, keepdims=True))
        s = jnp.where(sel, -jnp.inf, s)
    return jnp.concatenate(vals, axis=0), jnp.concatenate(pays, axis=0)


def _topk_kernel(h2_ref, wq_ref, keys_ref, idx_ref, gate_ref):
    T = TOPK_BLOCK
    q = _bf16_dot(h2_ref[...], wq_ref[...])
    key_ids = lax.broadcasted_iota(jnp.int32, (N_KEYS, T), 0).astype(jnp.float32)
    row = lax.broadcasted_iota(jnp.int32, (N_CAND, T), 0)
    flat = jnp.where(row < 16, row,
                     jnp.where(row < 72, jnp.right_shift(row - 8, 3) * 16 + jnp.bitwise_and(row, 7),
                               (row - 64) * 16)).astype(jnp.float32)
    idx_rows, gate_rows = [], []
    for hd in range(PEER_HEADS):
        tops = []
        for p in range(2):
            hp = hd * 2 + p
            qs = q[:, hp * HALF_KEY:(hp + 1) * HALF_KEY].astype(jnp.bfloat16)
            st = lax.dot_general(keys_ref[hp], qs, (((1,), (1,)), ((), ())),
                                 preferred_element_type=jnp.float32)
            tops.append(_extract_top(st, key_ids, None, TOPK))
        (s0, i0), (s1, i1) = tops
        i0 = i0 * float(N_KEYS)
        cand = jnp.concatenate(
            [s0[0:1] + s1] + [s0[a:a + 1] + s1[0:8] for a in range(1, 8)] + [s0[8:16] + s1[0:1]], axis=0)
        cidx = jnp.concatenate(
            [i0[0:1] + i1] + [i0[a:a + 1] + i1[0:8] for a in range(1, 8)] + [i0[8:16] + i1[0:1]],
            axis=0)
        best, sel_idx = _extract_top(cand, flat, cidx, TOPK)
        e = jnp.exp(best - jnp.max(best, axis=0, keepdims=True))
        gate_rows.append(e / jnp.sum(e, axis=0, keepdims=True))
        idx_rows.append(sel_idx)
    idx_ref[...] = jnp.concatenate(idx_rows, axis=0).T.astype(jnp.int32)
    gate_ref[...] = jnp.concatenate(gate_rows, axis=0).T


def _query_topk(h2, w_query, keys, row0, N):
    D = h2.shape[1]
    blk0 = row0 // TOPK_BLOCK
    tok = pl.BlockSpec((TOPK_BLOCK, N_SEL), lambda i: (i, 0))
    return pl.pallas_call(
        _topk_kernel,
        grid=(N // TOPK_BLOCK,),
        in_specs=[
            pl.BlockSpec((TOPK_BLOCK, D), lambda i: (blk0 + i, 0)),
            pl.BlockSpec((D, 2 * PEER_HEADS * HALF_KEY), lambda i: (0, 0)),
            pl.BlockSpec((2 * PEER_HEADS, N_KEYS, HALF_KEY), lambda i: (0, 0, 0)),
        ],
        out_specs=[tok, tok],
        out_shape=[jax.ShapeDtypeStruct((N, N_SEL), jnp.int32), jax.ShapeDtypeStruct((N, N_SEL), jnp.float32)],
        compiler_params=pltpu.CompilerParams(
            dimension_semantics=("arbitrary",), vmem_limit_bytes=VMEM_LIMIT_BYTES),
        name="query_topk",
    )(h2, w_query, keys)


def _sc_mesh_and_workers():
    info = plsc.get_sparse_core_info()
    assert info.num_lanes == SC_LANES
    mesh = plsc.VectorSubcoreMesh(core_axis_name="core", subcore_axis_name="subcore")
    return mesh, info.num_cores, info.num_cores * info.num_subcores


def _gather_ring(tab_hbm, idx_v, rows_v, sems, consume, n_tokens):
    look = SC_RING - 1

    def copy(t, hd):
        slot = hd % SC_RING
        return pltpu.make_async_copy(
            tab_hbm.at[idx_v.at[t, pl.ds(hd * TOPK, TOPK)]], rows_v.at[slot], sems.at[slot])

    for hd in range(look):
        copy(0, hd).start()

    @pl.loop(0, n_tokens)
    def _(t):
        for hd in range(PEER_HEADS):
            nxt = hd + look
            if nxt < PEER_HEADS:
                copy(t, nxt).start()
            else:
                @pl.when(t + 1 < n_tokens)
                def _():
                    copy(t + 1, nxt - PEER_HEADS).start()
            copy(t, hd).wait()
            consume(t, hd, rows_v.at[hd % SC_RING])


def _for_token_blocks(first, per_worker, block):
    n_full, tail = divmod(per_worker, SC_TOKENS)
    assert tail % 8 == 0, "HBM row offsets of the staged slices must stay 8-aligned"

    @pl.loop(0, n_full)
    def _(blk):
        block(first + blk * SC_TOKENS, SC_TOKENS)

    if tail:
        block(first + n_full * SC_TOKENS, tail)


SC_PAIR_WORDS = D_MODEL // 2
SC_STEPS = SC_PAIR_WORDS // SC_LANES
SC_BF16_SUM = 4


def _bf16_bits(a):
    return lax.bitcast_convert_type(a.astype(jnp.bfloat16), jnp.uint16).astype(jnp.uint32)


def _pack_rows(a):
    bits = _bf16_bits(a)
    return lax.bitcast_convert_type(bits[:, :SC_PAIR_WORDS] | (bits[:, SC_PAIR_WORDS:] << 16), jnp.int32)


def _pack_dup(a):
    bits = _bf16_bits(a)
    return lax.bitcast_convert_type(bits | (bits << 16), jnp.int32)


def _unpack_pair(words):
    lo = lax.bitcast_convert_type(lax.shift_left(words, jnp.int32(16)), jnp.float32)
    hi = lax.bitcast_convert_type(lax.bitwise_and(words, jnp.int32(-65536)), jnp.float32)
    return lo, hi


SC_ROW_TILE = (SC_PAIR_WORDS // 128, 128)


def _sc_table(packed):
    return packed.reshape(packed.shape[0], *SC_ROW_TILE)


def _row_run(rows, k, s):
    shift = (128 // SC_LANES).bit_length() - 1
    return rows[k, jnp.right_shift(s, shift), pl.ds(jnp.bitwise_and(s, 128 // SC_LANES - 1) * SC_LANES, SC_LANES)]


def _as_bf16(words):
    return plsc.bitcast(words, jnp.bfloat16)


def _widen_sum(products):
    return _unpack_pair(plsc.bitcast(_tree_sum(products), jnp.int32))


def _tree_sum(terms):
    terms = list(terms)
    while len(terms) > 1:
        terms = [terms[i] + terms[i + 1] for i in range(0, len(terms) - 1, 2)] + (
            [terms[-1]] if len(terms) % 2 else [])
    return terms[0]


def _peer_down(h_packed, idx, packed, N):
    mesh, n_cores, n_workers = _sc_mesh_and_workers()
    per_worker = N // n_workers

    def body(h_hbm, idx_hbm, tab_hbm, out_hbm, idx_v, h_v, rows_v, out_v, sems):
        wid = lax.axis_index("subcore") * n_cores + lax.axis_index("core")
        lane = lax.iota(jnp.int32, SC_LANES)

        def consume(t, hd, rows):
            def dot_step(g, accs):
                steps = [g * SC_BF16_SUM + i for i in range(SC_BF16_SUM)]
                hb = [_as_bf16(h_v[t, pl.ds(s * SC_LANES, SC_LANES)]) for s in steps]
                out = []
                for k, acc in enumerate(accs):
                    lo, hi = _widen_sum([_as_bf16(_row_run(rows, k, s)) * hbi for s, hbi in zip(steps, hb)])
                    out.append(acc + (lo + hi))
                return tuple(out)

            accs = lax.fori_loop(0, SC_STEPS // SC_BF16_SUM, dot_step,
                                 tuple(jnp.zeros((SC_LANES,), jnp.float32) for _ in range(TOPK)))
            tot = jnp.zeros((SC_LANES,), jnp.float32)
            for k in range(TOPK):
                tot = jnp.where(lane == k, jnp.sum(accs[k]), tot)
            out_v[t, pl.ds(hd * TOPK, TOPK)] = tot

        def block(base, n):
            pltpu.sync_copy(idx_hbm.at[pl.ds(base, n)], idx_v.at[pl.ds(0, n)])
            pltpu.sync_copy(h_hbm.at[pl.ds(base, n)], h_v.at[pl.ds(0, n)])
            _gather_ring(tab_hbm, idx_v, rows_v, sems, consume, n)
            pltpu.sync_copy(out_v.at[pl.ds(0, n)], out_hbm.at[pl.ds(base, n)])

        _for_token_blocks(wid * per_worker, per_worker, block)

    return pl.kernel(
        body,
        out_type=jax.ShapeDtypeStruct((N, N_SEL), jnp.float32),
        mesh=mesh,
        scratch_types=[
            pltpu.VMEM((SC_TOKENS, N_SEL), jnp.int32),
            pltpu.VMEM((SC_TOKENS, SC_PAIR_WORDS), jnp.int32),
            pltpu.VMEM((SC_RING, TOPK) + SC_ROW_TILE, jnp.int32),
            pltpu.VMEM((SC_TOKENS, N_SEL), jnp.float32),
            pltpu.SemaphoreType.DMA((SC_RING,)),
        ],
        compiler_params=pltpu.CompilerParams(needs_layout_passes=False),
        name="peer_down",
    )(h_packed, idx, packed)


def _peer_up(w_dup, idx, packed, N):
    D = D_MODEL
    mesh, n_cores, n_workers = _sc_mesh_and_workers()
    per_worker = N // n_workers

    def body(w_hbm, idx_hbm, tab_hbm, out_hbm, idx_v, w_v, rows_v, out_v, sems):
        wid = lax.axis_index("subcore") * n_cores + lax.axis_index("core")

        def consume(t, hd, rows):
            t_vec = jnp.full((SC_LANES,), t, jnp.int32)
            ws = [_as_bf16(plsc.load_gather(w_v, [t_vec, jnp.full((SC_LANES,), hd * TOPK + k, jnp.int32)]))
                  for k in range(TOPK)]

            @plsc.parallel_loop(0, SC_STEPS)
            def _(s):
                parts = [_widen_sum([_as_bf16(_row_run(rows, k, s)) * ws[k] for k in range(g, g + SC_BF16_SUM)])
                         for g in range(0, TOPK, SC_BF16_SUM)]
                for half, c in enumerate((s * SC_LANES, SC_PAIR_WORDS + s * SC_LANES)):
                    sl = pl.ds(c, SC_LANES)
                    terms = [p[half] for p in parts]
                    if hd > 0:
                        terms.append(out_v[t, sl])
                    out_v[t, sl] = _tree_sum(terms)

        def block(base, n):
            pltpu.sync_copy(idx_hbm.at[pl.ds(base, n)], idx_v.at[pl.ds(0, n)])
            pltpu.sync_copy(w_hbm.at[pl.ds(base, n)], w_v.at[pl.ds(0, n)])
            _gather_ring(tab_hbm, idx_v, rows_v, sems, consume, n)
            pltpu.sync_copy(out_v.at[pl.ds(0, n)], out_hbm.at[pl.ds(base, n)])

        _for_token_blocks(wid * per_worker, per_worker, block)

    return pl.kernel(
        body,
        out_type=jax.ShapeDtypeStruct((N, D), jnp.float32),
        mesh=mesh,
        scratch_types=[
            pltpu.VMEM((SC_TOKENS, N_SEL), jnp.int32),
            pltpu.VMEM((SC_TOKENS, N_SEL), jnp.int32),
            pltpu.VMEM((SC_RING, TOPK) + SC_ROW_TILE, jnp.int32),
            pltpu.VMEM((SC_TOKENS, D), jnp.float32),
            pltpu.SemaphoreType.DMA((SC_RING,)),
        ],
        compiler_params=pltpu.CompilerParams(needs_layout_passes=False),
        name="peer_up",
    )(w_dup, idx, packed)


def _peer_tc_kernel(idx_cur, idx_nxt, h_ref, gate_ref, tab_hbm, o_ref, gbuf, sems):
    i = pl.program_id(0)
    n = pl.num_programs(0)
    groups = N_SEL // 8

    def issue(idx_ref, slot):
        def body(g, carry):
            for j in range(8):
                e = idx_ref[g * 8 + j]
                pltpu.make_async_copy(
                    tab_hbm.at[pl.ds(e, 1), :], gbuf.at[slot, g, pl.ds(j, 1), :], sems.at[slot]).start()
            return carry
        lax.fori_loop(0, TC_TOKENS * groups, body, 0)

    @pl.when(i == 0)
    def _():
        issue(idx_cur, 0)

    @pl.when(i + 1 < n)
    def _():
        issue(idx_nxt, (i + 1) % 2)

    slot = i % 2
    pltpu.make_async_copy(gbuf.at[slot], gbuf.at[slot], sems.at[slot]).wait()

    half = SC_PAIR_WORDS

    def rows(t, c0):
        w = gbuf[slot, t * groups:(t + 1) * groups, :, c0:c0 + half].reshape(N_SEL, half)
        lo = lax.bitcast_convert_type(lax.shift_left(w, jnp.int32(16)), jnp.float32)
        hi = lax.bitcast_convert_type(lax.bitwise_and(w, jnp.int32(-65536)), jnp.float32)
        return lo, hi

    lane = lax.broadcasted_iota(jnp.int32, (N_SEL, 128), 1)
    pre_all = jnp.zeros((N_SEL, 128), jnp.float32)
    for t in range(TC_TOKENS):
        lo, hi = rows(t, 0)
        prod = lo * h_ref[t:t + 1, 0:half] + hi * h_ref[t:t + 1, half:D_MODEL]
        part = prod[:, 0:128]
        for c in range(1, half // 128):
            part = part + prod[:, c * 128:(c + 1) * 128]
        pre = jnp.sum(part, axis=1, keepdims=True)
        pre_all = jnp.where(lane == t, pre, pre_all)
    act = 0.5 * pre_all * (1.0 + lax.erf(pre_all * (1.0 / math.sqrt(2.0))))
    gate_t = jnp.concatenate([gate_ref[...]] * (128 // TC_TOKENS), axis=0).T
    wgt = gate_t * act
    for t in range(TC_TOKENS):
        w_col = jnp.sum(jnp.where(lane == t, wgt, 0.0), axis=1, keepdims=True)
        lo, hi = rows(t, half)
        o_ref[t:t + 1, 0:half] = jnp.sum(lo * w_col, axis=0, keepdims=True)
        o_ref[t:t + 1, half:D_MODEL] = jnp.sum(hi * w_col, axis=0, keepdims=True)


def _peer_tc(h2, h_row0, idx, gate, table2, row0, n_tokens):
    D = h2.shape[1]
    steps = n_tokens // TC_TOKENS
    blk0 = row0 // TC_TOKENS
    h_blk0 = (h_row0 + row0) // TC_TOKENS
    idx_flat = idx.reshape(-1)
    return pl.pallas_call(
        _peer_tc_kernel,
        grid=(steps,),
        in_specs=[
            pl.BlockSpec((TC_TOKENS * N_SEL,), lambda i: (blk0 + i,), memory_space=pltpu.SMEM),
            pl.BlockSpec((TC_TOKENS * N_SEL,), lambda i: (blk0 + jnp.minimum(i + 1, steps - 1),),
                         memory_space=pltpu.SMEM),
            pl.BlockSpec((TC_TOKENS, D), lambda i: (h_blk0 + i, 0)),
            pl.BlockSpec((TC_TOKENS, N_SEL), lambda i: (blk0 + i, 0)),
            pl.BlockSpec(memory_space=pl.ANY),
        ],
        out_specs=pl.BlockSpec((TC_TOKENS, D), lambda i: (i, 0)),
        out_shape=jax.ShapeDtypeStruct((n_tokens, D), jnp.float32),
        scratch_shapes=[
            pltpu.VMEM((2, TC_TOKENS * N_SEL // 8, 8, 2 * SC_PAIR_WORDS), jnp.int32),
            pltpu.SemaphoreType.DMA((2,)),
        ],
        compiler_params=pltpu.CompilerParams(
            dimension_semantics=("arbitrary",), vmem_limit_bytes=VMEM_LIMIT_BYTES),
        name="peer_tc",
    )(idx_flat, idx_flat, h2, gate, table2)


def _gate_act_kernel(pre_ref, gate_ref, o_ref):
    p = pre_ref[...]
    act = 0.5 * p * (1.0 + lax.erf(p * (1.0 / math.sqrt(2.0))))
    o_ref[...] = gate_ref[...] * act


def _gate_act(pre, gate):
    N = pre.shape[0]
    spec = pl.BlockSpec((EW_BLOCK, N_SEL), lambda i: (i, 0))
    return pl.pallas_call(
        _gate_act_kernel, grid=(N // EW_BLOCK,), in_specs=[spec, spec], out_specs=spec,
        out_shape=jax.ShapeDtypeStruct((N, N_SEL), jnp.float32), name="gate_act",
    )(pre, gate)


def _final_kernel(x1_ref, mod_ref, g_ref, *refs, normalize, starts):
    peer_refs, o_ref = refs[:-1], refs[-1]
    gate2 = mod_ref[0, 5:6, :]
    peer = peer_refs[0][...]
    for p in range(1, len(peer_refs)):
        peer = jnp.where(pl.program_id(0) >= starts[p], peer_refs[p][...], peer)
    y = x1_ref[0] + gate2 * peer
    o_ref[...] = _rms(y, g_ref[...]) if normalize else y


def _residual(x1, row0, peers, b, mod3, g_final, normalize):
    D = x1.shape[2]
    blocks = [p.shape[0] // EW_BLOCK for p in peers]
    starts = [sum(blocks[:p]) for p in range(len(peers))]
    blk0 = row0 // EW_BLOCK

    def piece_spec(start, n):
        return pl.BlockSpec((EW_BLOCK, D), lambda j: (jnp.clip(j - start, 0, n - 1), 0))

    return pl.pallas_call(
        functools.partial(_final_kernel, normalize=normalize, starts=tuple(starts)),
        grid=(sum(blocks),),
        in_specs=[pl.BlockSpec((1, EW_BLOCK, D), lambda j: (0, blk0 + j, 0)),
                  pl.BlockSpec((1, N_MOD, D), lambda j: (b, 0, 0)),
                  pl.BlockSpec((1, D), lambda j: (0, 0))]
                 + [piece_spec(s, n) for s, n in zip(starts, blocks)],
        out_specs=pl.BlockSpec((EW_BLOCK, D), lambda j: (j, 0)),
        out_shape=jax.ShapeDtypeStruct((sum(blocks) * EW_BLOCK, D), jnp.float32),
        compiler_params=pltpu.CompilerParams(vmem_limit_bytes=VMEM_LIMIT_BYTES),
        name="final_norm",
    )(x1, mod3, g_final, *peers)


def _token_split(u, n_units, n):
    unit = n // 16
    if u == n_units - 1 and n_units > 1:
        sc, c = 12, 2
    elif u == 0:
        sc, c = 14, 0
    else:
        sc, c = 13, 0
    n_sc, n_c = sc * unit, c * unit
    n_a = ((n - n_sc - n_c) // 2) // EW_BLOCK * EW_BLOCK
    return n_sc, n_a, n - n_sc - n_c - n_a, n_c


def _token_ranges(n_seq, S):
    return [(b, 0, S) for b in range(n_seq)]


def kernel(x, c, w_ada, b_ada, g_norm1, w_in, w_pool, pool_scale, sgu_ln_g, sgu_ln_b, w_spatial, b_spatial,
           w_out, g_norm2, w_query, sub_keys, expert_down, expert_up, g_final):
    B, S, D = x.shape
    depth = w_ada.shape[0]
    bf16 = jnp.bfloat16
    band = jnp.asarray(_pool_band(), bf16)
    c_pad = jnp.pad(c, ((0, 8 - B), (0, 0)))
    for l in range(depth):
        mod = _ada_mod(c_pad, w_ada[l].astype(bf16), b_ada[l][None, :])[:B]
        mod3 = mod.reshape(B, N_MOD, D)
        mix_w = (g_norm1[l][None, :], w_in[l].astype(bf16), band, w_pool[l].astype(bf16),
                 pool_scale[l][None, :], sgu_ln_g[l][None, :], sgu_ln_b[l][None, :], w_spatial[l],
                 b_spatial[l].T, w_out[l].astype(bf16), g_norm2[l][None, :])
        keys = sub_keys[l].reshape(2 * PEER_HEADS, N_KEYS, HALF_KEY).astype(bf16)
        wq = w_query[l].astype(bf16)
        down_packed = _pack_rows(expert_down[l])
        up_packed = _pack_rows(expert_up[l])
        table2 = jnp.concatenate([down_packed, up_packed], axis=1)
        down_sc, up_sc = _sc_table(down_packed), _sc_table(up_packed)
        last = l + 1 == depth
        outs = []
        pending = None
        mod3_b = mod3
        seq_done = -1
        ranges = _token_ranges(B, S)
        for u, (b, r0, n) in enumerate(ranges):
            n_sc, n_a, n_b, n_c = _token_split(u, len(ranges), n)
            if b != seq_done:
                x1, h2, h2p = _mixer(x, b, 1, mod3_b, *mix_w)
                h2f = h2.reshape(S, D)
                h2pf = h2p.reshape(S, D // 2)
                seq_done = b
            idx, gate = _query_topk(h2f, wq, keys, r0, n)
            tc_a = _peer_tc(h2f, r0, idx, gate, table2, n_sc, n_a)
            pre = _peer_down(h2pf if r0 == 0 else h2pf[r0:], idx, down_sc, n_sc)
            gate_b = gate
            if pending is not None:
                p_x1, p_r0, p_peers, p_seq = pending
                p_peers[0], tc_a = lax.optimization_barrier((p_peers[0], tc_a))
                out_prev = _residual(p_x1, p_r0, p_peers, p_seq, mod3, g_final[None, :], last)
                outs.append(out_prev)
                gate_b, _ = lax.optimization_barrier((gate, out_prev))
            tc_b = _peer_tc(h2f, r0, idx, gate_b, table2, n_sc + n_a, n_b)
            pre, tc_a, tc_b = lax.optimization_barrier((pre, tc_a, tc_b))
            wgt = _gate_act(pre, gate)
            peers = [_peer_up(_pack_dup(wgt), idx, up_sc, n_sc), tc_a, tc_b]
            if n_c:
                gate_c, _ = lax.optimization_barrier((gate, wgt))
                peers.append(_peer_tc(h2f, r0, idx, gate_c, table2, n_sc + n_a + n_b, n_c))
            mod3_b, _ = lax.optimization_barrier((mod3, wgt))
            pending = (x1, r0, peers, b)
        p_x1, p_r0, p_peers, p_seq = pending
        outs.append(_residual(p_x1, p_r0, p_peers, p_seq, mod3, g_final[None, :], last))
        x = jnp.concatenate(outs, axis=0).reshape(B, S, D)
    return x
```

```python
import functools
import math

import jax
import jax.numpy as jnp
import numpy as np
from jax import lax
from jax.experimental import pallas as pl
from jax.experimental.pallas import tpu as pltpu
from jax.experimental.pallas import tpu_sc as plsc

D_MODEL = 1024
POOL_WIDTH = 512
SGU_WIDTH = 512
POOL_WINDOWS = (2, 4, 8, 16)
GROUP_DIM = 128
CHUNK = 128
SGU_HEADS = 4
IN_PROJ_WIDTH = POOL_WIDTH + 2 * SGU_WIDTH
PEER_HEADS = 8
N_KEYS = 128
HALF_KEY = 128
TOPK = 16
N_MOD = 6
EPS = 1e-6

MIX_BLOCK = 512
TOPK_BLOCK = 256
EW_BLOCK = 256
VMEM_LIMIT_BYTES = 48 * 1024 * 1024

SC_LANES = 16
SC_TOKENS = 32
SC_RING = 8
N_SEL = PEER_HEADS * TOPK
TC_TOKENS = 8

N_CAND = 16 + 7 * 8 + 8


def _pool_band():
    t = np.arange(CHUNK)[:, None]
    j = np.arange(2 * CHUNK)[None, :]
    bands = []
    for w in POOL_WINDOWS:
        m = (j > CHUNK + t - w) & (j <= CHUNK + t)
        bands.append(np.where(m, 1.0 / w, 0.0))
    return np.stack(bands).astype(np.float32)


def _bf16_dot(a, b):
    return jnp.dot(a.astype(jnp.bfloat16), b.astype(jnp.bfloat16), preferred_element_type=jnp.float32)


def _ada_kernel(c_ref, w_ref, b_ref, o_ref):
    c = c_ref[...]
    c_act = c * jax.nn.sigmoid(c)
    o_ref[...] = _bf16_dot(c_act, w_ref[...]) + b_ref[...]


def _ada_mod(c_pad, w_ada, b_ada):
    rows = c_pad.shape[0]
    return pl.pallas_call(
        _ada_kernel,
        grid=(N_MOD,),
        in_specs=[
            pl.BlockSpec((rows, D_MODEL), lambda i: (0, 0)),
            pl.BlockSpec((D_MODEL, D_MODEL), lambda i: (0, i)),
            pl.BlockSpec((1, D_MODEL), lambda i: (0, i)),
        ],
        out_specs=pl.BlockSpec((rows, D_MODEL), lambda i: (0, i)),
        out_shape=jax.ShapeDtypeStruct((rows, N_MOD * D_MODEL), jnp.float32),
        name="ada_mod",
    )(c_pad, w_ada, b_ada)


def _rms(x, g):
    return x * lax.rsqrt(jnp.mean(x * x, axis=-1, keepdims=True) + EPS) * g


def _mix_kernel(x_ref, mod_ref, g1_ref, win_ref, band_ref, wpool_ref, pscale_ref, lng_ref, lnb_ref,
                wsp_ref, bsp_ref, wout_ref, g2_ref, x1_ref, h2_ref, h2p_ref, aext_ref, mixed_ref):
    j = pl.program_id(1)
    x = x_ref[0]
    shift1, scale1, gate1 = mod_ref[0, 0:1, :], mod_ref[0, 1:2, :], mod_ref[0, 2:3, :]
    shift2, scale2 = mod_ref[0, 3:4, :], mod_ref[0, 4:5, :]

    h = _rms(x, g1_ref[...]) * (1.0 + scale1) + shift1
    proj = _bf16_dot(h, win_ref[...])
    a = proj[:, :POOL_WIDTH]
    u = proj[:, POOL_WIDTH:POOL_WIDTH + SGU_WIDTH]
    v = proj[:, POOL_WIDTH + SGU_WIDTH:]

    @pl.when(j == 0)
    def _():
        aext_ref[0:CHUNK, :] = jnp.zeros((CHUNK, POOL_WIDTH), jnp.float32)

    aext_ref[CHUNK:, :] = a

    mu = jnp.mean(v, axis=-1, keepdims=True)
    vc = v - mu
    vn = vc * lax.rsqrt(jnp.mean(vc * vc, axis=-1, keepdims=True) + EPS) * lng_ref[...] + lnb_ref[...]

    row = lax.broadcasted_iota(jnp.int32, (CHUNK, CHUNK), 0)
    col = lax.broadcasted_iota(jnp.int32, (CHUNK, CHUNK), 1)
    tril = col <= row
    t_col = lax.broadcasted_iota(jnp.int32, (CHUNK, 1), 0)

    for c in range(MIX_BLOCK // CHUNK):
        r0 = c * CHUNK
        pos = (j * MIX_BLOCK + r0 + 1 + t_col).astype(jnp.float32)
        for g, w in enumerate(POOL_WINDOWS):
            c0 = g * GROUP_DIM
            seg = aext_ref[r0:r0 + 2 * CHUNK, c0:c0 + GROUP_DIM]
            hi = seg.astype(jnp.bfloat16)
            lo = (seg - hi.astype(jnp.float32)).astype(jnp.bfloat16)
            band = band_ref[g]
            win = (jnp.dot(band, hi, preferred_element_type=jnp.float32)
                   + jnp.dot(band, lo, preferred_element_type=jnp.float32))
            mean = win * (float(w) / jnp.minimum(pos, float(w)))
            pooled = mean - seg[CHUNK:, :]
            y = _bf16_dot(pooled, wpool_ref[g]) * pscale_ref[:, c0:c0 + GROUP_DIM]
            mixed_ref[r0:r0 + CHUNK, c0:c0 + GROUP_DIM] = y.astype(jnp.bfloat16)
        for hh in range(SGU_HEADS):
            c0 = hh * GROUP_DIM
            wm = jnp.where(tril, wsp_ref[hh], 0.0)
            m = _bf16_dot(wm, vn[r0:r0 + CHUNK, c0:c0 + GROUP_DIM]) + bsp_ref[:, hh:hh + 1]
            s = u[r0:r0 + CHUNK, c0:c0 + GROUP_DIM] * m
            mixed_ref[r0:r0 + CHUNK, POOL_WIDTH + c0:POOL_WIDTH + c0 + GROUP_DIM] = s.astype(jnp.bfloat16)

    aext_ref[0:CHUNK, :] = aext_ref[MIX_BLOCK:MIX_BLOCK + CHUNK, :]

    x1 = x + gate1 * jnp.dot(mixed_ref[...], wout_ref[...], preferred_element_type=jnp.float32)
    x1_ref[0] = x1
    h2 = _rms(x1, g2_ref[...]) * (1.0 + scale2) + shift2
    h2_ref[0] = h2
    bits = lax.bitcast_convert_type(h2.astype(jnp.bfloat16).astype(jnp.float32), jnp.int32)
    half = D_MODEL // 2
    h2p_ref[0] = lax.shift_right_logical(bits[:, :half], jnp.int32(16)) | (bits[:, half:] & jnp.int32(-65536))


def _mixer(x, b0, n_seq, mod3, g1, w_in, band, w_pool, pool_scale, ln_g, ln_b, w_sp, b_sp_t, w_out, g2):
    _, S, D = x.shape
    const2 = lambda b, j: (0, 0)
    const3 = lambda b, j: (0, 0, 0)
    tok = pl.BlockSpec((1, MIX_BLOCK, D), lambda b, j: (b, j, 0))
    return pl.pallas_call(
        _mix_kernel,
        grid=(n_seq, S // MIX_BLOCK),
        in_specs=[
            pl.BlockSpec((1, MIX_BLOCK, D), lambda b, j: (b0 + b, j, 0)),
            pl.BlockSpec((1, N_MOD, D), lambda b, j: (b0 + b, 0, 0)),
            pl.BlockSpec((1, D), const2),
            pl.BlockSpec((D, IN_PROJ_WIDTH), const2),
            pl.BlockSpec((len(POOL_WINDOWS), CHUNK, 2 * CHUNK), const3),
            pl.BlockSpec((len(POOL_WINDOWS), GROUP_DIM, GROUP_DIM), const3),
            pl.BlockSpec((1, POOL_WIDTH), const2),
            pl.BlockSpec((1, SGU_WIDTH), const2),
            pl.BlockSpec((1, SGU_WIDTH), const2),
            pl.BlockSpec((SGU_HEADS, CHUNK, CHUNK), const3),
            pl.BlockSpec((CHUNK, SGU_HEADS), const2),
            pl.BlockSpec((D, D), const2),
            pl.BlockSpec((1, D), const2),
        ],
        out_specs=[tok, tok, pl.BlockSpec((1, MIX_BLOCK, D // 2), lambda b, j: (b, j, 0))],
        out_shape=[jax.ShapeDtypeStruct((n_seq, S, D), jnp.float32)] * 2
                  + [jax.ShapeDtypeStruct((n_seq, S, D // 2), jnp.int32)],
        scratch_shapes=[
            pltpu.VMEM((MIX_BLOCK + CHUNK, POOL_WIDTH), jnp.float32),
            pltpu.VMEM((MIX_BLOCK, D), jnp.bfloat16),
        ],
        compiler_params=pltpu.CompilerParams(
            dimension_semantics=("arbitrary", "arbitrary"), vmem_limit_bytes=VMEM_LIMIT_BYTES),
        name="mixer",
    )(x, mod3, g1, w_in, band, w_pool, pool_scale, ln_g, ln_b, w_sp, b_sp_t, w_out, g2)


def _extract_top(s, ids, payload, n_out):
    vals, pays = [], []
    for _ in range(n_out):
        m = jnp.max(s, axis=0, keepdims=True)
        pick = jnp.min(jnp.where(s == m, ids, jnp.float32(1e9)), axis=0, keepdims=True)
        sel = ids == pick
        vals.append(m)
        pays.append(pick if payload is None
                    else jnp.max(jnp.where(sel, payload, -1.0), axis=0, keepdims=True))
        s = jnp.where(sel, -jnp.inf, s)
    return jnp.concatenate(vals, axis=0), jnp.concatenate(pays, axis=0)


def _topk_kernel(h2_ref, wq_ref, keys_ref, idx_ref, gate_ref):
    T = TOPK_BLOCK
    q = _bf16_dot(h2_ref[...], wq_ref[...])
    key_ids = lax.broadcasted_iota(jnp.int32, (N_KEYS, T), 0).astype(jnp.float32)
    row = lax.broadcasted_iota(jnp.int32, (N_CAND, T), 0)
    flat = jnp.where(row < 16, row,
                     jnp.where(row < 72, jnp.right_shift(row - 8, 3) * 16 + jnp.bitwise_and(row, 7),
                               (row - 64) * 16)).astype(jnp.float32)
    idx_rows, gate_rows = [], []
    for hd in range(PEER_HEADS):
        tops = []
        for p in range(2):
            hp = hd * 2 + p
            qs = q[:, hp * HALF_KEY:(hp + 1) * HALF_KEY].astype(jnp.bfloat16)
            st = lax.dot_general(keys_ref[hp], qs, (((1,), (1,)), ((), ())),
                                 preferred_element_type=jnp.float32)
            tops.append(_extract_top(st, key_ids, None, TOPK))
        (s0, i0), (s1, i1) = tops
        i0 = i0 * float(N_KEYS)
        cand = jnp.concatenate(
            [s0[0:1] + s1] + [s0[a:a + 1] + s1[0:8] for a in range(1, 8)] + [s0[8:16] + s1[0:1]], axis=0)
        cidx = jnp.concatenate(
            [i0[0:1] + i1] + [i0[a:a + 1] + i1[0:8] for a in range(1, 8)] + [i0[8:16] + i1[0:1]],
            axis=0)
        best, sel_idx = _extract_top(cand, flat, cidx, TOPK)
        e = jnp.exp(best - jnp.max(best, axis=0, keepdims=True))
        gate_rows.append(e / jnp.sum(e, axis=0, keepdims=True))
        idx_rows.append(sel_idx)
    idx_ref[...] = jnp.concatenate(idx_rows, axis=0).T.astype(jnp.int32)
    gate_ref[...] = jnp.concatenate(gate_rows, axis=0).T


def _query_topk(h2, w_query, keys, row0, N):
    D = h2.shape[1]
    blk0 = row0 // TOPK_BLOCK
    tok = pl.BlockSpec((TOPK_BLOCK, N_SEL), lambda i: (i, 0))
    return pl.pallas_call(
        _topk_kernel,
        grid=(N // TOPK_BLOCK,),
        in_specs=[
            pl.BlockSpec((TOPK_BLOCK, D), lambda i: (blk0 + i, 0)),
            pl.BlockSpec((D, 2 * PEER_HEADS * HALF_KEY), lambda i: (0, 0)),
            pl.BlockSpec((2 * PEER_HEADS, N_KEYS, HALF_KEY), lambda i: (0, 0, 0)),
        ],
        out_specs=[tok, tok],
        out_shape=[jax.ShapeDtypeStruct((N, N_SEL), jnp.int32), jax.ShapeDtypeStruct((N, N_SEL), jnp.float32)],
        compiler_params=pltpu.CompilerParams(
            dimension_semantics=("arbitrary",), vmem_limit_bytes=VMEM_LIMIT_BYTES),
        name="query_topk",
    )(h2, w_query, keys)


def _sc_mesh_and_workers():
    info = plsc.get_sparse_core_info()
    assert info.num_lanes == SC_LANES
    mesh = plsc.VectorSubcoreMesh(core_axis_name="core", subcore_axis_name="subcore")
    return mesh, info.num_cores, info.num_cores * info.num_subcores


def _gather_ring(tab_hbm, idx_v, rows_v, sems, consume, n_tokens):
    look = SC_RING - 1

    def copy(t, hd):
        slot = hd % SC_RING
        return pltpu.make_async_copy(
            tab_hbm.at[idx_v.at[t, pl.ds(hd * TOPK, TOPK)]], rows_v.at[slot], sems.at[slot])

    for hd in range(look):
        copy(0, hd).start()

    @pl.loop(0, n_tokens)
    def _(t):
        for hd in range(PEER_HEADS):
            nxt = hd + look
            if nxt < PEER_HEADS:
                copy(t, nxt).start()
            else:
                @pl.when(t + 1 < n_tokens)
                def _():
                    copy(t + 1, nxt - PEER_HEADS).start()
            copy(t, hd).wait()
            consume(t, hd, rows_v.at[hd % SC_RING])


def _for_token_blocks(first, per_worker, block):
    n_full, tail = divmod(per_worker, SC_TOKENS)
    assert tail % 8 == 0, "HBM row offsets of the staged slices must stay 8-aligned"

    @pl.loop(0, n_full)
    def _(blk):
        block(first + blk * SC_TOKENS, SC_TOKENS)

    if tail:
        block(first + n_full * SC_TOKENS, tail)


SC_PAIR_WORDS = D_MODEL // 2
SC_STEPS = SC_PAIR_WORDS // SC_LANES
SC_BF16_SUM = 4


def _bf16_bits(a):
    return lax.bitcast_convert_type(a.astype(jnp.bfloat16), jnp.uint16).astype(jnp.uint32)


def _pack_rows(a):
    bits = _bf16_bits(a)
    return lax.bitcast_convert_type(bits[:, :SC_PAIR_WORDS] | (bits[:, SC_PAIR_WORDS:] << 16), jnp.int32)


def _pack_dup(a):
    bits = _bf16_bits(a)
    return lax.bitcast_convert_type(bits | (bits << 16), jnp.int32)


def _unpack_pair(words):
    lo = lax.bitcast_convert_type(lax.shift_left(words, jnp.int32(16)), jnp.float32)
    hi = lax.bitcast_convert_type(lax.bitwise_and(words, jnp.int32(-65536)), jnp.float32)
    return lo, hi


SC_ROW_TILE = (SC_PAIR_WORDS // 128, 128)


def _sc_table(packed):
    return packed.reshape(packed.shape[0], *SC_ROW_TILE)


def _row_run(rows, k, s):
    shift = (128 // SC_LANES).bit_length() - 1
    return rows[k, jnp.right_shift(s, shift), pl.ds(jnp.bitwise_and(s, 128 // SC_LANES - 1) * SC_LANES, SC_LANES)]


def _as_bf16(words):
    return plsc.bitcast(words, jnp.bfloat16)


def _widen_sum(products):
    return _unpack_pair(plsc.bitcast(_tree_sum(products), jnp.int32))


def _tree_sum(terms):
    terms = list(terms)
    while len(terms) > 1:
        terms = [terms[i] + terms[i + 1] for i in range(0, len(terms) - 1, 2)] + (
            [terms[-1]] if len(terms) % 2 else [])
    return terms[0]


def _peer_down(h_packed, idx, packed, N):
    mesh, n_cores, n_workers = _sc_mesh_and_workers()
    per_worker = N // n_workers

    def body(h_hbm, idx_hbm, tab_hbm, out_hbm, idx_v, h_v, rows_v, out_v, sems):
        wid = lax.axis_index("subcore") * n_cores + lax.axis_index("core")
        lane = lax.iota(jnp.int32, SC_LANES)

        def consume(t, hd, rows):
            def dot_step(g, accs):
                steps = [g * SC_BF16_SUM + i for i in range(SC_BF16_SUM)]
                hb = [_as_bf16(h_v[t, pl.ds(s * SC_LANES, SC_LANES)]) for s in steps]
                out = []
                for k, acc in enumerate(accs):
                    lo, hi = _widen_sum([_as_bf16(_row_run(rows, k, s)) * hbi for s, hbi in zip(steps, hb)])
                    out.append(acc + (lo + hi))
                return tuple(out)

            accs = lax.fori_loop(0, SC_STEPS // SC_BF16_SUM, dot_step,
                                 tuple(jnp.zeros((SC_LANES,), jnp.float32) for _ in range(TOPK)))
            tot = jnp.zeros((SC_LANES,), jnp.float32)
            for k in range(TOPK):
                tot = jnp.where(lane == k, jnp.sum(accs[k]), tot)
            out_v[t, pl.ds(hd * TOPK, TOPK)] = tot

        def block(base, n):
            pltpu.sync_copy(idx_hbm.at[pl.ds(base, n)], idx_v.at[pl.ds(0, n)])
            pltpu.sync_copy(h_hbm.at[pl.ds(base, n)], h_v.at[pl.ds(0, n)])
            _gather_ring(tab_hbm, idx_v, rows_v, sems, consume, n)
            pltpu.sync_copy(out_v.at[pl.ds(0, n)], out_hbm.at[pl.ds(base, n)])

        _for_token_blocks(wid * per_worker, per_worker, block)

    return pl.kernel(
        body,
        out_type=jax.ShapeDtypeStruct((N, N_SEL), jnp.float32),
        mesh=mesh,
        scratch_types=[
            pltpu.VMEM((SC_TOKENS, N_SEL), jnp.int32),
            pltpu.VMEM((SC_TOKENS, SC_PAIR_WORDS), jnp.int32),
            pltpu.VMEM((SC_RING, TOPK) + SC_ROW_TILE, jnp.int32),
            pltpu.VMEM((SC_TOKENS, N_SEL), jnp.float32),
            pltpu.SemaphoreType.DMA((SC_RING,)),
        ],
        compiler_params=pltpu.CompilerParams(needs_layout_passes=False),
        name="peer_down",
    )(h_packed, idx, packed)


def _peer_up(w_dup, idx, packed, N):
    D = D_MODEL
    mesh, n_cores, n_workers = _sc_mesh_and_workers()
    per_worker = N // n_workers

    def body(w_hbm, idx_hbm, tab_hbm, out_hbm, idx_v, w_v, rows_v, out_v, sems):
        wid = lax.axis_index("subcore") * n_cores + lax.axis_index("core")

        def consume(t, hd, rows):
            t_vec = jnp.full((SC_LANES,), t, jnp.int32)
            ws = [_as_bf16(plsc.load_gather(w_v, [t_vec, jnp.full((SC_LANES,), hd * TOPK + k, jnp.int32)]))
                  for k in range(TOPK)]

            @plsc.parallel_loop(0, SC_STEPS)
            def _(s):
                parts = [_widen_sum([_as_bf16(_row_run(rows, k, s)) * ws[k] for k in range(g, g + SC_BF16_SUM)])
                         for g in range(0, TOPK, SC_BF16_SUM)]
                for half, c in enumerate((s * SC_LANES, SC_PAIR_WORDS + s * SC_LANES)):
                    sl = pl.ds(c, SC_LANES)
                    terms = [p[half] for p in parts]
                    if hd > 0:
                        terms.append(out_v[t, sl])
                    out_v[t, sl] = _tree_sum(terms)

        def block(base, n):
            pltpu.sync_copy(idx_hbm.at[pl.ds(base, n)], idx_v.at[pl.ds(0, n)])
            pltpu.sync_copy(w_hbm.at[pl.ds(base, n)], w_v.at[pl.ds(0, n)])
            _gather_ring(tab_hbm, idx_v, rows_v, sems, consume, n)
            pltpu.sync_copy(out_v.at[pl.ds(0, n)], out_hbm.at[pl.ds(base, n)])

        _for_token_blocks(wid * per_worker, per_worker, block)

    return pl.kernel(
        body,
        out_type=jax.ShapeDtypeStruct((N, D), jnp.float32),
        mesh=mesh,
        scratch_types=[
            pltpu.VMEM((SC_TOKENS, N_SEL), jnp.int32),
            pltpu.VMEM((SC_TOKENS, N_SEL), jnp.int32),
            pltpu.VMEM((SC_RING, TOPK) + SC_ROW_TILE, jnp.int32),
            pltpu.VMEM((SC_TOKENS, D), jnp.float32),
            pltpu.SemaphoreType.DMA((SC_RING,)),
        ],
        compiler_params=pltpu.CompilerParams(needs_layout_passes=False),
        name="peer_up",
    )(w_dup, idx, packed)


def _peer_tc_kernel(idx_cur, idx_nxt, h_ref, gate_ref, tab_hbm, o_ref, gbuf, sems):
    i = pl.program_id(0)
    n = pl.num_programs(0)
    groups = N_SEL // 8

    def issue(idx_ref, slot):
        def body(g, carry):
            for j in range(8):
                e = idx_ref[g * 8 + j]
                pltpu.make_async_copy(
                    tab_hbm.at[pl.ds(e, 1), :], gbuf.at[slot, g, pl.ds(j, 1), :], sems.at[slot]).start(priority=j % 2)
            return carry
        lax.fori_loop(0, TC_TOKENS * groups, body, 0)

    @pl.when(i == 0)
    def _():
        issue(idx_cur, 0)

    @pl.when(i + 1 < n)
    def _():
        issue(idx_nxt, (i + 1) % 2)

    slot = i % 2
    pltpu.make_async_copy(gbuf.at[slot], gbuf.at[slot], sems.at[slot]).wait()

    half = SC_PAIR_WORDS

    def rows(t, c0):
        w = gbuf[slot, t * groups:(t + 1) * groups, :, c0:c0 + half].reshape(N_SEL, half)
        lo = lax.bitcast_convert_type(lax.shift_left(w, jnp.int32(16)), jnp.float32)
        hi = lax.bitcast_convert_type(lax.bitwise_and(w, jnp.int32(-65536)), jnp.float32)
        return lo, hi

    lane = lax.broadcasted_iota(jnp.int32, (N_SEL, 128), 1)
    pre_all = jnp.zeros((N_SEL, 128), jnp.float32)
    for t in range(TC_TOKENS):
        lo, hi = rows(t, 0)
        prod = lo * h_ref[t:t + 1, 0:half] + hi * h_ref[t:t + 1, half:D_MODEL]
        part = prod[:, 0:128]
        for c in range(1, half // 128):
            part = part + prod[:, c * 128:(c + 1) * 128]
        pre = jnp.sum(part, axis=1, keepdims=True)
        pre_all = jnp.where(lane == t, pre, pre_all)
    act = 0.5 * pre_all * (1.0 + lax.erf(pre_all * (1.0 / math.sqrt(2.0))))
    gate_t = jnp.concatenate([gate_ref[...]] * (128 // TC_TOKENS), axis=0).T
    wgt = gate_t * act
    for t in range(TC_TOKENS):
        w_col = jnp.sum(jnp.where(lane == t, wgt, 0.0), axis=1, keepdims=True)
        lo, hi = rows(t, half)
        o_ref[t:t + 1, 0:half] = jnp.sum(lo * w_col, axis=0, keepdims=True)
        o_ref[t:t + 1, half:D_MODEL] = jnp.sum(hi * w_col, axis=0, keepdims=True)


def _peer_tc(h2, h_row0, idx, gate, table2, row0, n_tokens):
    D = h2.shape[1]
    steps = n_tokens // TC_TOKENS
    blk0 = row0 // TC_TOKENS
    h_blk0 = (h_row0 + row0) // TC_TOKENS
    idx_flat = idx.reshape(-1)
    return pl.pallas_call(
        _peer_tc_kernel,
        grid=(steps,),
        in_specs=[
            pl.BlockSpec((TC_TOKENS * N_SEL,), lambda i: (blk0 + i,), memory_space=pltpu.SMEM),
            pl.BlockSpec((TC_TOKENS * N_SEL,), lambda i: (blk0 + jnp.minimum(i + 1, steps - 1),),
                         memory_space=pltpu.SMEM),
            pl.BlockSpec((TC_TOKENS, D), lambda i: (h_blk0 + i, 0)),
            pl.BlockSpec((TC_TOKENS, N_SEL), lambda i: (blk0 + i, 0)),
            pl.BlockSpec(memory_space=pl.ANY),
        ],
        out_specs=pl.BlockSpec((TC_TOKENS, D), lambda i: (i, 0)),
        out_shape=jax.ShapeDtypeStruct((n_tokens, D), jnp.float32),
        scratch_shapes=[
            pltpu.VMEM((2, TC_TOKENS * N_SEL // 8, 8, 2 * SC_PAIR_WORDS), jnp.int32),
            pltpu.SemaphoreType.DMA((2,)),
        ],
        compiler_params=pltpu.CompilerParams(
            dimension_semantics=("arbitrary",), vmem_limit_bytes=VMEM_LIMIT_BYTES),
        name="peer_tc",
    )(idx_flat, idx_flat, h2, gate, table2)


def _gate_act_kernel(pre_ref, gate_ref, o_ref):
    p = pre_ref[...]
    act = 0.5 * p * (1.0 + lax.erf(p * (1.0 / math.sqrt(2.0))))
    o_ref[...] = gate_ref[...] * act


def _gate_act(pre, gate):
    N = pre.shape[0]
    spec = pl.BlockSpec((EW_BLOCK, N_SEL), lambda i: (i, 0))
    return pl.pallas_call(
        _gate_act_kernel, grid=(N // EW_BLOCK,), in_specs=[spec, spec], out_specs=spec,
        out_shape=jax.ShapeDtypeStruct((N, N_SEL), jnp.float32), name="gate_act",
    )(pre, gate)


def _final_kernel(x1_ref, mod_ref, g_ref, *refs, normalize, starts):
    peer_refs, o_ref = refs[:-1], refs[-1]
    gate2 = mod_ref[0, 5:6, :]
    peer = peer_refs[0][...]
    for p in range(1, len(peer_refs)):
        peer = jnp.where(pl.program_id(0) >= starts[p], peer_refs[p][...], peer)
    y = x1_ref[0] + gate2 * peer
    o_ref[...] = _rms(y, g_ref[...]) if normalize else y


def _residual(x1, row0, peers, b, mod3, g_final, normalize):
    D = x1.shape[2]
    blocks = [p.shape[0] // EW_BLOCK for p in peers]
    starts = [sum(blocks[:p]) for p in range(len(peers))]
    blk0 = row0 // EW_BLOCK

    def piece_spec(start, n):
        return pl.BlockSpec((EW_BLOCK, D), lambda j: (jnp.clip(j - start, 0, n - 1), 0))

    return pl.pallas_call(
        functools.partial(_final_kernel, normalize=normalize, starts=tuple(starts)),
        grid=(sum(blocks),),
        in_specs=[pl.BlockSpec((1, EW_BLOCK, D), lambda j: (0, blk0 + j, 0)),
                  pl.BlockSpec((1, N_MOD, D), lambda j: (b, 0, 0)),
                  pl.BlockSpec((1, D), lambda j: (0, 0))]
                 + [piece_spec(s, n) for s, n in zip(starts, blocks)],
        out_specs=pl.BlockSpec((EW_BLOCK, D), lambda j: (j, 0)),
        out_shape=jax.ShapeDtypeStruct((sum(blocks) * EW_BLOCK, D), jnp.float32),
        compiler_params=pltpu.CompilerParams(vmem_limit_bytes=VMEM_LIMIT_BYTES),
        name="final_norm",
    )(x1, mod3, g_final, *peers)


def _token_split(u, n_units, n):
    unit = n // 16
    if u == n_units - 1 and n_units > 1:
        sc, c = 12, 2
    elif u == 0:
        sc, c = 14, 0
    else:
        sc, c = 13, 0
    n_sc, n_c = sc * unit, c * unit
    n_a = ((n - n_sc - n_c) // 2) // EW_BLOCK * EW_BLOCK
    return n_sc, n_a, n - n_sc - n_c - n_a, n_c


def _token_ranges(n_seq, S):
    return [(b, 0, S) for b in range(n_seq)]


def kernel(x, c, w_ada, b_ada, g_norm1, w_in, w_pool, pool_scale, sgu_ln_g, sgu_ln_b, w_spatial, b_spatial,
           w_out, g_norm2, w_query, sub_keys, expert_down, expert_up, g_final):
    B, S, D = x.shape
    depth = w_ada.shape[0]
    bf16 = jnp.bfloat16
    band = jnp.asarray(_pool_band(), bf16)
    c_pad = jnp.pad(c, ((0, 8 - B), (0, 0)))
    for l in range(depth):
        mod = _ada_mod(c_pad, w_ada[l].astype(bf16), b_ada[l][None, :])[:B]
        mod3 = mod.reshape(B, N_MOD, D)
        mix_w = (g_norm1[l][None, :], w_in[l].astype(bf16), band, w_pool[l].astype(bf16),
                 pool_scale[l][None, :], sgu_ln_g[l][None, :], sgu_ln_b[l][None, :], w_spatial[l],
                 b_spatial[l].T, w_out[l].astype(bf16), g_norm2[l][None, :])
        keys = sub_keys[l].reshape(2 * PEER_HEADS, N_KEYS, HALF_KEY).astype(bf16)
        wq = w_query[l].astype(bf16)
        down_packed = _pack_rows(expert_down[l])
        up_packed = _pack_rows(expert_up[l])
        table2 = jnp.concatenate([down_packed, up_packed], axis=1)
        down_sc, up_sc = _sc_table(down_packed), _sc_table(up_packed)
        last = l + 1 == depth
        outs = []
        pending = None
        mod3_b = mod3
        seq_done = -1
        ranges = _token_ranges(B, S)
        for u, (b, r0, n) in enumerate(ranges):
            n_sc, n_a, n_b, n_c = _token_split(u, len(ranges), n)
            if b != seq_done:
                x1, h2, h2p = _mixer(x, b, 1, mod3_b, *mix_w)
                h2f = h2.reshape(S, D)
                h2pf = h2p.reshape(S, D // 2)
                seq_done = b
            idx, gate = _query_topk(h2f, wq, keys, r0, n)
            tc_a = _peer_tc(h2f, r0, idx, gate, table2, n_sc, n_a)
            pre = _peer_down(h2pf if r0 == 0 else h2pf[r0:], idx, down_sc, n_sc)
            gate_b = gate
            if pending is not None:
                p_x1, p_r0, p_peers, p_seq = pending
                p_peers[0], tc_a = lax.optimization_barrier((p_peers[0], tc_a))
                out_prev = _residual(p_x1, p_r0, p_peers, p_seq, mod3, g_final[None, :], last)
                outs.append(out_prev)
                gate_b, _ = lax.optimization_barrier((gate, out_prev))
            tc_b = _peer_tc(h2f, r0, idx, gate_b, table2, n_sc + n_a, n_b)
            pre, tc_a, tc_b = lax.optimization_barrier((pre, tc_a, tc_b))
            wgt = _gate_act(pre, gate)
            peers = [_peer_up(_pack_dup(wgt), idx, up_sc, n_sc), tc_a, tc_b]
            if n_c:
                gate_c, _ = lax.optimization_barrier((gate, wgt))
                peers.append(_peer_tc(h2f, r0, idx, gate_c, table2, n_sc + n_a + n_b, n_c))
            mod3_b, _ = lax.optimization_barrier((mod3, wgt))
            pending = (x1, r0, peers, b)
        p_x1, p_r0, p_peers, p_seq = pending
        outs.append(_residual(p_x1, p_r0, p_peers, p_seq, mod3, g_final[None, :], last))
        x = jnp.concatenate(outs, axis=0).reshape(B, S, D)
    return x
```
